```python
import math
import jax
import jax.numpy as jnp
from jax import lax
import numpy as np

D_MODEL = 2048
BATCH = 1
SEQ = 16384
DEPTH = 1

D_MIX = D_MODEL
D_POOL = D_MIX // 2
POOL_WINDOWS = (2, 4, 8, 16)
N_POOL_GROUPS = len(POOL_WINDOWS)
POOL_GROUP = D_POOL // N_POOL_GROUPS
D_ATTN = D_MIX - D_POOL
HEAD_DIM = 128
N_HEADS = D_ATTN // HEAD_DIM
IDX_HEADS = 16
IDX_DIM = 64
TOPK_MAX = 256
Q_BLOCK = 128
D_FF = 5632
CONV_WIDTH = 3
EPS = 1e-6

D_IN = D_POOL + 3 * D_ATTN + IDX_HEADS * IDX_DIM + IDX_DIM + IDX_HEADS

kernel_name = "hybrid_pool_dsa_convffn_block"


def rmsnorm(x, g):
    xf = x.astype(jnp.float32)
    y = xf * lax.rsqrt(jnp.mean(xf * xf, axis=-1, keepdims=True) + EPS)
    return (y * g.astype(jnp.float32)).astype(x.dtype)


def pool_mixer(u, pool_w, pool_scale):
    B, S, C = u.shape
    uf = u.astype(jnp.float32)
    csum = jnp.concatenate([jnp.zeros((B, 1, C), jnp.float32), jnp.cumsum(uf, axis=1)], axis=1)
    t = jnp.arange(S)
    outs = []
    for gi, w in enumerate(POOL_WINDOWS):
        lo = jnp.maximum(t + 1 - w, 0)
        cnt = (t + 1 - lo).astype(jnp.float32)[None, :, None]
        c_g = csum[:, :, gi * POOL_GROUP:(gi + 1) * POOL_GROUP]
        mean = (c_g[:, 1:] - c_g[:, lo]) / cnt
        outs.append(mean - uf[:, :, gi * POOL_GROUP:(gi + 1) * POOL_GROUP])
    d = jnp.stack(outs, axis=2).astype(u.dtype)
    y = jnp.einsum('bsgc,gcd->bsgd', d, pool_w).reshape(B, S, C)
    return y * pool_scale


def dsa_attention(q, k, v, q_idx, k_idx, w_idx):
    B, S, H, Dh = q.shape
    topk = min(TOPK_MAX, S // 4)
    n_blocks = S // Q_BLOCK
    key_pos = jnp.arange(S)
    scale = Dh ** -0.5
    neg = jnp.finfo(jnp.float32).min

    def one_block(blk):
        start = blk * Q_BLOCK
        qb = lax.dynamic_slice_in_dim(q, start, Q_BLOCK, axis=1)
        qib = lax.dynamic_slice_in_dim(q_idx, start, Q_BLOCK, axis=1)
        wb = lax.dynamic_slice_in_dim(w_idx, start, Q_BLOCK, axis=1)
        q_pos = start + jnp.arange(Q_BLOCK)
        rel = jax.nn.relu(jnp.einsum('bthd,bsd->bths', qib.astype(jnp.float32), k_idx.astype(jnp.float32)))
        iscore = jnp.einsum('bths,bth->bts', rel, wb.astype(jnp.float32))
        causal = key_pos[None, :] <= q_pos[:, None]
        iscore = jnp.where(causal[None], iscore, neg)
        _, sel = lax.top_k(iscore, topk)
        valid = sel <= q_pos[None, :, None]
        k_sel = jax.vmap(lambda kb, ib: kb[ib])(k, sel)
        v_sel = jax.vmap(lambda vb, ib: vb[ib])(v, sel)
        s = jnp.einsum('bthd,btjhd->bhtj', qb, k_sel).astype(jnp.float32) * scale
        s = jnp.where(valid[:, None], s, neg)
        p = jax.nn.softmax(s, axis=-1).astype(v.dtype)
        return jnp.einsum('bhtj,btjhd->bthd', p, v_sel)

    out = lax.map(one_block, jnp.arange(n_blocks, dtype=jnp.int32))
    return jnp.transpose(out, (1, 0, 2, 3, 4)).reshape(B, S, H * Dh)


def conv_ffn(h, w_up, conv_w, conv_b, w_down):
    B, S, _ = h.shape
    up = h @ w_up
    up_p = jnp.pad(up, ((0, 0), (CONV_WIDTH - 1, 0), (0, 0)))
    c = conv_b + sum(conv_w[j] * up_p[:, j:j + S] for j in range(CONV_WIDTH))
    gate, val = jnp.split(c, 2, axis=-1)
    return (jax.nn.silu(gate) * val) @ w_down


def setup_inputs(seed: int = 0) -> dict:
    key = jax.random.key(seed)
    ks = jax.random.split(key, 14)
    f32 = jnp.float32
    nrm = lambda k, shape, s: jax.random.normal(k, shape, f32) * s
    return {
        "x": nrm(ks[0], (BATCH, SEQ, D_MODEL), 1.0),
        "attn_norm_g": 1.0 + nrm(ks[1], (D_MODEL,), 0.02),
        "w_in": nrm(ks[2], (D_MODEL, D_IN), D_MODEL ** -0.5),
        "pool_w": nrm(ks[3], (N_POOL_GROUPS, POOL_GROUP, POOL_GROUP), POOL_GROUP ** -0.5),
        "pool_scale": 1.0 + nrm(ks[4], (D_POOL,), 0.02),
        "q_norm_g": 1.0 + nrm(ks[5], (HEAD_DIM,), 0.02),
        "k_norm_g": 1.0 + nrm(ks[6], (HEAD_DIM,), 0.02),
        "w_out": nrm(ks[7], (D_MIX, D_MODEL), D_MIX ** -0.5),
        "ffn_norm_g": 1.0 + nrm(ks[8], (D_MODEL,), 0.02),
        "w_up": nrm(ks[9], (D_MODEL, 2 * D_FF), D_MODEL ** -0.5),
        "conv_w": nrm(ks[10], (CONV_WIDTH, 2 * D_FF), CONV_WIDTH ** -0.5),
        "conv_b": nrm(ks[11], (2 * D_FF,), 0.01),
        "w_down": nrm(ks[12], (D_FF, D_MODEL), D_FF ** -0.5),
    }


def reference(x, attn_norm_g, w_in, pool_w, pool_scale, q_norm_g, k_norm_g, w_out,
              ffn_norm_g, w_up, conv_w, conv_b, w_down):
    B, S, _ = x.shape
    for _layer in range(DEPTH):
        h = rmsnorm(x, attn_norm_g)
        z = h @ w_in
        o = 0
        u_pool = z[..., o:o + D_POOL]; o += D_POOL
        q = z[..., o:o + D_ATTN].reshape(B, S, N_HEADS, HEAD_DIM); o += D_ATTN
        k = z[..., o:o + D_ATTN].reshape(B, S, N_HEADS, HEAD_DIM); o += D_ATTN
        v = z[..., o:o + D_ATTN].reshape(B, S, N_HEADS, HEAD_DIM); o += D_ATTN
        q_idx = z[..., o:o + IDX_HEADS * IDX_DIM].reshape(B, S, IDX_HEADS, IDX_DIM); o += IDX_HEADS * IDX_DIM
        k_idx = z[..., o:o + IDX_DIM]; o += IDX_DIM
        w_idx = z[..., o:o + IDX_HEADS] * (IDX_HEADS ** -0.5) * (IDX_DIM ** -0.5)
        q = rmsnorm(q, q_norm_g)
        k = rmsnorm(k, k_norm_g)
        y_pool = pool_mixer(u_pool, pool_w, pool_scale)
        y_attn = dsa_attention(q, k, v, q_idx, k_idx, w_idx)
        x = x + jnp.concatenate([y_pool, y_attn], axis=-1) @ w_out
        x = x + conv_ffn(rmsnorm(x, ffn_norm_g), w_up, conv_w, conv_b, w_down)
    return x
```

```python
import functools

import numpy as np
import jax
import jax.numpy as jnp
from jax import lax
from jax.experimental import pallas as pl
from jax.experimental.pallas import tpu as pltpu

D_MODEL = 2048
D_POOL = 1024
POOL_WINDOWS = (2, 4, 8, 16)
POOL_GROUP = D_POOL // len(POOL_WINDOWS)
D_ATTN = 1024
HEAD_DIM = 128
N_HEADS = D_ATTN // HEAD_DIM
IDX_HEADS = 16
IDX_DIM = 64
TOPK_MAX = 256
D_FF = 5632
CONV_WIDTH = 3
EPS = 1e-6

D_MAIN = D_POOL + 3 * D_ATTN + IDX_HEADS * IDX_DIM
D_TAIL = IDX_DIM + IDX_HEADS
LANES = 128
HALO = 16
MASK_NEG = -1e30
VMEM_LIMIT_BYTES = 56 * 1024 * 1024

F32 = jnp.float32
BF16 = jnp.bfloat16


def _params(n_axes):
    return pltpu.CompilerParams(dimension_semantics=("arbitrary",) * n_axes,
                                vmem_limit_bytes=VMEM_LIMIT_BYTES)


def _rms(xf, g):
    return xf * lax.rsqrt(jnp.mean(xf * xf, axis=-1, keepdims=True) + EPS) * g


def _inproj_kernel(x_ref, g_ref, wm_ref, wt_ref, qg_ref, kg_ref, zm_ref, zt_ref, h_ref, *, tn):
    j = pl.program_id(1)
    q_lo, k_lo, k_hi = D_POOL // tn, (D_POOL + D_ATTN) // tn, (D_POOL + 2 * D_ATTN) // tn

    @pl.when(j == 0)
    def _():
        h = _rms(x_ref[...], g_ref[...]).astype(BF16)
        h_ref[...] = h
        zt_ref[...] = jnp.dot(h, wt_ref[...], preferred_element_type=F32)

    z = jnp.dot(h_ref[...], wm_ref[...], preferred_element_type=F32)
    is_qk = (j >= q_lo) & (j < k_hi)

    @pl.when(is_qk)
    def _():
        g = jnp.where(j < k_lo, qg_ref[...], kg_ref[...])
        for c in range(tn // HEAD_DIM):
            sl = slice(c * HEAD_DIM, (c + 1) * HEAD_DIM)
            zm_ref[:, sl] = _rms(z[:, sl], g).astype(BF16)

    @pl.when(jnp.logical_not(is_qk))
    def _():
        zm_ref[...] = z.astype(BF16)


def _inproj(x2, g, w_main, w_tail, qg, kg, *, tm, tn):
    S = x2.shape[0]
    return pl.pallas_call(
        functools.partial(_inproj_kernel, tn=tn),
        out_shape=(jax.ShapeDtypeStruct((S, D_MAIN), BF16), jax.ShapeDtypeStruct((S, LANES), F32)),
        grid=(S // tm, D_MAIN // tn),
        in_specs=[
            pl.BlockSpec((tm, D_MODEL), lambda i, j: (i, 0)),
            pl.BlockSpec((1, D_MODEL), lambda i, j: (0, 0)),
            pl.BlockSpec((D_MODEL, tn), lambda i, j: (0, j)),
            pl.BlockSpec((D_MODEL, LANES), lambda i, j: (0, 0)),
            pl.BlockSpec((1, HEAD_DIM), lambda i, j: (0, 0)),
            pl.BlockSpec((1, HEAD_DIM), lambda i, j: (0, 0)),
        ],
        out_specs=(pl.BlockSpec((tm, tn), lambda i, j: (i, j)),
                   pl.BlockSpec((tm, LANES), lambda i, j: (i, 0))),
        scratch_shapes=[pltpu.VMEM((tm, D_MODEL), BF16)],
        compiler_params=_params(2),
        name="inproj",
    )(x2, g, w_main, w_tail, qg, kg)


def _pool_kernel(u_ref, halo_ref, pw_ref, ps_ref, o_ref):
    i = pl.program_id(0)
    tm = u_ref.shape[0]
    u = u_ref[...].astype(F32)
    halo = jnp.where(i == 0, 0.0, halo_ref[...].astype(F32))
    ext = jnp.concatenate([halo, u], axis=0)
    t = i * tm + lax.broadcasted_iota(jnp.int32, (tm, 1), 0)
    for gi, w in enumerate(POOL_WINDOWS):
        sl = slice(gi * POOL_GROUP, (gi + 1) * POOL_GROUP)
        s = ext[:, sl]
        step = 1
        while step < w:
            s = s + pltpu.roll(s, step, axis=0)
            step *= 2
        cnt = jnp.minimum(t + 1, w).astype(F32)
        d = s[HALO:, :] / cnt - u[:, sl]
        y = jnp.dot(d.astype(BF16), pw_ref[gi], preferred_element_type=F32)
        o_ref[:, sl] = (y * ps_ref[:, sl]).astype(BF16)


def _pool(zm, pool_w, pool_scale, *, tm):
    S = zm.shape[0]
    return pl.pallas_call(
        _pool_kernel,
        out_shape=jax.ShapeDtypeStruct((S, D_POOL), BF16),
        grid=(S // tm,),
        in_specs=[
            pl.BlockSpec((tm, D_POOL), lambda i: (i, 0)),
            pl.BlockSpec((HALO, D_POOL), lambda i: (jnp.maximum(i * (tm // HALO) - 1, 0), 0)),
            pl.BlockSpec((len(POOL_WINDOWS), POOL_GROUP, POOL_GROUP), lambda i: (0, 0, 0)),
            pl.BlockSpec((1, D_POOL), lambda i: (0, 0)),
        ],
        out_specs=pl.BlockSpec((tm, D_POOL), lambda i: (i, 0)),
        compiler_params=_params(1),
        name="pool",
    )(zm, zm, pool_w, pool_scale)


def _ordered_int_to_float(k):
    return lax.bitcast_convert_type(k ^ ((k >> 31) & jnp.int32(0x7FFFFFFF)), F32)


def _index_kernel(kidx_ref, qiT_ref, wT_ref, bias_ref, row_ref, *, T, CK, S, topk):
    qb = pl.program_id(0)
    nck = (qb * T + T + CK - 1) // CK
    q_pos = qb * T + lax.broadcasted_iota(jnp.int32, (1, T), 1)

    def causal(off):
        return (off + lax.broadcasted_iota(jnp.int32, (CK, 1), 0)) <= q_pos

    for h in range(IDX_HEADS):
        qh = qiT_ref[h * LANES:(h + 1) * LANES, :]
        wh = wT_ref[h:h + 1, :]

        def score_body(c, carry, h=h, qh=qh, wh=wh):
            off = pl.multiple_of(c * CK, CK)
            s = jnp.dot(kidx_ref[pl.ds(off, CK), :], qh, preferred_element_type=F32)
            new = wh * jnp.maximum(s, 0.0)
            if h > 0:
                new = row_ref[pl.ds(off, CK), :] + new
            if h == IDX_HEADS - 1:
                new = jnp.where(causal(off), new, -jnp.inf)
            row_ref[pl.ds(off, CK), :] = new
            return carry

        lax.fori_loop(0, nck, score_body, 0)

    int_min = jnp.int32(-2 ** 31)

    def bit_body(i, v):
        cand = v + (jnp.int32(1) << (31 - i))
        cf = _ordered_int_to_float(cand)

        def count_body(c, acc):
            off = pl.multiple_of(c * CK, CK)
            ge = (row_ref[pl.ds(off, CK), :] >= cf).astype(jnp.int32)
            return acc + ge.reshape(CK // 8, 8, T).sum(axis=0)

        acc = lax.fori_loop(0, nck, count_body, jnp.zeros((8, T), jnp.int32))
        return jnp.where(acc.sum(axis=0, keepdims=True) >= topk, cand, v)

    v = lax.fori_loop(0, 32, bit_body, jnp.full((1, T), int_min, jnp.int32))
    thr = jnp.where(v == int_min, -jnp.inf, _ordered_int_to_float(v))

    def emit_body(c, carry):
        off = pl.multiple_of(c * CK, CK)
        sel = (row_ref[pl.ds(off, CK), :] >= thr) & causal(off)
        bias_ref[0, pl.ds(off, CK), :] = jnp.where(sel, 0.0, MASK_NEG).astype(BF16)
        return carry

    lax.fori_loop(0, nck, emit_body, 0)

    def fill_body(c, carry):
        off = pl.multiple_of(c * CK, CK)
        bias_ref[0, pl.ds(off, CK), :] = jnp.full((CK, T), MASK_NEG, BF16)
        return carry

    lax.fori_loop(nck, S // CK, fill_body, 0)


def _index(kidx, qiT, wT, *, T, CK, topk):
    S = kidx.shape[0]
    return pl.pallas_call(
        functools.partial(_index_kernel, T=T, CK=CK, S=S, topk=topk),
        out_shape=jax.ShapeDtypeStruct((S // T, S, T), BF16),
        grid=(S // T,),
        in_specs=[
            pl.BlockSpec((S, LANES), lambda q: (0, 0)),
            pl.BlockSpec((IDX_HEADS * LANES, T), lambda q: (0, q)),
            pl.BlockSpec((IDX_HEADS, T), lambda q: (0, q)),
        ],
        out_specs=pl.BlockSpec((1, S, T), lambda q: (q, 0, 0)),
        scratch_shapes=[pltpu.VMEM((S, T), F32)],
        compiler_params=_params(1),
        name="index",
    )(kidx, qiT, wT)


def _attn_kernel(qb_ref, kb_ref, kn_ref, qT_ref, vT_ref, bias_ref, o_ref, m_ref, l_ref, acc_ref,
                 *, T, KB, scale):
    i = pl.program_id(0)
    qb = qb_ref[i]
    kb = kb_ref[i]

    @pl.when(kb == 0)
    def _():
        m_ref[...] = jnp.full(m_ref.shape, MASK_NEG, F32)
        l_ref[...] = jnp.zeros(l_ref.shape, F32)
        acc_ref[...] = jnp.zeros(acc_ref.shape, F32)

    bias = bias_ref[0].astype(F32)
    for h in range(N_HEADS):
        hs = slice(h * HEAD_DIM, (h + 1) * HEAD_DIM)
        s = jnp.dot(kn_ref[:, hs], qT_ref[hs, :], preferred_element_type=F32) * scale + bias
        m_old = m_ref[h]
        m_new = jnp.maximum(m_old, s.max(axis=0, keepdims=True))
        alpha = jnp.exp(m_old - m_new)
        p = jnp.exp(s - m_new)
        l_ref[h] = alpha * l_ref[h] + p.sum(axis=0, keepdims=True)
        pv = jnp.dot(vT_ref[hs, :], p.astype(BF16), preferred_element_type=F32)
        acc_ref[h] = alpha * acc_ref[h] + pv
        m_ref[h] = m_new

    @pl.when(kb == (qb * T + T - 1) // KB)
    def _():
        for h in range(N_HEADS):
            o_ref[h * HEAD_DIM:(h + 1) * HEAD_DIM, :] = (acc_ref[h] / l_ref[h]).astype(BF16)


def _attn(kn, qT, vT, bias, *, T, KB):
    S = kn.shape[0]
    pairs = [(q, k) for q in range(S // T) for k in range((q * T + T - 1) // KB + 1)]
    qb_ids = jnp.asarray(np.array([p[0] for p in pairs], np.int32))
    kb_ids = jnp.asarray(np.array([p[1] for p in pairs], np.int32))
    grid_spec = pltpu.PrefetchScalarGridSpec(
        num_scalar_prefetch=2,
        grid=(len(pairs),),
        in_specs=[
            pl.BlockSpec((KB, D_ATTN), lambda i, qb, kb: (kb[i], 0)),
            pl.BlockSpec((D_ATTN, T), lambda i, qb, kb: (0, qb[i])),
            pl.BlockSpec((D_ATTN, KB), lambda i, qb, kb: (0, kb[i])),
            pl.BlockSpec((1, KB, T), lambda i, qb, kb: (qb[i], kb[i], 0)),
        ],
        out_specs=pl.BlockSpec((D_ATTN, T), lambda i, qb, kb: (0, qb[i])),
        scratch_shapes=[pltpu.VMEM((N_HEADS, 1, T), F32), pltpu.VMEM((N_HEADS, 1, T), F32),
                        pltpu.VMEM((N_HEADS, HEAD_DIM, T), F32)],
    )
    return pl.pallas_call(
        functools.partial(_attn_kernel, T=T, KB=KB, scale=HEAD_DIM ** -0.5),
        out_shape=jax.ShapeDtypeStruct((D_ATTN, S), BF16),
        grid_spec=grid_spec,
        compiler_params=_params(1),
        name="attn",
    )(qb_ids, kb_ids, kn, qT, vT, bias)


def _outproj_kernel(yp_ref, ya_ref, wp_ref, wa_ref, x_ref, o_ref):
    acc = jnp.dot(yp_ref[...], wp_ref[...], preferred_element_type=F32)
    acc = acc + jnp.dot(ya_ref[...], wa_ref[...], preferred_element_type=F32)
    o_ref[...] = x_ref[...] + acc


def _outproj(yp, ya, w_out, x2, *, tm, tn):
    S = x2.shape[0]
    n_j = D_MODEL // tn
    return pl.pallas_call(
        _outproj_kernel,
        out_shape=jax.ShapeDtypeStruct((S, D_MODEL), F32),
        grid=(S // tm, n_j),
        in_specs=[
            pl.BlockSpec((tm, D_POOL), lambda i, j: (i, 0)),
            pl.BlockSpec((tm, D_ATTN), lambda i, j: (i, 0)),
            pl.BlockSpec((D_POOL, tn), lambda i, j: (0, j)),
            pl.BlockSpec((D_ATTN, tn), lambda i, j: (D_POOL // D_ATTN, j)),
            pl.BlockSpec((tm, tn), lambda i, j: (i, j)),
        ],
        out_specs=pl.BlockSpec((tm, tn), lambda i, j: (i, j)),
        compiler_params=_params(2),
        name="outproj",
    )(yp, ya, w_out, w_out, x2)


def _ffn_kernel(x_ref, halo_ref, g_ref, wg_ref, wv_ref, cwg_ref, cwv_ref, cbg_ref, cbv_ref, wd_ref,
                o_ref, h_ref, acc_ref):
    i = pl.program_id(0)
    f = pl.program_id(1)

    @pl.when(f == 0)
    def _():
        hh = jnp.where(i == 0, 0.0, _rms(halo_ref[...], g_ref[...]))
        h_ref[0:HALO, :] = hh.astype(BF16)
        h_ref[HALO:, :] = _rms(x_ref[...], g_ref[...]).astype(BF16)
        acc_ref[...] = jnp.zeros(acc_ref.shape, F32)

    h = h_ref[...]

    def conv(w_ref, cw_ref, cb_ref):
        up = jnp.dot(h, w_ref[...], preferred_element_type=F32)
        c = (cw_ref[0:1, :] * pltpu.roll(up, 2, axis=0) + cw_ref[1:2, :] * pltpu.roll(up, 1, axis=0)
             + cw_ref[2:3, :] * up)
        return cb_ref[...] + c[HALO:, :]

    cg = conv(wg_ref, cwg_ref, cbg_ref)
    cv = conv(wv_ref, cwv_ref, cbv_ref)
    act = cg * (1.0 / (1.0 + jnp.exp(-cg))) * cv
    acc_ref[...] += jnp.dot(act.astype(BF16), wd_ref[...], preferred_element_type=F32)

    @pl.when(f == pl.num_programs(1) - 1)
    def _():
        o_ref[...] = x_ref[...] + acc_ref[...]


def _ffn(x1, g, w_up, conv_w, conv_b, w_down, *, tm, tf):
    S = x1.shape[0]
    n_f = D_FF // tf
    return pl.pallas_call(
        _ffn_kernel,
        out_shape=jax.ShapeDtypeStruct((S, D_MODEL), F32),
        grid=(S // tm, n_f),
        in_specs=[
            pl.BlockSpec((tm, D_MODEL), lambda i, f: (i, 0)),
            pl.BlockSpec((HALO, D_MODEL), lambda i, f: (jnp.maximum(i * (tm // HALO) - 1, 0), 0)),
            pl.BlockSpec((1, D_MODEL), lambda i, f: (0, 0)),
            pl.BlockSpec((D_MODEL, tf), lambda i, f: (0, f)),
            pl.BlockSpec((D_MODEL, tf), lambda i, f: (0, f + n_f)),
            pl.BlockSpec((CONV_WIDTH, tf), lambda i, f: (0, f)),
            pl.BlockSpec((CONV_WIDTH, tf), lambda i, f: (0, f + n_f)),
            pl.BlockSpec((1, tf), lambda i, f: (0, f)),
            pl.BlockSpec((1, tf), lambda i, f: (0, f + n_f)),
            pl.BlockSpec((tf, D_MODEL), lambda i, f: (f, 0)),
        ],
        out_specs=pl.BlockSpec((tm, D_MODEL), lambda i, f: (i, 0)),
        scratch_shapes=[pltpu.VMEM((HALO + tm, D_MODEL), BF16), pltpu.VMEM((tm, D_MODEL), F32)],
        compiler_params=_params(2),
        name="ffn",
    )(x1, x1, g, w_up, w_up, conv_w, conv_w, conv_b, conv_b, w_down)


def kernel(x, attn_norm_g, w_in, pool_w, pool_scale, q_norm_g, k_norm_g, w_out, ffn_norm_g, w_up,
           conv_w, conv_b, w_down):
    B, S, D = x.shape
    assert B == 1 and D == D_MODEL and w_in.shape == (D_MODEL, D_MAIN + D_TAIL)
    T = 256
    assert S % 1024 == 0
    topk = min(TOPK_MAX, S // 4)
    x2 = x.reshape(S, D)

    w_main = w_in[:, :D_MAIN].astype(BF16)
    w_tail = jnp.pad(w_in[:, D_MAIN:], ((0, 0), (0, LANES - D_TAIL))).astype(BF16)
    zm, zt = _inproj(x2, attn_norm_g.reshape(1, D), w_main, w_tail, q_norm_g.reshape(1, HEAD_DIM),
                     k_norm_g.reshape(1, HEAD_DIM), tm=1024, tn=512)

    y_pool = _pool(zm, pool_w.astype(BF16), pool_scale.reshape(1, D_POOL), tm=1024)

    o = D_POOL
    qT = zm[:, o:o + D_ATTN].T
    kn = zm[:, o + D_ATTN:o + 2 * D_ATTN]
    vT = zm[:, o + 2 * D_ATTN:o + 3 * D_ATTN].T
    qi = zm[:, o + 3 * D_ATTN:].reshape(S, IDX_HEADS, IDX_DIM)
    qiT = jnp.pad(qi, ((0, 0), (0, 0), (0, LANES - IDX_DIM))).reshape(S, IDX_HEADS * LANES).T
    kidx = jnp.pad(zt[:, :IDX_DIM], ((0, 0), (0, LANES - IDX_DIM))).astype(BF16)
    wT = (zt[:, IDX_DIM:D_TAIL] * (IDX_HEADS ** -0.5) * (IDX_DIM ** -0.5)).T

    bias = _index(kidx, qiT, wT, T=T, CK=512, topk=topk)
    y_attn = _attn(kn, qT, vT, bias, T=T, KB=512).T

    x1 = _outproj(y_pool, y_attn, w_out.astype(BF16), x2, tm=1024, tn=1024)
    out = _ffn(x1, ffn_norm_g.reshape(1, D), w_up.astype(BF16), conv_w, conv_b.reshape(1, 2 * D_FF),
               w_down.astype(BF16), tm=512, tf=512)
    return out.reshape(B, S, D)
```

```python
import functools

import numpy as np
import jax
import jax.numpy as jnp
from jax import lax
from jax.experimental import pallas as pl
from jax.experimental.pallas import tpu as pltpu

D_MODEL = 2048
D_POOL = 1024
POOL_WINDOWS = (2, 4, 8, 16)
POOL_GROUP = D_POOL // len(POOL_WINDOWS)
D_ATTN = 1024
HEAD_DIM = 128
N_HEADS = D_ATTN // HEAD_DIM
IDX_HEADS = 16
IDX_DIM = 64
TOPK_MAX = 256
D_FF = 5632
CONV_WIDTH = 3
EPS = 1e-6

D_MAIN = D_POOL + 3 * D_ATTN + IDX_HEADS * IDX_DIM
D_TAIL = IDX_DIM + IDX_HEADS
LANES = 128
HALO = 16
MASK_NEG = -1e30
LOG2E = 1.4426950408889634
VMEM_LIMIT_BYTES = 56 * 1024 * 1024

F32 = jnp.float32
BF16 = jnp.bfloat16


def _params(n_axes):
    return pltpu.CompilerParams(dimension_semantics=("arbitrary",) * n_axes,
                                vmem_limit_bytes=VMEM_LIMIT_BYTES)


def _rms(xf, g):
    return xf * lax.rsqrt(jnp.mean(xf * xf, axis=-1, keepdims=True) + EPS) * g


def _inproj_kernel(x_ref, g_ref, wm_ref, wt_ref, qg_ref, kg_ref, zm_ref, zt_ref, h_ref, *, tn):
    j = pl.program_id(1)
    q_lo, k_lo, k_hi = D_POOL // tn, (D_POOL + D_ATTN) // tn, (D_POOL + 2 * D_ATTN) // tn

    @pl.when(j == 0)
    def _():
        h = _rms(x_ref[...], g_ref[...]).astype(BF16)
        h_ref[...] = h
        zt_ref[...] = jnp.dot(h, wt_ref[...], preferred_element_type=F32)

    z = jnp.dot(h_ref[...], wm_ref[...], preferred_element_type=F32)
    is_qk = (j >= q_lo) & (j < k_hi)

    @pl.when(is_qk)
    def _():
        g = jnp.where(j < k_lo, qg_ref[...], kg_ref[...])
        for c in range(tn // HEAD_DIM):
            sl = slice(c * HEAD_DIM, (c + 1) * HEAD_DIM)
            zm_ref[:, sl] = _rms(z[:, sl], g).astype(BF16)

    @pl.when(jnp.logical_not(is_qk))
    def _():
        zm_ref[...] = z.astype(BF16)


def _inproj(x2, g, w_main, w_tail, qg, kg, *, tm, tn):
    S = x2.shape[0]
    return pl.pallas_call(
        functools.partial(_inproj_kernel, tn=tn),
        out_shape=(jax.ShapeDtypeStruct((S, D_MAIN), BF16), jax.ShapeDtypeStruct((S, LANES), F32)),
        grid=(S // tm, D_MAIN // tn),
        in_specs=[
            pl.BlockSpec((tm, D_MODEL), lambda i, j: (i, 0)),
            pl.BlockSpec((1, D_MODEL), lambda i, j: (0, 0)),
            pl.BlockSpec((D_MODEL, tn), lambda i, j: (0, j)),
            pl.BlockSpec((D_MODEL, LANES), lambda i, j: (0, 0)),
            pl.BlockSpec((1, HEAD_DIM), lambda i, j: (0, 0)),
            pl.BlockSpec((1, HEAD_DIM), lambda i, j: (0, 0)),
        ],
        out_specs=(pl.BlockSpec((tm, tn), lambda i, j: (i, j)),
                   pl.BlockSpec((tm, LANES), lambda i, j: (i, 0))),
        scratch_shapes=[pltpu.VMEM((tm, D_MODEL), BF16)],
        compiler_params=_params(2),
        name="inproj",
    )(x2, g, w_main, w_tail, qg, kg)


def _pool_kernel(u_ref, halo_ref, pw_ref, ps_ref, o_ref):
    i = pl.program_id(0)
    tm = u_ref.shape[0]
    u = u_ref[...].astype(F32)
    halo = jnp.where(i == 0, 0.0, halo_ref[...].astype(F32))
    ext = jnp.concatenate([halo, u], axis=0)
    t = i * tm + lax.broadcasted_iota(jnp.int32, (tm, 1), 0)
    for gi, w in enumerate(POOL_WINDOWS):
        sl = slice(gi * POOL_GROUP, (gi + 1) * POOL_GROUP)
        s = ext[:, sl]
        step = 1
        while step < w:
            s = s + pltpu.roll(s, step, axis=0)
            step *= 2
        cnt = jnp.minimum(t + 1, w).astype(F32)
        d = s[HALO:, :] / cnt - u[:, sl]
        y = jnp.dot(d.astype(BF16), pw_ref[gi], preferred_element_type=F32)
        o_ref[:, sl] = (y * ps_ref[:, sl]).astype(BF16)


def _pool(zm, pool_w, pool_scale, *, tm):
    S = zm.shape[0]
    return pl.pallas_call(
        _pool_kernel,
        out_shape=jax.ShapeDtypeStruct((S, D_POOL), BF16),
        grid=(S // tm,),
        in_specs=[
            pl.BlockSpec((tm, D_POOL), lambda i: (i, 0)),
            pl.BlockSpec((HALO, D_POOL), lambda i: (jnp.maximum(i * (tm // HALO) - 1, 0), 0)),
            pl.BlockSpec((len(POOL_WINDOWS), POOL_GROUP, POOL_GROUP), lambda i: (0, 0, 0)),
            pl.BlockSpec((1, D_POOL), lambda i: (0, 0)),
        ],
        out_specs=pl.BlockSpec((tm, D_POOL), lambda i: (i, 0)),
        compiler_params=_params(1),
        name="pool",
    )(zm, zm, pool_w, pool_scale)


def _ordered_int_to_float(k):
    return lax.bitcast_convert_type(k ^ ((k >> 31) & jnp.int32(0x7FFFFFFF)), F32)


def _index_kernel(kidx_ref, qiT_ref, wT_ref, bias_ref, row_ref, *, T, CK, CS, S, topk):
    qb = pl.program_id(0)
    nck = (qb * T + T + CK - 1) // CK
    q_pos = qb * T + lax.broadcasted_iota(jnp.int32, (1, T), 1)

    def causal(off):
        return (off + lax.broadcasted_iota(jnp.int32, (CK, 1), 0)) <= q_pos

    def score_body(c, carry):
        off = pl.multiple_of(c * CS, CS)
        kc = kidx_ref[pl.ds(off, CS), :]
        acc = None
        for h in range(IDX_HEADS):
            s = jnp.dot(kc, qiT_ref[h * LANES:(h + 1) * LANES, :], preferred_element_type=F32)
            t = wT_ref[h:h + 1, :] * jnp.maximum(s, 0.0)
            acc = t if acc is None else acc + t
        key_pos = off + lax.broadcasted_iota(jnp.int32, (CS, 1), 0)
        row_ref[pl.ds(off, CS), :] = jnp.where(key_pos <= q_pos, acc, -jnp.inf)
        return carry

    lax.fori_loop(0, nck * (CK // CS), score_body, 0)

    int_min = jnp.int32(-2 ** 31)

    def bit_body(i, v):
        cand = v + (jnp.int32(1) << (31 - i))
        cf = _ordered_int_to_float(cand)

        def count_body(c, acc):
            off = pl.multiple_of(c * CK, CK)
            ge = (row_ref[pl.ds(off, CK), :] >= cf).astype(jnp.int32)
            return acc + ge.reshape(CK // 8, 8, T).sum(axis=0)

        acc = lax.fori_loop(0, nck, count_body, jnp.zeros((8, T), jnp.int32))
        return jnp.where(acc.sum(axis=0, keepdims=True) >= topk, cand, v)

    v = lax.fori_loop(0, 32, bit_body, jnp.full((1, T), int_min, jnp.int32))
    thr = jnp.where(v == int_min, -jnp.inf, _ordered_int_to_float(v))

    def emit_body(c, carry):
        off = pl.multiple_of(c * CK, CK)
        sel = (row_ref[pl.ds(off, CK), :] >= thr) & causal(off)
        bias_ref[0, pl.ds(off, CK), :] = jnp.where(sel, 0.0, MASK_NEG).astype(BF16)
        return carry

    lax.fori_loop(0, nck, emit_body, 0)

    def fill_body(c, carry):
        off = pl.multiple_of(c * CK, CK)
        bias_ref[0, pl.ds(off, CK), :] = jnp.full((CK, T), MASK_NEG, BF16)
        return carry

    lax.fori_loop(nck, S // CK, fill_body, 0)


def _index(kidx, qiT, wT, *, T, CK, CS, topk):
    S = kidx.shape[0]
    return pl.pallas_call(
        functools.partial(_index_kernel, T=T, CK=CK, CS=CS, S=S, topk=topk),
        out_shape=jax.ShapeDtypeStruct((S // T, S, T), BF16),
        grid=(S // T,),
        in_specs=[
            pl.BlockSpec((S, LANES), lambda q: (0, 0)),
            pl.BlockSpec((IDX_HEADS * LANES, T), lambda q: (0, q)),
            pl.BlockSpec((IDX_HEADS, T), lambda q: (0, q)),
        ],
        out_specs=pl.BlockSpec((1, S, T), lambda q: (q, 0, 0)),
        scratch_shapes=[pltpu.VMEM((S, T), F32)],
        compiler_params=_params(1),
        name="index",
    )(kidx, qiT, wT)


def _attn_kernel(qb_ref, kb_ref, kn_ref, qT_ref, vT_ref, bias_ref, o_ref, m_ref, l_ref, acc_ref,
                 *, T, KB, scale):
    i = pl.program_id(0)
    qb = qb_ref[i]
    kb = kb_ref[i]

    @pl.when(kb == 0)
    def _():
        m_ref[...] = jnp.full(m_ref.shape, MASK_NEG, F32)
        l_ref[...] = jnp.zeros(l_ref.shape, F32)
        acc_ref[...] = jnp.zeros(acc_ref.shape, F32)

    bias = bias_ref[0].astype(F32)

    def qk(h):
        hs = slice(h * HEAD_DIM, (h + 1) * HEAD_DIM)
        return jnp.dot(kn_ref[:, hs], qT_ref[hs, :], preferred_element_type=F32)

    c2 = scale * LOG2E
    s_next = qk(0)
    for h in range(N_HEADS):
        hs = slice(h * HEAD_DIM, (h + 1) * HEAD_DIM)
        s = s_next * c2 + bias
        if h + 1 < N_HEADS:
            s_next = qk(h + 1)
        m_old = m_ref[h]
        m_new = jnp.maximum(m_old, s.max(axis=0, keepdims=True))
        alpha = jnp.exp2(m_old - m_new)
        p = jnp.exp2(s - m_new)
        l_ref[h] = alpha * l_ref[h] + p.sum(axis=0, keepdims=True)
        pv = jnp.dot(vT_ref[hs, :], p.astype(BF16), preferred_element_type=F32)
        acc_ref[h] = alpha * acc_ref[h] + pv
        m_ref[h] = m_new

    @pl.when(kb == (qb * T + T - 1) // KB)
    def _():
        for h in range(N_HEADS):
            o_ref[h * HEAD_DIM:(h + 1) * HEAD_DIM, :] = (acc_ref[h] / l_ref[h]).astype(BF16)


def _attn(kn, qT, vT, bias, *, T, KB):
    S = kn.shape[0]
    pairs = [(q, k) for q in range(S // T) for k in range((q * T + T - 1) // KB + 1)]
    qb_ids = jnp.asarray(np.array([p[0] for p in pairs], np.int32))
    kb_ids = jnp.asarray(np.array([p[1] for p in pairs], np.int32))
    grid_spec = pltpu.PrefetchScalarGridSpec(
        num_scalar_prefetch=2,
        grid=(len(pairs),),
        in_specs=[
            pl.BlockSpec((KB, D_ATTN), lambda i, qb, kb: (kb[i], 0)),
            pl.BlockSpec((D_ATTN, T), lambda i, qb, kb: (0, qb[i])),
            pl.BlockSpec((D_ATTN, KB), lambda i, qb, kb: (0, kb[i])),
            pl.BlockSpec((1, KB, T), lambda i, qb, kb: (qb[i], kb[i], 0)),
        ],
        out_specs=pl.BlockSpec((D_ATTN, T), lambda i, qb, kb: (0, qb[i])),
        scratch_shapes=[pltpu.VMEM((N_HEADS, 1, T), F32), pltpu.VMEM((N_HEADS, 1, T), F32),
                        pltpu.VMEM((N_HEADS, HEAD_DIM, T), F32)],
    )
    return pl.pallas_call(
        functools.partial(_attn_kernel, T=T, KB=KB, scale=HEAD_DIM ** -0.5),
        out_shape=jax.ShapeDtypeStruct((D_ATTN, S), BF16),
        grid_spec=grid_spec,
        compiler_params=_params(1),
        name="attn",
    )(qb_ids, kb_ids, kn, qT, vT, bias)


def _outproj_kernel(yp_ref, ya_ref, wp_ref, wa_ref, x_ref, o_ref):
    acc = jnp.dot(yp_ref[...], wp_ref[...], preferred_element_type=F32)
    acc = acc + jnp.dot(ya_ref[...], wa_ref[...], preferred_element_type=F32)
    o_ref[...] = x_ref[...] + acc


def _outproj(yp, ya, w_out, x2, *, tm, tn):
    S = x2.shape[0]
    n_j = D_MODEL // tn
    return pl.pallas_call(
        _outproj_kernel,
        out_shape=jax.ShapeDtypeStruct((S, D_MODEL), F32),
        grid=(S // tm, n_j),
        in_specs=[
            pl.BlockSpec((tm, D_POOL), lambda i, j: (i, 0)),
            pl.BlockSpec((tm, D_ATTN), lambda i, j: (i, 0)),
            pl.BlockSpec((D_POOL, tn), lambda i, j: (0, j)),
            pl.BlockSpec((D_ATTN, tn), lambda i, j: (D_POOL // D_ATTN, j)),
            pl.BlockSpec((tm, tn), lambda i, j: (i, j)),
        ],
        out_specs=pl.BlockSpec((tm, tn), lambda i, j: (i, j)),
        compiler_params=_params(2),
        name="outproj",
    )(yp, ya, w_out, w_out, x2)


def _ffn_kernel(x_ref, halo_ref, g_ref, wg_ref, wv_ref, cwg_ref, cwv_ref, cbg_ref, cbv_ref, wd_ref,
                o_ref, h_ref, acc_ref):
    i = pl.program_id(0)
    f = pl.program_id(1)

    @pl.when(f == 0)
    def _():
        hh = jnp.where(i == 0, 0.0, _rms(halo_ref[...], g_ref[...]))
        h_ref[0:HALO, :] = hh.astype(BF16)
        h_ref[HALO:, :] = _rms(x_ref[...], g_ref[...]).astype(BF16)
        acc_ref[...] = jnp.zeros(acc_ref.shape, F32)

    h = h_ref[...]

    def conv(w_ref, cw_ref, cb_ref):
        up = jnp.dot(h, w_ref[...], preferred_element_type=F32)
        c = (cw_ref[0:1, :] * pltpu.roll(up, 2, axis=0) + cw_ref[1:2, :] * pltpu.roll(up, 1, axis=0)
             + cw_ref[2:3, :] * up)
        return cb_ref[...] + c[HALO:, :]

    cg = conv(wg_ref, cwg_ref, cbg_ref)
    cv = conv(wv_ref, cwv_ref, cbv_ref)
    act = cg * (1.0 / (1.0 + jnp.exp(-cg))) * cv
    acc_ref[...] += jnp.dot(act.astype(BF16), wd_ref[...], preferred_element_type=F32)

    @pl.when(f == pl.num_programs(1) - 1)
    def _():
        o_ref[...] = x_ref[...] + acc_ref[...]


def _ffn(x1, g, w_up, conv_w, conv_b, w_down, *, tm, tf):
    S = x1.shape[0]
    n_f = D_FF // tf
    return pl.pallas_call(
        _ffn_kernel,
        out_shape=jax.ShapeDtypeStruct((S, D_MODEL), F32),
        grid=(S // tm, n_f),
        in_specs=[
            pl.BlockSpec((tm, D_MODEL), lambda i, f: (i, 0)),
            pl.BlockSpec((HALO, D_MODEL), lambda i, f: (jnp.maximum(i * (tm // HALO) - 1, 0), 0)),
            pl.BlockSpec((1, D_MODEL), lambda i, f: (0, 0)),
            pl.BlockSpec((D_MODEL, tf), lambda i, f: (0, f)),
            pl.BlockSpec((D_MODEL, tf), lambda i, f: (0, f + n_f)),
            pl.BlockSpec((CONV_WIDTH, tf), lambda i, f: (0, f)),
            pl.BlockSpec((CONV_WIDTH, tf), lambda i, f: (0, f + n_f)),
            pl.BlockSpec((1, tf), lambda i, f: (0, f)),
            pl.BlockSpec((1, tf), lambda i, f: (0, f + n_f)),
            pl.BlockSpec((tf, D_MODEL), lambda i, f: (f, 0)),
        ],
        out_specs=pl.BlockSpec((tm, D_MODEL), lambda i, f: (i, 0)),
        scratch_shapes=[pltpu.VMEM((HALO + tm, D_MODEL), BF16), pltpu.VMEM((tm, D_MODEL), F32)],
        compiler_params=_params(2),
        name="ffn",
    )(x1, x1, g, w_up, w_up, conv_w, conv_w, conv_b, conv_b, w_down)


def kernel(x, attn_norm_g, w_in, pool_w, pool_scale, q_norm_g, k_norm_g, w_out, ffn_norm_g, w_up,
           conv_w, conv_b, w_down):
    B, S, D = x.shape
    assert B == 1 and D == D_MODEL and w_in.shape == (D_MODEL, D_MAIN + D_TAIL)
    T = 256
    assert S % 1024 == 0
    topk = min(TOPK_MAX, S // 4)
    x2 = x.reshape(S, D)

    w_main = w_in[:, :D_MAIN].astype(BF16)
    w_tail = jnp.pad(w_in[:, D_MAIN:], ((0, 0), (0, LANES - D_TAIL))).astype(BF16)
    zm, zt = _inproj(x2, attn_norm_g.reshape(1, D), w_main, w_tail, q_norm_g.reshape(1, HEAD_DIM),
                     k_norm_g.reshape(1, HEAD_DIM), tm=1024, tn=512)

    y_pool = _pool(zm, pool_w.astype(BF16), pool_scale.reshape(1, D_POOL), tm=1024)

    o = D_POOL
    qT = zm[:, o:o + D_ATTN].T
    kn = zm[:, o + D_ATTN:o + 2 * D_ATTN]
    vT = zm[:, o + 2 * D_ATTN:o + 3 * D_ATTN].T
    qi = zm[:, o + 3 * D_ATTN:].reshape(S, IDX_HEADS, IDX_DIM)
    qiT = jnp.pad(qi, ((0, 0), (0, 0), (0, LANES - IDX_DIM))).reshape(S, IDX_HEADS * LANES).T
    kidx = jnp.pad(zt[:, :IDX_DIM], ((0, 0), (0, LANES - IDX_DIM))).astype(BF16)
    wT = (zt[:, IDX_DIM:D_TAIL] * (IDX_HEADS ** -0.5) * (IDX_DIM ** -0.5)).T

    bias = _index(kidx, qiT, wT, T=T, CK=512, CS=512, topk=topk)
    y_attn = _attn(kn, qT, vT, bias, T=T, KB=512).T

    x1 = _outproj(y_pool, y_attn, w_out.astype(BF16), x2, tm=1024, tn=1024)
    out = _ffn(x1, ffn_norm_g.reshape(1, D), w_up.astype(BF16), conv_w, conv_b.reshape(1, 2 * D_FF),
               w_down.astype(BF16), tm=512, tf=512)
    return out.reshape(B, S, D)
```

```python
import functools

import numpy as np
import jax
import jax.numpy as jnp
from jax import lax
from jax.experimental import pallas as pl
from jax.experimental.pallas import tpu as pltpu

D_MODEL = 2048
D_POOL = 1024
POOL_WINDOWS = (2, 4, 8, 16)
POOL_GROUP = D_POOL // len(POOL_WINDOWS)
D_ATTN = 1024
HEAD_DIM = 128
N_HEADS = D_ATTN // HEAD_DIM
IDX_HEADS = 16
IDX_DIM = 64
TOPK_MAX = 256
D_FF = 5632
CONV_WIDTH = 3
EPS = 1e-6

D_MAIN = D_POOL + 3 * D_ATTN + IDX_HEADS * IDX_DIM
D_TAIL = IDX_DIM + IDX_HEADS
LANES = 128
HALO = 16
MASK_NEG = -1e30
LOG2E = 1.4426950408889634
VMEM_LIMIT_BYTES = 56 * 1024 * 1024

F32 = jnp.float32
BF16 = jnp.bfloat16


def _params(n_axes):
    return pltpu.CompilerParams(dimension_semantics=("arbitrary",) * n_axes,
                                vmem_limit_bytes=VMEM_LIMIT_BYTES)


def _rms(xf, g):
    return xf * lax.rsqrt(jnp.mean(xf * xf, axis=-1, keepdims=True) + EPS) * g


def _inproj_kernel(x_ref, g_ref, wm_ref, wt_ref, qg_ref, kg_ref, zm_ref, zt_ref, h_ref, *, tn):
    j = pl.program_id(1)
    q_lo, k_lo, k_hi = D_POOL // tn, (D_POOL + D_ATTN) // tn, (D_POOL + 2 * D_ATTN) // tn

    @pl.when(j == 0)
    def _():
        h = _rms(x_ref[...], g_ref[...]).astype(BF16)
        h_ref[...] = h
        zt_ref[...] = jnp.dot(h, wt_ref[...], preferred_element_type=F32)

    z = jnp.dot(h_ref[...], wm_ref[...], preferred_element_type=F32)
    is_qk = (j >= q_lo) & (j < k_hi)

    @pl.when(is_qk)
    def _():
        g = jnp.where(j < k_lo, qg_ref[...], kg_ref[...])
        for c in range(tn // HEAD_DIM):
            sl = slice(c * HEAD_DIM, (c + 1) * HEAD_DIM)
            zm_ref[:, sl] = _rms(z[:, sl], g).astype(BF16)

    @pl.when(jnp.logical_not(is_qk))
    def _():
        zm_ref[...] = z.astype(BF16)


def _inproj(x2, g, w_main, w_tail, qg, kg, *, tm, tn):
    S = x2.shape[0]
    return pl.pallas_call(
        functools.partial(_inproj_kernel, tn=tn),
        out_shape=(jax.ShapeDtypeStruct((S, D_MAIN), BF16), jax.ShapeDtypeStruct((S, LANES), F32)),
        grid=(S // tm, D_MAIN // tn),
        in_specs=[
            pl.BlockSpec((tm, D_MODEL), lambda i, j: (i, 0)),
            pl.BlockSpec((1, D_MODEL), lambda i, j: (0, 0)),
            pl.BlockSpec((D_MODEL, tn), lambda i, j: (0, j)),
            pl.BlockSpec((D_MODEL, LANES), lambda i, j: (0, 0)),
            pl.BlockSpec((1, HEAD_DIM), lambda i, j: (0, 0)),
            pl.BlockSpec((1, HEAD_DIM), lambda i, j: (0, 0)),
        ],
        out_specs=(pl.BlockSpec((tm, tn), lambda i, j: (i, j)),
                   pl.BlockSpec((tm, LANES), lambda i, j: (i, 0))),
        scratch_shapes=[pltpu.VMEM((tm, D_MODEL), BF16)],
        compiler_params=_params(2),
        name="inproj",
    )(x2, g, w_main, w_tail, qg, kg)


def _pool_kernel(u_ref, halo_ref, pw_ref, ps_ref, o_ref):
    i = pl.program_id(0)
    tm = u_ref.shape[0]
    u = u_ref[...].astype(F32)
    halo = jnp.where(i == 0, 0.0, halo_ref[...].astype(F32))
    ext = jnp.concatenate([halo, u], axis=0)
    t = i * tm + lax.broadcasted_iota(jnp.int32, (tm, 1), 0)
    for gi, w in enumerate(POOL_WINDOWS):
        sl = slice(gi * POOL_GROUP, (gi + 1) * POOL_GROUP)
        s = ext[:, sl]
        step = 1
        while step < w:
            s = s + pltpu.roll(s, step, axis=0)
            step *= 2
        cnt = jnp.minimum(t + 1, w).astype(F32)
        d = s[HALO:, :] / cnt - u[:, sl]
        y = jnp.dot(d.astype(BF16), pw_ref[gi], preferred_element_type=F32)
        o_ref[:, sl] = (y * ps_ref[:, sl]).astype(BF16)


def _pool(zm, pool_w, pool_scale, *, tm):
    S = zm.shape[0]
    return pl.pallas_call(
        _pool_kernel,
        out_shape=jax.ShapeDtypeStruct((S, D_POOL), BF16),
        grid=(S // tm,),
        in_specs=[
            pl.BlockSpec((tm, D_POOL), lambda i: (i, 0)),
            pl.BlockSpec((HALO, D_POOL), lambda i: (jnp.maximum(i * (tm // HALO) - 1, 0), 0)),
            pl.BlockSpec((len(POOL_WINDOWS), POOL_GROUP, POOL_GROUP), lambda i: (0, 0, 0)),
            pl.BlockSpec((1, D_POOL), lambda i: (0, 0)),
        ],
        out_specs=pl.BlockSpec((tm, D_POOL), lambda i: (i, 0)),
        compiler_params=_params(1),
        name="pool",
    )(zm, zm, pool_w, pool_scale)


I16 = jnp.int16
I16_MIN = -2 ** 15


def _index_kernel(kidx_ref, qiT_ref, wT_ref, bias_ref, hi_ref, lo_ref, *, T, CK, S, topk):
    qb = pl.program_id(0)
    nck = (qb * T + T + CK - 1) // CK
    q_pos = qb * T + lax.broadcasted_iota(jnp.int32, (1, T), 1)

    def score_body(c, carry):
        off = pl.multiple_of(c * CK, CK)
        kc = kidx_ref[pl.ds(off, CK), :]
        acc = None
        for h in range(IDX_HEADS):
            s = jnp.dot(kc, qiT_ref[h * LANES:(h + 1) * LANES, :], preferred_element_type=F32)
            t = wT_ref[h:h + 1, :] * jnp.maximum(s, 0.0)
            acc = t if acc is None else acc + t
        bits = lax.bitcast_convert_type(acc, jnp.int32)
        key = bits ^ ((bits >> 31) & jnp.int32(0x7FFFFFFF))
        key_pos = off + lax.broadcasted_iota(jnp.int32, (CK, 1), 0)
        key = jnp.where(key_pos <= q_pos, key, jnp.int32(-2 ** 31))
        hi_ref[pl.ds(off, CK), :] = (key >> 16).astype(I16)
        lo_ref[pl.ds(off, CK), :] = ((key & jnp.int32(0xFFFF)) + I16_MIN).astype(I16)
        return carry

    lax.fori_loop(0, nck, score_body, 0)

    def count_ge(ref, cand):
        c16 = cand.astype(I16)

        def count_body(c, acc):
            off = pl.multiple_of(c * CK, CK)
            ge = jnp.where(ref[pl.ds(off, CK), :] >= c16, jnp.ones((), I16), jnp.zeros((), I16))
            ge = ge.reshape(CK // 16, 16, T)
            parts = [ge[r] for r in range(CK // 16)]
            while len(parts) > 1:
                parts = [parts[i] + parts[i + 1] for i in range(0, len(parts), 2)]
            return acc + parts[0]

        acc = lax.fori_loop(0, nck, count_body, jnp.zeros((16, T), I16))
        return acc.astype(jnp.int32).sum(axis=0, keepdims=True)

    def kth_largest(ref, k):
        def bit_body(i, v):
            cand = v + (jnp.int32(1) << (15 - i))
            return jnp.where(count_ge(ref, cand) >= k, cand, v)

        return lax.fori_loop(0, 16, bit_body, jnp.full((1, T), I16_MIN, jnp.int32))

    H = kth_largest(hi_ref, topk)
    n_gt = jnp.where(H == -I16_MIN - 1, 0, count_ge(hi_ref, H + 1))
    H16 = H.astype(I16)

    def mask_body(c, carry):
        off = pl.multiple_of(c * CK, CK)
        lo_ref[pl.ds(off, CK), :] = jnp.where(hi_ref[pl.ds(off, CK), :] == H16, lo_ref[pl.ds(off, CK), :],
                                              jnp.full((), I16_MIN, I16))
        return carry

    lax.fori_loop(0, nck, mask_body, 0)
    L16 = kth_largest(lo_ref, topk - n_gt).astype(I16)

    def emit_body(c, carry):
        off = pl.multiple_of(c * CK, CK)
        hi = hi_ref[pl.ds(off, CK), :]
        sel = (hi > H16) | ((hi == H16) & (lo_ref[pl.ds(off, CK), :] >= L16))
        sel = sel & (hi > jnp.full((), I16_MIN, I16))
        bias_ref[0, pl.ds(off, CK), :] = jnp.where(sel, jnp.zeros((), BF16), jnp.full((), MASK_NEG, BF16))
        return carry

    lax.fori_loop(0, nck, emit_body, 0)

    def fill_body(c, carry):
        off = pl.multiple_of(c * CK, CK)
        bias_ref[0, pl.ds(off, CK), :] = jnp.full((CK, T), MASK_NEG, BF16)
        return carry

    lax.fori_loop(nck, S // CK, fill_body, 0)


def _index(kidx, qiT, wT, *, T, CK, topk):
    S = kidx.shape[0]
    return pl.pallas_call(
        functools.partial(_index_kernel, T=T, CK=CK, S=S, topk=topk),
        out_shape=jax.ShapeDtypeStruct((S // T, S, T), BF16),
        grid=(S // T,),
        in_specs=[
            pl.BlockSpec((S, LANES), lambda q: (0, 0)),
            pl.BlockSpec((IDX_HEADS * LANES, T), lambda q: (0, q)),
            pl.BlockSpec((IDX_HEADS, T), lambda q: (0, q)),
        ],
        out_specs=pl.BlockSpec((1, S, T), lambda q: (q, 0, 0)),
        scratch_shapes=[pltpu.VMEM((S, T), I16), pltpu.VMEM((S, T), I16)],
        compiler_params=_params(1),
        name="index",
    )(kidx, qiT, wT)


def _attn_kernel(qb_ref, kb_ref, kn_ref, qT_ref, vT_ref, bias_ref, o_ref, m_ref, l_ref, acc_ref,
                 *, T, KB, scale):
    i = pl.program_id(0)
    qb = qb_ref[i]
    kb = kb_ref[i]

    @pl.when(kb == 0)
    def _():
        m_ref[...] = jnp.full(m_ref.shape, MASK_NEG, F32)
        l_ref[...] = jnp.zeros(l_ref.shape, F32)
        acc_ref[...] = jnp.zeros(acc_ref.shape, F32)

    bias = bias_ref[0].astype(F32)

    def qk(h):
        hs = slice(h * HEAD_DIM, (h + 1) * HEAD_DIM)
        return jnp.dot(kn_ref[:, hs], qT_ref[hs, :], preferred_element_type=F32)

    c2 = scale * LOG2E
    ahead = 3
    pending = [qk(h) for h in range(ahead)]
    for h in range(N_HEADS):
        hs = slice(h * HEAD_DIM, (h + 1) * HEAD_DIM)
        s = pending.pop(0) * c2 + bias
        if h + ahead < N_HEADS:
            pending.append(qk(h + ahead))
        m_old = m_ref[h]
        m_new = jnp.maximum(m_old, s.max(axis=0, keepdims=True))
        alpha = jnp.exp2(m_old - m_new)
        p = jnp.exp2(s - m_new)
        l_ref[h] = alpha * l_ref[h] + p.sum(axis=0, keepdims=True)
        pv = jnp.dot(vT_ref[hs, :], p.astype(BF16), preferred_element_type=F32)
        acc_ref[h] = alpha * acc_ref[h] + pv
        m_ref[h] = m_new

    @pl.when(kb == (qb * T + T - 1) // KB)
    def _():
        for h in range(N_HEADS):
            o_ref[h * HEAD_DIM:(h + 1) * HEAD_DIM, :] = (acc_ref[h] / l_ref[h]).astype(BF16)


def _attn(kn, qT, vT, bias, *, T, KB):
    S = kn.shape[0]
    pairs = [(q, k) for q in range(S // T) for k in range((q * T + T - 1) // KB + 1)]
    qb_ids = jnp.asarray(np.array([p[0] for p in pairs], np.int32))
    kb_ids = jnp.asarray(np.array([p[1] for p in pairs], np.int32))
    grid_spec = pltpu.PrefetchScalarGridSpec(
        num_scalar_prefetch=2,
        grid=(len(pairs),),
        in_specs=[
            pl.BlockSpec((KB, D_ATTN), lambda i, qb, kb: (kb[i], 0)),
            pl.BlockSpec((D_ATTN, T), lambda i, qb, kb: (0, qb[i])),
            pl.BlockSpec((D_ATTN, KB), lambda i, qb, kb: (0, kb[i])),
            pl.BlockSpec((1, KB, T), lambda i, qb, kb: (qb[i], kb[i], 0)),
        ],
        out_specs=pl.BlockSpec((D_ATTN, T), lambda i, qb, kb: (0, qb[i])),
        scratch_shapes=[pltpu.VMEM((N_HEADS, 1, T), F32), pltpu.VMEM((N_HEADS, 1, T), F32),
                        pltpu.VMEM((N_HEADS, HEAD_DIM, T), F32)],
    )
    return pl.pallas_call(
        functools.partial(_attn_kernel, T=T, KB=KB, scale=HEAD_DIM ** -0.5),
        out_shape=jax.ShapeDtypeStruct((D_ATTN, S), BF16),
        grid_spec=grid_spec,
        compiler_params=_params(1),
        name="attn",
    )(qb_ids, kb_ids, kn, qT, vT, bias)


def _outproj_kernel(yp_ref, ya_ref, wp_ref, wa_ref, x_ref, o_ref):
    acc = jnp.dot(yp_ref[...], wp_ref[...], preferred_element_type=F32)
    acc = acc + jnp.dot(ya_ref[...], wa_ref[...], preferred_element_type=F32)
    o_ref[...] = x_ref[...] + acc


def _outproj(yp, ya, w_out, x2, *, tm, tn):
    S = x2.shape[0]
    n_j = D_MODEL // tn
    return pl.pallas_call(
        _outproj_kernel,
        out_shape=jax.ShapeDtypeStruct((S, D_MODEL), F32),
        grid=(S // tm, n_j),
        in_specs=[
            pl.BlockSpec((tm, D_POOL), lambda i, j: (i, 0)),
            pl.BlockSpec((tm, D_ATTN), lambda i, j: (i, 0)),
            pl.BlockSpec((D_POOL, tn), lambda i, j: (0, j)),
            pl.BlockSpec((D_ATTN, tn), lambda i, j: (D_POOL // D_ATTN, j)),
            pl.BlockSpec((tm, tn), lambda i, j: (i, j)),
        ],
        out_specs=pl.BlockSpec((tm, tn), lambda i, j: (i, j)),
        compiler_params=_params(2),
        name="outproj",
    )(yp, ya, w_out, w_out, x2)


def _ffn_kernel(x_ref, halo_ref, g_ref, wg_ref, wv_ref, cwg_ref, cwv_ref, cbg_ref, cbv_ref, wd_ref,
                o_ref, h_ref, acc_ref):
    i = pl.program_id(0)
    f = pl.program_id(1)

    @pl.when(f == 0)
    def _():
        hh = jnp.where(i == 0, 0.0, _rms(halo_ref[...], g_ref[...]))
        h_ref[0:HALO, :] = hh.astype(BF16)
        h_ref[HALO:, :] = _rms(x_ref[...], g_ref[...]).astype(BF16)
        acc_ref[...] = jnp.zeros(acc_ref.shape, F32)

    h = h_ref[...]

    def conv(w_ref, cw_ref, cb_ref):
        up = jnp.dot(h, w_ref[...], preferred_element_type=F32)
        c = (cw_ref[0:1, :] * pltpu.roll(up, 2, axis=0) + cw_ref[1:2, :] * pltpu.roll(up, 1, axis=0)
             + cw_ref[2:3, :] * up)
        return cb_ref[...] + c[HALO:, :]

    cg = conv(wg_ref, cwg_ref, cbg_ref)
    cv = conv(wv_ref, cwv_ref, cbv_ref)
    act = cg * (1.0 / (1.0 + jnp.exp(-cg))) * cv
    acc_ref[...] += jnp.dot(act.astype(BF16), wd_ref[...], preferred_element_type=F32)

    @pl.when(f == pl.num_programs(1) - 1)
    def _():
        o_ref[...] = x_ref[...] + acc_ref[...]


def _ffn(x1, g, w_up, conv_w, conv_b, w_down, *, tm, tf):
    S = x1.shape[0]
    n_f = D_FF // tf
    return pl.pallas_call(
        _ffn_kernel,
        out_shape=jax.ShapeDtypeStruct((S, D_MODEL), F32),
        grid=(S // tm, n_f),
        in_specs=[
            pl.BlockSpec((tm, D_MODEL), lambda i, f: (i, 0)),
            pl.BlockSpec((HALO, D_MODEL), lambda i, f: (jnp.maximum(i * (tm // HALO) - 1, 0), 0)),
            pl.BlockSpec((1, D_MODEL), lambda i, f: (0, 0)),
            pl.BlockSpec((D_MODEL, tf), lambda i, f: (0, f)),
            pl.BlockSpec((D_MODEL, tf), lambda i, f: (0, f + n_f)),
            pl.BlockSpec((CONV_WIDTH, tf), lambda i, f: (0, f)),
            pl.BlockSpec((CONV_WIDTH, tf), lambda i, f: (0, f + n_f)),
            pl.BlockSpec((1, tf), lambda i, f: (0, f)),
            pl.BlockSpec((1, tf), lambda i, f: (0, f + n_f)),
            pl.BlockSpec((tf, D_MODEL), lambda i, f: (f, 0)),
        ],
        out_specs=pl.BlockSpec((tm, D_MODEL), lambda i, f: (i, 0)),
        scratch_shapes=[pltpu.VMEM((HALO + tm, D_MODEL), BF16), pltpu.VMEM((tm, D_MODEL), F32)],
        compiler_params=_params(2),
        name="ffn",
    )(x1, x1, g, w_up, w_up, conv_w, conv_w, conv_b, conv_b, w_down)


def kernel(x, attn_norm_g, w_in, pool_w, pool_scale, q_norm_g, k_norm_g, w_out, ffn_norm_g, w_up,
           conv_w, conv_b, w_down):
    B, S, D = x.shape
    assert B == 1 and D == D_MODEL and w_in.shape == (D_MODEL, D_MAIN + D_TAIL)
    T = 256
    assert S % 1024 == 0
    topk = min(TOPK_MAX, S // 4)
    x2 = x.reshape(S, D)

    w_main = w_in[:, :D_MAIN].astype(BF16)
    w_tail = jnp.pad(w_in[:, D_MAIN:], ((0, 0), (0, LANES - D_TAIL))).astype(BF16)
    zm, zt = _inproj(x2, attn_norm_g.reshape(1, D), w_main, w_tail, q_norm_g.reshape(1, HEAD_DIM),
                     k_norm_g.reshape(1, HEAD_DIM), tm=1024, tn=512)

    y_pool = _pool(zm, pool_w.astype(BF16), pool_scale.reshape(1, D_POOL), tm=1024)

    o = D_POOL
    qT = zm[:, o:o + D_ATTN].T
    kn = zm[:, o + D_ATTN:o + 2 * D_ATTN]
    vT = zm[:, o + 2 * D_ATTN:o + 3 * D_ATTN].T
    qi = zm[:, o + 3 * D_ATTN:].reshape(S, IDX_HEADS, IDX_DIM)
    qiT = jnp.pad(qi, ((0, 0), (0, 0), (0, LANES - IDX_DIM))).reshape(S, IDX_HEADS * LANES).T
    kidx = jnp.pad(zt[:, :IDX_DIM], ((0, 0), (0, LANES - IDX_DIM))).astype(BF16)
    wT = (zt[:, IDX_DIM:D_TAIL] * (IDX_HEADS ** -0.5) * (IDX_DIM ** -0.5)).T

    bias = _index(kidx, qiT, wT, T=T, CK=512, topk=topk)
    y_attn = _attn(kn, qT, vT, bias, T=T, KB=512).T

    x1 = _outproj(y_pool, y_attn, w_out.astype(BF16), x2, tm=1024, tn=1024)
    out = _ffn(x1, ffn_norm_g.reshape(1, D), w_up.astype(BF16), conv_w, conv_b.reshape(1, 2 * D_FF),
               w_down.astype(BF16), tm=512, tf=512)
    return out.reshape(B, S, D)
```

```python
import functools

import numpy as np
import jax
import jax.numpy as jnp
from jax import lax
from jax.experimental import pallas as pl
from jax.experimental.pallas import tpu as pltpu

D_MODEL = 2048
D_POOL = 1024
POOL_WINDOWS = (2, 4, 8, 16)
POOL_GROUP = D_POOL // len(POOL_WINDOWS)
D_ATTN = 1024
HEAD_DIM = 128
N_HEADS = D_ATTN // HEAD_DIM
IDX_HEADS = 16
IDX_DIM = 64
TOPK_MAX = 256
D_FF = 5632
CONV_WIDTH = 3
EPS = 1e-6

D_MAIN = D_POOL + 3 * D_ATTN + IDX_HEADS * IDX_DIM
D_TAIL = IDX_DIM + IDX_HEADS
LANES = 128
HALO = 16
MASK_NEG = -1e30
LOG2E = 1.4426950408889634
VMEM_LIMIT_BYTES = 56 * 1024 * 1024

F32 = jnp.float32
BF16 = jnp.bfloat16


def _params(n_axes):
    return pltpu.CompilerParams(dimension_semantics=("arbitrary",) * n_axes,
                                vmem_limit_bytes=VMEM_LIMIT_BYTES)


def _rms(xf, g):
    return xf * lax.rsqrt(jnp.mean(xf * xf, axis=-1, keepdims=True) + EPS) * g


_NT_DIMS = (((1,), (1,)), ((), ()))


def _inproj_kernel(x_ref, g_ref, wn_ref, wt_ref, wk_ref, ww_ref, kg_ref, qg_ref, nat_ref, tr_ref, kidx_ref,
                   wT_ref, h_ref, *, tn, q_scale):
    j = pl.program_id(1)
    n_nat = (D_POOL + D_ATTN) // tn

    @pl.when(j == 0)
    def _():
        h = _rms(x_ref[...], g_ref[...]).astype(BF16)
        h_ref[...] = h
        kidx_ref[...] = jnp.dot(h, wk_ref[...], preferred_element_type=F32).astype(BF16)
        w = lax.dot_general(ww_ref[...], h, _NT_DIMS, preferred_element_type=F32)
        wT_ref[...] = w * (IDX_HEADS ** -0.5) * (IDX_DIM ** -0.5)

    @pl.when(j < n_nat)
    def _():
        z = jnp.dot(h_ref[...], wn_ref[...], preferred_element_type=F32)

        @pl.when(j < D_POOL // tn)
        def _():
            nat_ref[...] = z.astype(BF16)

        @pl.when(j >= D_POOL // tn)
        def _():
            for c in range(tn // HEAD_DIM):
                sl = slice(c * HEAD_DIM, (c + 1) * HEAD_DIM)
                nat_ref[:, sl] = _rms(z[:, sl], kg_ref[...]).astype(BF16)

    @pl.when(j >= n_nat)
    def _():
        zT = lax.dot_general(wt_ref[...], h_ref[...], _NT_DIMS, preferred_element_type=F32)

        @pl.when(j < n_nat + D_ATTN // tn)
        def _():
            for c in range(tn // HEAD_DIM):
                sl = slice(c * HEAD_DIM, (c + 1) * HEAD_DIM)
                zc = zT[sl, :]
                inv = lax.rsqrt(jnp.mean(zc * zc, axis=0, keepdims=True) + EPS)
                tr_ref[sl, :] = (zc * inv * (qg_ref[...] * q_scale)).astype(BF16)

        @pl.when(j >= n_nat + D_ATTN // tn)
        def _():
            tr_ref[...] = zT.astype(BF16)


def _inproj(x2, g, w_nat, w_trT, w_kidx, w_widxT, kg, qg_col, *, tm, tn, q_scale):
    S = x2.shape[0]
    n_nat, n_tr = w_nat.shape[1] // tn, w_trT.shape[0] // tn
    return pl.pallas_call(
        functools.partial(_inproj_kernel, tn=tn, q_scale=q_scale),
        out_shape=(jax.ShapeDtypeStruct((S, w_nat.shape[1]), BF16),
                   jax.ShapeDtypeStruct((w_trT.shape[0], S), BF16),
                   jax.ShapeDtypeStruct((S, IDX_DIM), BF16),
                   jax.ShapeDtypeStruct((IDX_HEADS, S), F32)),
        grid=(S // tm, n_nat + n_tr),
        in_specs=[
            pl.BlockSpec((tm, D_MODEL), lambda i, j: (i, 0)),
            pl.BlockSpec((1, D_MODEL), lambda i, j: (0, 0)),
            pl.BlockSpec((D_MODEL, tn), lambda i, j: (0, jnp.minimum(j, n_nat - 1))),
            pl.BlockSpec((tn, D_MODEL), lambda i, j: (jnp.maximum(j - n_nat, 0), 0)),
            pl.BlockSpec((D_MODEL, IDX_DIM), lambda i, j: (0, 0)),
            pl.BlockSpec((IDX_HEADS, D_MODEL), lambda i, j: (0, 0)),
            pl.BlockSpec((1, HEAD_DIM), lambda i, j: (0, 0)),
            pl.BlockSpec((HEAD_DIM, 1), lambda i, j: (0, 0)),
        ],
        out_specs=(pl.BlockSpec((tm, tn), lambda i, j: (i, jnp.minimum(j, n_nat - 1))),
                   pl.BlockSpec((tn, tm), lambda i, j: (jnp.maximum(j - n_nat, 0), i)),
                   pl.BlockSpec((tm, IDX_DIM), lambda i, j: (i, 0)),
                   pl.BlockSpec((IDX_HEADS, tm), lambda i, j: (0, i))),
        scratch_shapes=[pltpu.VMEM((tm, D_MODEL), BF16)],
        compiler_params=_params(2),
        name="inproj",
    )(x2, g, w_nat, w_trT, w_kidx, w_widxT, kg, qg_col)


def _pool_kernel(u_ref, halo_ref, pw_ref, ps_ref, o_ref):
    i = pl.program_id(0)
    tm = u_ref.shape[0]
    u = u_ref[...].astype(F32)
    halo = jnp.where(i == 0, 0.0, halo_ref[...].astype(F32))
    ext = jnp.concatenate([halo, u], axis=0)
    t = i * tm + lax.broadcasted_iota(jnp.int32, (tm, 1), 0)
    for gi, w in enumerate(POOL_WINDOWS):
        sl = slice(gi * POOL_GROUP, (gi + 1) * POOL_GROUP)
        s = ext[:, sl]
        step = 1
        while step < w:
            s = s + pltpu.roll(s, step, axis=0)
            step *= 2
        cnt = jnp.minimum(t + 1, w).astype(F32)
        d = s[HALO:, :] / cnt - u[:, sl]
        y = jnp.dot(d.astype(BF16), pw_ref[gi], preferred_element_type=F32)
        o_ref[:, sl] = (y * ps_ref[:, sl]).astype(BF16)


def _pool(zm, pool_w, pool_scale, *, tm):
    S = zm.shape[0]
    return pl.pallas_call(
        _pool_kernel,
        out_shape=jax.ShapeDtypeStruct((S, D_POOL), BF16),
        grid=(S // tm,),
        in_specs=[
            pl.BlockSpec((tm, D_POOL), lambda i: (i, 0)),
            pl.BlockSpec((HALO, D_POOL), lambda i: (jnp.maximum(i * (tm // HALO) - 1, 0), 0)),
            pl.BlockSpec((len(POOL_WINDOWS), POOL_GROUP, POOL_GROUP), lambda i: (0, 0, 0)),
            pl.BlockSpec((1, D_POOL), lambda i: (0, 0)),
        ],
        out_specs=pl.BlockSpec((tm, D_POOL), lambda i: (i, 0)),
        compiler_params=_params(1),
        name="pool",
    )(zm, zm, pool_w, pool_scale)


I16 = jnp.int16
I16_MIN = -2 ** 15


def _index_kernel(kidx_ref, qiT_ref, wT_ref, bias_ref, hi_ref, lo_ref, *, T, CK, S, topk):
    qb = pl.program_id(0)
    nck = (qb * T + T + CK - 1) // CK
    q_pos = qb * T + lax.broadcasted_iota(jnp.int32, (1, T), 1)

    def score_body(c, carry):
        off = pl.multiple_of(c * CK, CK)
        kc = kidx_ref[pl.ds(off, CK), :]
        acc = None
        for h in range(IDX_HEADS):
            s = jnp.dot(kc, qiT_ref[h * IDX_DIM:(h + 1) * IDX_DIM, :], preferred_element_type=F32)
            t = wT_ref[h:h + 1, :] * jnp.maximum(s, 0.0)
            acc = t if acc is None else acc + t
        bits = lax.bitcast_convert_type(acc, jnp.int32)
        key = bits ^ ((bits >> 31) & jnp.int32(0x7FFFFFFF))
        key_pos = off + lax.broadcasted_iota(jnp.int32, (CK, 1), 0)
        key = jnp.where(key_pos <= q_pos, key, jnp.int32(-2 ** 31))
        hi_ref[pl.ds(off, CK), :] = (key >> 16).astype(I16)
        lo_ref[pl.ds(off, CK), :] = ((key & jnp.int32(0xFFFF)) + I16_MIN).astype(I16)
        return carry

    lax.fori_loop(0, nck, score_body, 0)

    def count_ge(ref, cand):
        c16 = cand.astype(I16)

        def count_body(c, acc):
            off = pl.multiple_of(c * CK, CK)
            ge = jnp.where(ref[pl.ds(off, CK), :] >= c16, jnp.ones((), I16), jnp.zeros((), I16))
            ge = ge.reshape(CK // 16, 16, T)
            parts = [ge[r] for r in range(CK // 16)]
            while len(parts) > 1:
                parts = [parts[i] + parts[i + 1] for i in range(0, len(parts), 2)]
            return acc + parts[0]

        acc = lax.fori_loop(0, nck, count_body, jnp.zeros((16, T), I16))
        return acc.astype(jnp.int32).sum(axis=0, keepdims=True)

    def kth_largest(ref, k):
        def bit_body(i, v):
            cand = v + (jnp.int32(1) << (15 - i))
            return jnp.where(count_ge(ref, cand) >= k, cand, v)

        return lax.fori_loop(0, 16, bit_body, jnp.full((1, T), I16_MIN, jnp.int32))

    H = kth_largest(hi_ref, topk)
    n_gt = jnp.where(H == -I16_MIN - 1, 0, count_ge(hi_ref, H + 1))
    H16 = H.astype(I16)

    def mask_body(c, carry):
        off = pl.multiple_of(c * CK, CK)
        lo_ref[pl.ds(off, CK), :] = jnp.where(hi_ref[pl.ds(off, CK), :] == H16, lo_ref[pl.ds(off, CK), :],
                                              jnp.full((), I16_MIN, I16))
        return carry

    lax.fori_loop(0, nck, mask_body, 0)
    L16 = kth_largest(lo_ref, topk - n_gt).astype(I16)

    def emit_body(c, carry):
        off = pl.multiple_of(c * CK, CK)
        hi = hi_ref[pl.ds(off, CK), :]
        sel = (hi > H16) | ((hi == H16) & (lo_ref[pl.ds(off, CK), :] >= L16))
        sel = sel & (hi > jnp.full((), I16_MIN, I16))
        bias_ref[0, pl.ds(off, CK), :] = jnp.where(sel, jnp.zeros((), BF16), jnp.full((), MASK_NEG, BF16))
        return carry

    lax.fori_loop(0, nck, emit_body, 0)

    def fill_body(c, carry):
        off = pl.multiple_of(c * CK, CK)
        bias_ref[0, pl.ds(off, CK), :] = jnp.full((CK, T), MASK_NEG, BF16)
        return carry

    lax.fori_loop(nck, S // CK, fill_body, 0)


def _index(kidx, tr, wT, *, T, CK, topk):
    S = kidx.shape[0]
    return pl.pallas_call(
        functools.partial(_index_kernel, T=T, CK=CK, S=S, topk=topk),
        out_shape=jax.ShapeDtypeStruct((S // T, S, T), BF16),
        grid=(S // T,),
        in_specs=[
            pl.BlockSpec((S, IDX_DIM), lambda q: (0, 0)),
            pl.BlockSpec((IDX_HEADS * IDX_DIM, T), lambda q: (2, q)),
            pl.BlockSpec((IDX_HEADS, T), lambda q: (0, q)),
        ],
        out_specs=pl.BlockSpec((1, S, T), lambda q: (q, 0, 0)),
        scratch_shapes=[pltpu.VMEM((S, T), I16), pltpu.VMEM((S, T), I16)],
        compiler_params=_params(1),
        name="index",
    )(kidx, tr, wT)


def _attn_kernel(qb_ref, kb_ref, kn_ref, qT_ref, vT_ref, bias_ref, o_ref, m_ref, l_ref, acc_ref, *, T, KB,
                 SUB):
    i = pl.program_id(0)
    qb = qb_ref[i]
    kb = kb_ref[i]

    @pl.when(kb == 0)
    def _():
        m_ref[...] = jnp.full(m_ref.shape, MASK_NEG, F32)
        l_ref[...] = jnp.zeros(l_ref.shape, F32)
        acc_ref[...] = jnp.zeros(acc_ref.shape, F32)

    items = [(kb0, h) for kb0 in range(0, KB, SUB) for h in range(N_HEADS)]

    def qk(item):
        kb0, h = item
        hs = slice(h * HEAD_DIM, (h + 1) * HEAD_DIM)
        return jnp.dot(kn_ref[kb0:kb0 + SUB, hs], qT_ref[hs, :], preferred_element_type=F32)

    ahead = 4
    pending = [qk(it) for it in items[:ahead]]
    bias = {kb0: bias_ref[0, kb0:kb0 + SUB, :].astype(F32) for kb0 in range(0, KB, SUB)}
    for n, (kb0, h) in enumerate(items):
        hs = slice(h * HEAD_DIM, (h + 1) * HEAD_DIM)
        s = bias[kb0] + pending.pop(0)
        if n + ahead < len(items):
            pending.append(qk(items[n + ahead]))
        m_old = m_ref[h]
        m_new = jnp.maximum(m_old, s.max(axis=0, keepdims=True))
        alpha = jnp.exp2(m_old - m_new)
        p = jnp.exp2(s - m_new)
        l_ref[h] = alpha * l_ref[h] + p.sum(axis=0, keepdims=True)
        pv = jnp.dot(vT_ref[hs, kb0:kb0 + SUB], p.astype(BF16), preferred_element_type=F32)
        acc_ref[h] = alpha * acc_ref[h] + pv
        m_ref[h] = m_new

    @pl.when(kb == (qb * T + T - 1) // KB)
    def _():
        for h in range(N_HEADS):
            o_ref[:, h * HEAD_DIM:(h + 1) * HEAD_DIM] = (acc_ref[h] / l_ref[h]).T.astype(BF16)


def _attn(nat, tr, bias, *, T, KB, SUB):
    S = nat.shape[0]
    pairs = [(q, k) for q in range(S // T) for k in range((q * T + T - 1) // KB + 1)]
    qb_ids = jnp.asarray(np.array([p[0] for p in pairs], np.int32))
    kb_ids = jnp.asarray(np.array([p[1] for p in pairs], np.int32))
    grid_spec = pltpu.PrefetchScalarGridSpec(
        num_scalar_prefetch=2,
        grid=(len(pairs),),
        in_specs=[
            pl.BlockSpec((KB, D_ATTN), lambda i, qb, kb: (kb[i], 1)),
            pl.BlockSpec((D_ATTN, T), lambda i, qb, kb: (0, qb[i])),
            pl.BlockSpec((D_ATTN, KB), lambda i, qb, kb: (1, kb[i])),
            pl.BlockSpec((1, KB, T), lambda i, qb, kb: (qb[i], kb[i], 0)),
        ],
        out_specs=pl.BlockSpec((T, D_ATTN), lambda i, qb, kb: (qb[i], 0)),
        scratch_shapes=[pltpu.VMEM((N_HEADS, 1, T), F32), pltpu.VMEM((N_HEADS, 1, T), F32),
                        pltpu.VMEM((N_HEADS, HEAD_DIM, T), F32)],
    )
    return pl.pallas_call(
        functools.partial(_attn_kernel, T=T, KB=KB, SUB=SUB),
        out_shape=jax.ShapeDtypeStruct((S, D_ATTN), BF16),
        grid_spec=grid_spec,
        compiler_params=_params(1),
        name="attn",
    )(qb_ids, kb_ids, nat, tr, tr, bias)


def _outproj_kernel(yp_ref, ya_ref, wp_ref, wa_ref, x_ref, o_ref):
    acc = jnp.dot(yp_ref[...], wp_ref[...], preferred_element_type=F32)
    acc = acc + jnp.dot(ya_ref[...], wa_ref[...], preferred_element_type=F32)
    o_ref[...] = x_ref[...] + acc


def _outproj(yp, ya, w_out, x2, *, tm, tn):
    S = x2.shape[0]
    n_j = D_MODEL // tn
    return pl.pallas_call(
        _outproj_kernel,
        out_shape=jax.ShapeDtypeStruct((S, D_MODEL), F32),
        grid=(S // tm, n_j),
        in_specs=[
            pl.BlockSpec((tm, D_POOL), lambda i, j: (i, 0)),
            pl.BlockSpec((tm, D_ATTN), lambda i, j: (i, 0)),
            pl.BlockSpec((D_POOL, tn), lambda i, j: (0, j)),
            pl.BlockSpec((D_ATTN, tn), lambda i, j: (D_POOL // D_ATTN, j)),
            pl.BlockSpec((tm, tn), lambda i, j: (i, j)),
        ],
        out_specs=pl.BlockSpec((tm, tn), lambda i, j: (i, j)),
        compiler_params=_params(2),
        name="outproj",
    )(yp, ya, w_out, w_out, x2)


def _ffn_kernel(x_ref, halo_ref, g_ref, wg_ref, wv_ref, cwg_ref, cwv_ref, cbg_ref, cbv_ref, wd_ref,
                o_ref, h_ref, acc_ref):
    i = pl.program_id(0)
    f = pl.program_id(1)

    @pl.when(f == 0)
    def _():
        hh = jnp.where(i == 0, 0.0, _rms(halo_ref[...], g_ref[...]))
        h_ref[0:HALO, :] = hh.astype(BF16)
        h_ref[HALO:, :] = _rms(x_ref[...], g_ref[...]).astype(BF16)
        acc_ref[...] = jnp.zeros(acc_ref.shape, F32)

    h = h_ref[...]

    def conv(w_ref, cw_ref, cb_ref):
        up = jnp.dot(h, w_ref[...], preferred_element_type=F32)
        c = (cw_ref[0:1, :] * pltpu.roll(up, 2, axis=0) + cw_ref[1:2, :] * pltpu.roll(up, 1, axis=0)
             + cw_ref[2:3, :] * up)
        return cb_ref[...] + c[HALO:, :]

    cg = conv(wg_ref, cwg_ref, cbg_ref)
    cv = conv(wv_ref, cwv_ref, cbv_ref)
    act = cg * (1.0 / (1.0 + jnp.exp(-cg))) * cv
    acc_ref[...] += jnp.dot(act.astype(BF16), wd_ref[...], preferred_element_type=F32)

    @pl.when(f == pl.num_programs(1) - 1)
    def _():
        o_ref[...] = x_ref[...] + acc_ref[...]


def _ffn(x1, g, w_up, conv_w, conv_b, w_down, *, tm, tf):
    S = x1.shape[0]
    n_f = D_FF // tf
    return pl.pallas_call(
        _ffn_kernel,
        out_shape=jax.ShapeDtypeStruct((S, D_MODEL), F32),
        grid=(S // tm, n_f),
        in_specs=[
            pl.BlockSpec((tm, D_MODEL), lambda i, f: (i, 0)),
            pl.BlockSpec((HALO, D_MODEL), lambda i, f: (jnp.maximum(i * (tm // HALO) - 1, 0), 0)),
            pl.BlockSpec((1, D_MODEL), lambda i, f: (0, 0)),
            pl.BlockSpec((D_MODEL, tf), lambda i, f: (0, f)),
            pl.BlockSpec((D_MODEL, tf), lambda i, f: (0, f + n_f)),
            pl.BlockSpec((CONV_WIDTH, tf), lambda i, f: (0, f)),
            pl.BlockSpec((CONV_WIDTH, tf), lambda i, f: (0, f + n_f)),
            pl.BlockSpec((1, tf), lambda i, f: (0, f)),
            pl.BlockSpec((1, tf), lambda i, f: (0, f + n_f)),
            pl.BlockSpec((tf, D_MODEL), lambda i, f: (f, 0)),
        ],
        out_specs=pl.BlockSpec((tm, D_MODEL), lambda i, f: (i, 0)),
        scratch_shapes=[pltpu.VMEM((HALO + tm, D_MODEL), BF16), pltpu.VMEM((tm, D_MODEL), F32)],
        compiler_params=_params(2),
        name="ffn",
    )(x1, x1, g, w_up, w_up, conv_w, conv_w, conv_b, conv_b, w_down)


def kernel(x, attn_norm_g, w_in, pool_w, pool_scale, q_norm_g, k_norm_g, w_out, ffn_norm_g, w_up,
           conv_w, conv_b, w_down):
    B, S, D = x.shape
    assert B == 1 and D == D_MODEL and w_in.shape == (D_MODEL, D_MAIN + D_TAIL)
    T = 256
    assert S % 1024 == 0
    topk = min(TOPK_MAX, S // 4)
    x2 = x.reshape(S, D)

    o = D_POOL
    w_q, w_k, w_v = (w_in[:, o + n * D_ATTN:o + (n + 1) * D_ATTN] for n in range(3))
    w_qi = w_in[:, o + 3 * D_ATTN:D_MAIN]
    w_nat = jnp.concatenate([w_in[:, :o], w_k], axis=1).astype(BF16)
    w_trT = jnp.concatenate([w_q, w_v, w_qi], axis=1).T.astype(BF16)
    w_kidx = w_in[:, D_MAIN:D_MAIN + IDX_DIM].astype(BF16)
    w_widxT = w_in[:, D_MAIN + IDX_DIM:].T.astype(BF16)
    nat, tr, kidx, wT = _inproj(x2, attn_norm_g.reshape(1, D), w_nat, w_trT, w_kidx, w_widxT,
                                k_norm_g.reshape(1, HEAD_DIM), q_norm_g.reshape(HEAD_DIM, 1),
                                tm=1024, tn=512, q_scale=HEAD_DIM ** -0.5 * LOG2E)

    y_pool = _pool(nat, pool_w.astype(BF16), pool_scale.reshape(1, D_POOL), tm=1024)
    bias = _index(kidx, tr, wT, T=T, CK=512, topk=topk)
    y_attn = _attn(nat, tr, bias, T=T, KB=512, SUB=512)

    x1 = _outproj(y_pool, y_attn, w_out.astype(BF16), x2, tm=1024, tn=1024)
    out = _ffn(x1, ffn_norm_g.reshape(1, D), w_up.astype(BF16), conv_w, conv_b.reshape(1, 2 * D_FF),
               w_down.astype(BF16), tm=512, tf=512)
    return out.reshape(B, S, D)
```

```python
import functools

import numpy as np
import jax
import jax.numpy as jnp
from jax import lax
from jax.experimental import pallas as pl
from jax.experimental.pallas import tpu as pltpu

D_MODEL = 2048
D_POOL = 1024
POOL_WINDOWS = (2, 4, 8, 16)
POOL_GROUP = D_POOL // len(POOL_WINDOWS)
D_ATTN = 1024
HEAD_DIM = 128
N_HEADS = D_ATTN // HEAD_DIM
IDX_HEADS = 16
IDX_DIM = 64
TOPK_MAX = 256
D_FF = 5632
CONV_WIDTH = 3
EPS = 1e-6

D_MAIN = D_POOL + 3 * D_ATTN + IDX_HEADS * IDX_DIM
D_TAIL = IDX_DIM + IDX_HEADS
LANES = 128
HALO = 16
MASK_NEG = -1e30
LOG2E = 1.4426950408889634
MAX_UNSHIFTED_LOG2 = 60.0
VMEM_LIMIT_BYTES = 56 * 1024 * 1024

F32 = jnp.float32
BF16 = jnp.bfloat16


def _params(n_axes):
    return pltpu.CompilerParams(dimension_semantics=("arbitrary",) * n_axes,
                                vmem_limit_bytes=VMEM_LIMIT_BYTES)


def _rms(xf, g):
    return xf * lax.rsqrt(jnp.mean(xf * xf, axis=-1, keepdims=True) + EPS) * g


_NT_DIMS = (((1,), (1,)), ((), ()))


def _inproj_kernel(x_ref, g_ref, wn_ref, wt_ref, wk_ref, ww_ref, kg_ref, qg_ref, nat_ref, tr_ref, kidx_ref,
                   wT_ref, h_ref, *, tn, q_scale):
    j = pl.program_id(1)
    n_nat = (D_POOL + D_ATTN) // tn

    @pl.when(j == 0)
    def _():
        h = _rms(x_ref[...], g_ref[...]).astype(BF16)
        h_ref[...] = h
        kidx_ref[...] = jnp.dot(h, wk_ref[...], preferred_element_type=F32).astype(BF16)
        w = lax.dot_general(ww_ref[...], h, _NT_DIMS, preferred_element_type=F32)
        wT_ref[...] = w * (IDX_HEADS ** -0.5) * (IDX_DIM ** -0.5)

    @pl.when(j < n_nat)
    def _():
        z = jnp.dot(h_ref[...], wn_ref[...], preferred_element_type=F32)

        @pl.when(j < D_POOL // tn)
        def _():
            nat_ref[...] = z.astype(BF16)

        @pl.when(j >= D_POOL // tn)
        def _():
            for c in range(tn // HEAD_DIM):
                sl = slice(c * HEAD_DIM, (c + 1) * HEAD_DIM)
                nat_ref[:, sl] = _rms(z[:, sl], kg_ref[...]).astype(BF16)

    @pl.when(j >= n_nat)
    def _():
        zT = lax.dot_general(wt_ref[...], h_ref[...], _NT_DIMS, preferred_element_type=F32)

        @pl.when(j < n_nat + D_ATTN // tn)
        def _():
            for c in range(tn // HEAD_DIM):
                sl = slice(c * HEAD_DIM, (c + 1) * HEAD_DIM)
                zc = zT[sl, :]
                inv = lax.rsqrt(jnp.mean(zc * zc, axis=0, keepdims=True) + EPS)
                tr_ref[sl, :] = (zc * inv * (qg_ref[...] * q_scale)).astype(BF16)

        @pl.when(j >= n_nat + D_ATTN // tn)
        def _():
            tr_ref[...] = zT.astype(BF16)


def _inproj(x2, g, w_nat, w_trT, w_kidx, w_widxT, kg, qg_col, *, tm, tn, q_scale):
    S = x2.shape[0]
    n_nat, n_tr = w_nat.shape[1] // tn, w_trT.shape[0] // tn
    return pl.pallas_call(
        functools.partial(_inproj_kernel, tn=tn, q_scale=q_scale),
        out_shape=(jax.ShapeDtypeStruct((S, w_nat.shape[1]), BF16),
                   jax.ShapeDtypeStruct((w_trT.shape[0], S), BF16),
                   jax.ShapeDtypeStruct((S, IDX_DIM), BF16),
                   jax.ShapeDtypeStruct((IDX_HEADS, S), F32)),
        grid=(S // tm, n_nat + n_tr),
        in_specs=[
            pl.BlockSpec((tm, D_MODEL), lambda i, j: (i, 0)),
            pl.BlockSpec((1, D_MODEL), lambda i, j: (0, 0)),
            pl.BlockSpec((D_MODEL, tn), lambda i, j: (0, jnp.minimum(j, n_nat - 1))),
            pl.BlockSpec((tn, D_MODEL), lambda i, j: (jnp.maximum(j - n_nat, 0), 0)),
            pl.BlockSpec((D_MODEL, IDX_DIM), lambda i, j: (0, 0)),
            pl.BlockSpec((IDX_HEADS, D_MODEL), lambda i, j: (0, 0)),
            pl.BlockSpec((1, HEAD_DIM), lambda i, j: (0, 0)),
            pl.BlockSpec((HEAD_DIM, 1), lambda i, j: (0, 0)),
        ],
        out_specs=(pl.BlockSpec((tm, tn), lambda i, j: (i, jnp.minimum(j, n_nat - 1))),
                   pl.BlockSpec((tn, tm), lambda i, j: (jnp.maximum(j - n_nat, 0), i)),
                   pl.BlockSpec((tm, IDX_DIM), lambda i, j: (i, 0)),
                   pl.BlockSpec((IDX_HEADS, tm), lambda i, j: (0, i))),
        scratch_shapes=[pltpu.VMEM((tm, D_MODEL), BF16)],
        compiler_params=_params(2),
        name="inproj",
    )(x2, g, w_nat, w_trT, w_kidx, w_widxT, kg, qg_col)


def _pool_kernel(u_ref, halo_ref, pw_ref, ps_ref, o_ref):
    i = pl.program_id(0)
    tm = u_ref.shape[0]
    u = u_ref[...].astype(F32)
    halo = jnp.where(i == 0, 0.0, halo_ref[...].astype(F32))
    ext = jnp.concatenate([halo, u], axis=0)
    t = i * tm + lax.broadcasted_iota(jnp.int32, (tm, 1), 0)
    for gi, w in enumerate(POOL_WINDOWS):
        sl = slice(gi * POOL_GROUP, (gi + 1) * POOL_GROUP)
        s = ext[:, sl]
        step = 1
        while step < w:
            s = s + pltpu.roll(s, step, axis=0)
            step *= 2
        cnt = jnp.minimum(t + 1, w).astype(F32)
        d = s[HALO:, :] / cnt - u[:, sl]
        y = jnp.dot(d.astype(BF16), pw_ref[gi], preferred_element_type=F32)
        o_ref[:, sl] = (y * ps_ref[:, sl]).astype(BF16)


def _pool(zm, pool_w, pool_scale, *, tm):
    S = zm.shape[0]
    return pl.pallas_call(
        _pool_kernel,
        out_shape=jax.ShapeDtypeStruct((S, D_POOL), BF16),
        grid=(S // tm,),
        in_specs=[
            pl.BlockSpec((tm, D_POOL), lambda i: (i, 0)),
            pl.BlockSpec((HALO, D_POOL), lambda i: (jnp.maximum(i * (tm // HALO) - 1, 0), 0)),
            pl.BlockSpec((len(POOL_WINDOWS), POOL_GROUP, POOL_GROUP), lambda i: (0, 0, 0)),
            pl.BlockSpec((1, D_POOL), lambda i: (0, 0)),
        ],
        out_specs=pl.BlockSpec((tm, D_POOL), lambda i: (i, 0)),
        compiler_params=_params(1),
        name="pool",
    )(zm, zm, pool_w, pool_scale)


I16 = jnp.int16
I16_MIN = -2 ** 15


def _index_kernel(kidx_ref, qiT_ref, wT_ref, bias_ref, hi_ref, lo_ref, *, T, CK, S, topk):
    qb = pl.program_id(0)
    nck = (qb * T + T + CK - 1) // CK
    q_pos = qb * T + lax.broadcasted_iota(jnp.int32, (1, T), 1)

    def score_body(c, carry):
        off = pl.multiple_of(c * CK, CK)
        kc = kidx_ref[pl.ds(off, CK), :]
        acc = None
        for h in range(IDX_HEADS):
            s = jnp.dot(kc, qiT_ref[h * IDX_DIM:(h + 1) * IDX_DIM, :], preferred_element_type=F32)
            t = wT_ref[h:h + 1, :] * jnp.maximum(s, 0.0)
            acc = t if acc is None else acc + t
        bits = lax.bitcast_convert_type(acc, jnp.int32)
        key = bits ^ ((bits >> 31) & jnp.int32(0x7FFFFFFF))
        key_pos = off + lax.broadcasted_iota(jnp.int32, (CK, 1), 0)
        key = jnp.where(key_pos <= q_pos, key, jnp.int32(-2 ** 31))
        hi_ref[pl.ds(off, CK), :] = (key >> 16).astype(I16)
        lo_ref[pl.ds(off, CK), :] = ((key & jnp.int32(0xFFFF)) + I16_MIN).astype(I16)
        return carry

    lax.fori_loop(0, nck, score_body, 0)

    def count_ge(ref, cand):
        c16 = cand.astype(I16)

        def count_body(c, acc):
            off = pl.multiple_of(c * CK, CK)
            ge = jnp.where(ref[pl.ds(off, CK), :] >= c16, jnp.ones((), I16), jnp.zeros((), I16))
            ge = ge.reshape(CK // 16, 16, T)
            parts = [ge[r] for r in range(CK // 16)]
            while len(parts) > 1:
                parts = [parts[i] + parts[i + 1] for i in range(0, len(parts), 2)]
            return acc + parts[0]

        acc = lax.fori_loop(0, nck, count_body, jnp.zeros((16, T), I16))
        return acc.astype(jnp.int32).sum(axis=0, keepdims=True)

    def kth_largest(ref, k):
        def bit_body(i, v):
            cand = v + (jnp.int32(1) << (15 - i))
            return jnp.where(count_ge(ref, cand) >= k, cand, v)

        return lax.fori_loop(0, 16, bit_body, jnp.full((1, T), I16_MIN, jnp.int32))

    H = kth_largest(hi_ref, topk)
    n_gt = jnp.where(H == -I16_MIN - 1, 0, count_ge(hi_ref, H + 1))
    H16 = H.astype(I16)

    def mask_body(c, carry):
        off = pl.multiple_of(c * CK, CK)
        lo_ref[pl.ds(off, CK), :] = jnp.where(hi_ref[pl.ds(off, CK), :] == H16, lo_ref[pl.ds(off, CK), :],
                                              jnp.full((), I16_MIN, I16))
        return carry

    lax.fori_loop(0, nck, mask_body, 0)
    L16 = kth_largest(lo_ref, topk - n_gt).astype(I16)

    def emit_body(c, carry):
        off = pl.multiple_of(c * CK, CK)
        hi = hi_ref[pl.ds(off, CK), :]
        sel = (hi > H16) | ((hi == H16) & (lo_ref[pl.ds(off, CK), :] >= L16))
        sel = sel & (hi > jnp.full((), I16_MIN, I16))
        bias_ref[0, pl.ds(off, CK), :] = jnp.where(sel, jnp.zeros((), BF16), jnp.full((), MASK_NEG, BF16))
        return carry

    lax.fori_loop(0, nck, emit_body, 0)

    def fill_body(c, carry):
        off = pl.multiple_of(c * CK, CK)
        bias_ref[0, pl.ds(off, CK), :] = jnp.full((CK, T), MASK_NEG, BF16)
        return carry

    lax.fori_loop(nck, S // CK, fill_body, 0)


def _index(kidx, tr, wT, *, T, CK, topk):
    S = kidx.shape[0]
    return pl.pallas_call(
        functools.partial(_index_kernel, T=T, CK=CK, S=S, topk=topk),
        out_shape=jax.ShapeDtypeStruct((S // T, S, T), BF16),
        grid=(S // T,),
        in_specs=[
            pl.BlockSpec((S, IDX_DIM), lambda q: (0, 0)),
            pl.BlockSpec((IDX_HEADS * IDX_DIM, T), lambda q: (2, q)),
            pl.BlockSpec((IDX_HEADS, T), lambda q: (0, q)),
        ],
        out_specs=pl.BlockSpec((1, S, T), lambda q: (q, 0, 0)),
        scratch_shapes=[pltpu.VMEM((S, T), I16), pltpu.VMEM((S, T), I16)],
        compiler_params=_params(1),
        name="index",
    )(kidx, tr, wT)


def _attn_kernel(qb_ref, kb_ref, kn_ref, qT_ref, vT_ref, bias_ref, o_ref, m_ref, l_ref, acc_ref, *, T, KB,
                 SUB, online):
    i = pl.program_id(0)
    qb = qb_ref[i]
    kb = kb_ref[i]

    @pl.when(kb == 0)
    def _():
        m_ref[...] = jnp.full(m_ref.shape, MASK_NEG, F32)
        l_ref[...] = jnp.zeros(l_ref.shape, F32)
        acc_ref[...] = jnp.zeros(acc_ref.shape, F32)

    items = [(kb0, h) for kb0 in range(0, KB, SUB) for h in range(N_HEADS)]

    def qk(item):
        kb0, h = item
        hs = slice(h * HEAD_DIM, (h + 1) * HEAD_DIM)
        return jnp.dot(kn_ref[kb0:kb0 + SUB, hs], qT_ref[hs, :], preferred_element_type=F32)

    ahead = 4
    pending = [qk(it) for it in items[:ahead]]
    bias = {kb0: bias_ref[0, kb0:kb0 + SUB, :].astype(F32) for kb0 in range(0, KB, SUB)}
    for n, (kb0, h) in enumerate(items):
        hs = slice(h * HEAD_DIM, (h + 1) * HEAD_DIM)
        s = bias[kb0] + pending.pop(0)
        if n + ahead < len(items):
            pending.append(qk(items[n + ahead]))
        if online:
            m_old = m_ref[h]
            m_new = jnp.maximum(m_old, s.max(axis=0, keepdims=True))
            alpha = jnp.exp2(m_old - m_new)
            p = jnp.exp2(s - m_new)
            l_ref[h] = alpha * l_ref[h] + p.sum(axis=0, keepdims=True)
            pv = jnp.dot(vT_ref[hs, kb0:kb0 + SUB], p.astype(BF16), preferred_element_type=F32)
            acc_ref[h] = alpha * acc_ref[h] + pv
            m_ref[h] = m_new
        else:
            p = jnp.exp2(s)
            l_ref[h] += p.sum(axis=0, keepdims=True)
            acc_ref[h] += jnp.dot(vT_ref[hs, kb0:kb0 + SUB], p.astype(BF16), preferred_element_type=F32)

    @pl.when(kb == (qb * T + T - 1) // KB)
    def _():
        for h in range(N_HEADS):
            o_ref[:, h * HEAD_DIM:(h + 1) * HEAD_DIM] = (acc_ref[h] / l_ref[h]).T.astype(BF16)


def _attn(nat, tr, bias, *, T, KB, SUB, online):
    S = nat.shape[0]
    pairs = [(q, k) for q in range(S // T) for k in range((q * T + T - 1) // KB + 1)]
    qb_ids = jnp.asarray(np.array([p[0] for p in pairs], np.int32))
    kb_ids = jnp.asarray(np.array([p[1] for p in pairs], np.int32))
    grid_spec = pltpu.PrefetchScalarGridSpec(
        num_scalar_prefetch=2,
        grid=(len(pairs),),
        in_specs=[
            pl.BlockSpec((KB, D_ATTN), lambda i, qb, kb: (kb[i], 1)),
            pl.BlockSpec((D_ATTN, T), lambda i, qb, kb: (0, qb[i])),
            pl.BlockSpec((D_ATTN, KB), lambda i, qb, kb: (1, kb[i])),
            pl.BlockSpec((1, KB, T), lambda i, qb, kb: (qb[i], kb[i], 0)),
        ],
        out_specs=pl.BlockSpec((T, D_ATTN), lambda i, qb, kb: (qb[i], 0)),
        scratch_shapes=[pltpu.VMEM((N_HEADS, 1, T), F32), pltpu.VMEM((N_HEADS, 1, T), F32),
                        pltpu.VMEM((N_HEADS, HEAD_DIM, T), F32)],
    )
    return pl.pallas_call(
        functools.partial(_attn_kernel, T=T, KB=KB, SUB=SUB, online=online),
        out_shape=jax.ShapeDtypeStruct((S, D_ATTN), BF16),
        grid_spec=grid_spec,
        compiler_params=_params(1),
        name="attn_online" if online else "attn",
    )(qb_ids, kb_ids, nat, tr, tr, bias)


def _outproj_kernel(yp_ref, ya_ref, wp_ref, wa_ref, x_ref, o_ref):
    acc = jnp.dot(yp_ref[...], wp_ref[...], preferred_element_type=F32)
    acc = acc + jnp.dot(ya_ref[...], wa_ref[...], preferred_element_type=F32)
    o_ref[...] = x_ref[...] + acc


def _outproj(yp, ya, w_out, x2, *, tm, tn):
    S = x2.shape[0]
    n_j = D_MODEL // tn
    return pl.pallas_call(
        _outproj_kernel,
        out_shape=jax.ShapeDtypeStruct((S, D_MODEL), F32),
        grid=(S // tm, n_j),
        in_specs=[
            pl.BlockSpec((tm, D_POOL), lambda i, j: (i, 0)),
            pl.BlockSpec((tm, D_ATTN), lambda i, j: (i, 0)),
            pl.BlockSpec((D_POOL, tn), lambda i, j: (0, j)),
            pl.BlockSpec((D_ATTN, tn), lambda i, j: (D_POOL // D_ATTN, j)),
            pl.BlockSpec((tm, tn), lambda i, j: (i, j)),
        ],
        out_specs=pl.BlockSpec((tm, tn), lambda i, j: (i, j)),
        compiler_params=_params(2),
        name="outproj",
    )(yp, ya, w_out, w_out, x2)


def _ffn_kernel(x_ref, halo_ref, g_ref, wg_ref, wv_ref, cwg_ref, cwv_ref, cbg_ref, cbv_ref, wd_ref,
                o_ref, h_ref, act_ref, acc_ref):
    i = pl.program_id(0)
    f = pl.program_id(1)
    n_f = pl.num_programs(1) - 1

    def up_proj(w_ref):
        return jnp.dot(h_ref[...], w_ref[...], preferred_element_type=F32)

    def conv(up, cw_ref, cb_ref):
        c = (cw_ref[0:1, :] * pltpu.roll(up, 2, axis=0) + cw_ref[1:2, :] * pltpu.roll(up, 1, axis=0)
             + cw_ref[2:3, :] * up)
        return cb_ref[...] + c[HALO:, :]

    def activation(up_g, up_v):
        cg = conv(up_g, cwg_ref, cbg_ref)
        cv = conv(up_v, cwv_ref, cbv_ref)
        return (cg * (1.0 / (1.0 + jnp.exp(-cg))) * cv).astype(BF16)

    def down_proj(act):
        return jnp.dot(act, wd_ref[...], preferred_element_type=F32)

    @pl.when(f == 0)
    def _():
        hh = jnp.where(i == 0, 0.0, _rms(halo_ref[...], g_ref[...]))
        h_ref[0:HALO, :] = hh.astype(BF16)
        h_ref[HALO:, :] = _rms(x_ref[...], g_ref[...]).astype(BF16)
        acc_ref[...] = jnp.zeros(acc_ref.shape, F32)
        act_ref[...] = activation(up_proj(wg_ref), up_proj(wv_ref))

    @pl.when((f > 0) & (f < n_f))
    def _():
        prev = act_ref[...]
        up_g, up_v = up_proj(wg_ref), up_proj(wv_ref)
        acc_ref[...] += down_proj(prev)
        act_ref[...] = activation(up_g, up_v)

    @pl.when(f == n_f)
    def _():
        o_ref[...] = x_ref[...] + acc_ref[...] + down_proj(act_ref[...])


def _ffn(x1, g, w_up, conv_w, conv_b, w_down, *, tm, tf):
    S = x1.shape[0]
    n_f = D_FF // tf
    assert n_f >= 2
    cur = lambda f: jnp.minimum(f, n_f - 1)
    prev = lambda f: jnp.maximum(f - 1, 0)
    return pl.pallas_call(
        _ffn_kernel,
        out_shape=jax.ShapeDtypeStruct((S, D_MODEL), F32),
        grid=(S // tm, n_f + 1),
        in_specs=[
            pl.BlockSpec((tm, D_MODEL), lambda i, f: (i, 0)),
            pl.BlockSpec((HALO, D_MODEL), lambda i, f: (jnp.maximum(i * (tm // HALO) - 1, 0), 0)),
            pl.BlockSpec((1, D_MODEL), lambda i, f: (0, 0)),
            pl.BlockSpec((D_MODEL, tf), lambda i, f: (0, cur(f))),
            pl.BlockSpec((D_MODEL, tf), lambda i, f: (0, cur(f) + n_f)),
            pl.BlockSpec((CONV_WIDTH, tf), lambda i, f: (0, cur(f))),
            pl.BlockSpec((CONV_WIDTH, tf), lambda i, f: (0, cur(f) + n_f)),
            pl.BlockSpec((1, tf), lambda i, f: (0, cur(f))),
            pl.BlockSpec((1, tf), lambda i, f: (0, cur(f) + n_f)),
            pl.BlockSpec((tf, D_MODEL), lambda i, f: (prev(f), 0)),
        ],
        out_specs=pl.BlockSpec((tm, D_MODEL), lambda i, f: (i, 0)),
        scratch_shapes=[pltpu.VMEM((HALO + tm, D_MODEL), BF16), pltpu.VMEM((tm, tf), BF16),
                        pltpu.VMEM((tm, D_MODEL), F32)],
        compiler_params=_params(2),
        name="ffn",
    )(x1, x1, g, w_up, w_up, conv_w, conv_w, conv_b, conv_b, w_down)


def kernel(x, attn_norm_g, w_in, pool_w, pool_scale, q_norm_g, k_norm_g, w_out, ffn_norm_g, w_up,
           conv_w, conv_b, w_down):
    B, S, D = x.shape
    assert B == 1 and D == D_MODEL and w_in.shape == (D_MODEL, D_MAIN + D_TAIL)
    T = 256
    assert S % 1024 == 0
    topk = min(TOPK_MAX, S // 4)
    q_scale = HEAD_DIM ** -0.5 * LOG2E
    x2 = x.reshape(S, D)

    o = D_POOL
    w_q, w_k, w_v = (w_in[:, o + n * D_ATTN:o + (n + 1) * D_ATTN] for n in range(3))
    w_qi = w_in[:, o + 3 * D_ATTN:D_MAIN]
    w_nat = jnp.concatenate([w_in[:, :o], w_k], axis=1).astype(BF16)
    w_trT = jnp.concatenate([w_q, w_v, w_qi], axis=1).T.astype(BF16)
    w_kidx = w_in[:, D_MAIN:D_MAIN + IDX_DIM].astype(BF16)
    w_widxT = w_in[:, D_MAIN + IDX_DIM:].T.astype(BF16)
    nat, tr, kidx, wT = _inproj(x2, attn_norm_g.reshape(1, D), w_nat, w_trT, w_kidx, w_widxT,
                                k_norm_g.reshape(1, HEAD_DIM), q_norm_g.reshape(HEAD_DIM, 1),
                                tm=1024, tn=512, q_scale=q_scale)

    y_pool = _pool(nat, pool_w.astype(BF16), pool_scale.reshape(1, D_POOL), tm=1024)
    bias = _index(kidx, tr, wT, T=T, CK=512, topk=topk)
    score_bound = HEAD_DIM * jnp.max(jnp.abs(q_norm_g)) * jnp.max(jnp.abs(k_norm_g)) * q_scale
    attn = functools.partial(_attn, T=T, KB=512, SUB=512)
    y_attn = lax.cond(score_bound < MAX_UNSHIFTED_LOG2,
                      functools.partial(attn, online=False), functools.partial(attn, online=True),
                      nat, tr, bias)

    x1 = _outproj(y_pool, y_attn, w_out.astype(BF16), x2, tm=1024, tn=1024)
    out = _ffn(x1, ffn_norm_g.reshape(1, D), w_up.astype(BF16), conv_w, conv_b.reshape(1, 2 * D_FF),
               w_down.astype(BF16), tm=512, tf=512)
    return out.reshape(B, S, D)
```

```python
import functools

import numpy as np
import jax
import jax.numpy as jnp
from jax import lax
from jax.experimental import pallas as pl
from jax.experimental.pallas import tpu as pltpu

D_MODEL = 2048
D_POOL = 1024
POOL_WINDOWS = (2, 4, 8, 16)
POOL_GROUP = D_POOL // len(POOL_WINDOWS)
D_ATTN = 1024
HEAD_DIM = 128
N_HEADS = D_ATTN // HEAD_DIM
IDX_HEADS = 16
IDX_DIM = 64
TOPK_MAX = 256
D_FF = 5632
CONV_WIDTH = 3
EPS = 1e-6

D_MAIN = D_POOL + 3 * D_ATTN + IDX_HEADS * IDX_DIM
D_TAIL = IDX_DIM + IDX_HEADS
LANES = 128
HALO = 16
MASK_NEG = -1e30
LOG2E = 1.4426950408889634
MAX_UNSHIFTED_LOG2 = 60.0
VMEM_LIMIT_BYTES = 56 * 1024 * 1024

F32 = jnp.float32
BF16 = jnp.bfloat16


def _params(n_axes):
    return pltpu.CompilerParams(dimension_semantics=("arbitrary",) * n_axes,
                                vmem_limit_bytes=VMEM_LIMIT_BYTES)


def _rms(xf, g):
    return xf * lax.rsqrt(jnp.mean(xf * xf, axis=-1, keepdims=True) + EPS) * g


_NT_DIMS = (((1,), (1,)), ((), ()))


def _inproj_kernel(x_ref, g_ref, wn_ref, wt_ref, wk_ref, ww_ref, kg_ref, qg_ref, nat_ref, tr_ref, kidx_ref,
                   wT_ref, h_ref, *, tn, q_scale):
    j = pl.program_id(1)
    n_nat = (D_POOL + D_ATTN) // tn

    @pl.when(j == 0)
    def _():
        h = _rms(x_ref[...], g_ref[...]).astype(BF16)
        h_ref[...] = h
        kidx_ref[...] = jnp.dot(h, wk_ref[...], preferred_element_type=F32).astype(BF16)
        w = lax.dot_general(ww_ref[...], h, _NT_DIMS, preferred_element_type=F32)
        wT_ref[...] = w * (IDX_HEADS ** -0.5) * (IDX_DIM ** -0.5)

    @pl.when(j < n_nat)
    def _():
        z = jnp.dot(h_ref[...], wn_ref[...], preferred_element_type=F32)

        @pl.when(j < D_POOL // tn)
        def _():
            nat_ref[...] = z.astype(BF16)

        @pl.when(j >= D_POOL // tn)
        def _():
            for c in range(tn // HEAD_DIM):
                sl = slice(c * HEAD_DIM, (c + 1) * HEAD_DIM)
                nat_ref[:, sl] = _rms(z[:, sl], kg_ref[...]).astype(BF16)

    @pl.when(j >= n_nat)
    def _():
        zT = lax.dot_general(wt_ref[...], h_ref[...], _NT_DIMS, preferred_element_type=F32)

        @pl.when(j < n_nat + D_ATTN // tn)
        def _():
            for c in range(tn // HEAD_DIM):
                sl = slice(c * HEAD_DIM, (c + 1) * HEAD_DIM)
                zc = zT[sl, :]
                inv = lax.rsqrt(jnp.mean(zc * zc, axis=0, keepdims=True) + EPS)
                tr_ref[sl, :] = (zc * inv * (qg_ref[...] * q_scale)).astype(BF16)

        @pl.when(j >= n_nat + D_ATTN // tn)
        def _():
            tr_ref[...] = zT.astype(BF16)


def _inproj(x2, g, w_nat, w_trT, w_kidx, w_widxT, kg, qg_col, *, tm, tn, q_scale):
    S = x2.shape[0]
    n_nat, n_tr = w_nat.shape[1] // tn, w_trT.shape[0] // tn
    return pl.pallas_call(
        functools.partial(_inproj_kernel, tn=tn, q_scale=q_scale),
        out_shape=(jax.ShapeDtypeStruct((S, w_nat.shape[1]), BF16),
                   jax.ShapeDtypeStruct((w_trT.shape[0], S), BF16),
                   jax.ShapeDtypeStruct((S, IDX_DIM), BF16),
                   jax.ShapeDtypeStruct((IDX_HEADS, S), F32)),
        grid=(S // tm, n_nat + n_tr),
        in_specs=[
            pl.BlockSpec((tm, D_MODEL), lambda i, j: (i, 0)),
            pl.BlockSpec((1, D_MODEL), lambda i, j: (0, 0)),
            pl.BlockSpec((D_MODEL, tn), lambda i, j: (0, jnp.minimum(j, n_nat - 1))),
            pl.BlockSpec((tn, D_MODEL), lambda i, j: (jnp.maximum(j - n_nat, 0), 0)),
            pl.BlockSpec((D_MODEL, IDX_DIM), lambda i, j: (0, 0)),
            pl.BlockSpec((IDX_HEADS, D_MODEL), lambda i, j: (0, 0)),
            pl.BlockSpec((1, HEAD_DIM), lambda i, j: (0, 0)),
            pl.BlockSpec((HEAD_DIM, 1), lambda i, j: (0, 0)),
        ],
        out_specs=(pl.BlockSpec((tm, tn), lambda i, j: (i, jnp.minimum(j, n_nat - 1))),
                   pl.BlockSpec((tn, tm), lambda i, j: (jnp.maximum(j - n_nat, 0), i)),
                   pl.BlockSpec((tm, IDX_DIM), lambda i, j: (i, 0)),
                   pl.BlockSpec((IDX_HEADS, tm), lambda i, j: (0, i))),
        scratch_shapes=[pltpu.VMEM((tm, D_MODEL), BF16)],
        compiler_params=_params(2),
        name="inproj",
    )(x2, g, w_nat, w_trT, w_kidx, w_widxT, kg, qg_col)


def _pool_kernel(u_ref, halo_ref, pw_ref, ps_ref, o_ref):
    i = pl.program_id(0)
    tm = u_ref.shape[0]
    u = u_ref[...].astype(F32)
    halo = jnp.where(i == 0, 0.0, halo_ref[...].astype(F32))
    ext = jnp.concatenate([halo, u], axis=0)
    t = i * tm + lax.broadcasted_iota(jnp.int32, (tm, 1), 0)
    for gi, w in enumerate(POOL_WINDOWS):
        sl = slice(gi * POOL_GROUP, (gi + 1) * POOL_GROUP)
        s = ext[:, sl]
        step = 1
        while step < w:
            s = s + pltpu.roll(s, step, axis=0)
            step *= 2
        cnt = jnp.minimum(t + 1, w).astype(F32)
        d = s[HALO:, :] / cnt - u[:, sl]
        y = jnp.dot(d.astype(BF16), pw_ref[gi], preferred_element_type=F32)
        o_ref[:, sl] = (y * ps_ref[:, sl]).astype(BF16)


def _pool(zm, pool_w, pool_scale, *, tm):
    S = zm.shape[0]
    return pl.pallas_call(
        _pool_kernel,
        out_shape=jax.ShapeDtypeStruct((S, D_POOL), BF16),
        grid=(S // tm,),
        in_specs=[
            pl.BlockSpec((tm, D_POOL), lambda i: (i, 0)),
            pl.BlockSpec((HALO, D_POOL), lambda i: (jnp.maximum(i * (tm // HALO) - 1, 0), 0)),
            pl.BlockSpec((len(POOL_WINDOWS), POOL_GROUP, POOL_GROUP), lambda i: (0, 0, 0)),
            pl.BlockSpec((1, D_POOL), lambda i: (0, 0)),
        ],
        out_specs=pl.BlockSpec((tm, D_POOL), lambda i: (i, 0)),
        compiler_params=_params(1),
        name="pool",
    )(zm, zm, pool_w, pool_scale)


I16 = jnp.int16
I16_MIN = -2 ** 15


def _index_kernel(kidx_ref, qiT_ref, wT_ref, bias_ref, hi_ref, lo_ref, *, T, CK, S, topk):
    qb = pl.program_id(0)
    nck = (qb * T + T + CK - 1) // CK
    q_pos = qb * T + lax.broadcasted_iota(jnp.int32, (1, T), 1)

    def score_chunk(off):
        kc = kidx_ref[pl.ds(off, CK), :]
        acc = None
        for h in range(IDX_HEADS):
            s = jnp.dot(kc, qiT_ref[h * IDX_DIM:(h + 1) * IDX_DIM, :], preferred_element_type=F32)
            t = wT_ref[h:h + 1, :] * jnp.maximum(s, 0.0)
            acc = t if acc is None else acc + t
        bits = lax.bitcast_convert_type(acc, jnp.int32)
        key = bits ^ ((bits >> 31) & jnp.int32(0x7FFFFFFF))
        key_pos = off + lax.broadcasted_iota(jnp.int32, (CK, 1), 0)
        key = jnp.where(key_pos <= q_pos, key, jnp.int32(-2 ** 31))
        hi_ref[pl.ds(off, CK), :] = (key >> 16).astype(I16)
        lo_ref[pl.ds(off, CK), :] = ((key & jnp.int32(0xFFFF)) + I16_MIN).astype(I16)

    def score_body(c, carry):
        for u in range(2):
            score_chunk(pl.multiple_of((2 * c + u) * CK, CK))
        return carry

    lax.fori_loop(0, (nck + 1) // 2, score_body, 0)

    def count_ge(ref, cand):
        c16 = cand.astype(I16)

        def count_body(c, acc):
            off = pl.multiple_of(c * CK, CK)
            ge = jnp.where(ref[pl.ds(off, CK), :] >= c16, jnp.ones((), I16), jnp.zeros((), I16))
            ge = ge.reshape(CK // 16, 16, T)
            parts = [ge[r] for r in range(CK // 16)]
            while len(parts) > 1:
                parts = [parts[i] + parts[i + 1] for i in range(0, len(parts), 2)]
            return acc + parts[0]

        acc = lax.fori_loop(0, nck, count_body, jnp.zeros((16, T), I16))
        return acc.astype(jnp.int32).sum(axis=0, keepdims=True)

    def kth_largest(ref, k):
        def bit_body(i, v):
            cand = v + (jnp.int32(1) << (15 - i))
            return jnp.where(count_ge(ref, cand) >= k, cand, v)

        return lax.fori_loop(0, 16, bit_body, jnp.full((1, T), I16_MIN, jnp.int32))

    H = kth_largest(hi_ref, topk)
    n_gt = jnp.where(H == -I16_MIN - 1, 0, count_ge(hi_ref, H + 1))
    H16 = H.astype(I16)

    def mask_body(c, carry):
        off = pl.multiple_of(c * CK, CK)
        lo_ref[pl.ds(off, CK), :] = jnp.where(hi_ref[pl.ds(off, CK), :] == H16, lo_ref[pl.ds(off, CK), :],
                                              jnp.full((), I16_MIN, I16))
        return carry

    lax.fori_loop(0, nck, mask_body, 0)
    L16 = kth_largest(lo_ref, topk - n_gt).astype(I16)

    def emit_body(c, carry):
        off = pl.multiple_of(c * CK, CK)
        hi = hi_ref[pl.ds(off, CK), :]
        sel = (hi > H16) | ((hi == H16) & (lo_ref[pl.ds(off, CK), :] >= L16))
        sel = sel & (hi > jnp.full((), I16_MIN, I16))
        bias_ref[0, pl.ds(off, CK), :] = jnp.where(sel, jnp.zeros((), BF16), jnp.full((), MASK_NEG, BF16))
        return carry

    lax.fori_loop(0, nck, emit_body, 0)

    def fill_body(c, carry):
        off = pl.multiple_of(c * CK, CK)
        bias_ref[0, pl.ds(off, CK), :] = jnp.full((CK, T), MASK_NEG, BF16)
        return carry

    lax.fori_loop(nck, S // CK, fill_body, 0)


def _index(kidx, tr, wT, *, T, CK, topk):
    S = kidx.shape[0]
    assert S % (2 * CK) == 0 and CK % T == 0
    return pl.pallas_call(
        functools.partial(_index_kernel, T=T, CK=CK, S=S, topk=topk),
        out_shape=jax.ShapeDtypeStruct((S // T, S, T), BF16),
        grid=(S // T,),
        in_specs=[
            pl.BlockSpec((S, IDX_DIM), lambda q: (0, 0)),
            pl.BlockSpec((IDX_HEADS * IDX_DIM, T), lambda q: (2, q)),
            pl.BlockSpec((IDX_HEADS, T), lambda q: (0, q)),
        ],
        out_specs=pl.BlockSpec((1, S, T), lambda q: (q, 0, 0)),
        scratch_shapes=[pltpu.VMEM((S, T), I16), pltpu.VMEM((S, T), I16)],
        compiler_params=_params(1),
        name="index",
    )(kidx, tr, wT)


def _attn_kernel(qb_ref, kb_ref, kn_ref, qT_ref, vT_ref, bias_ref, o_ref, m_ref, l_ref, acc_ref, *, T, KB,
                 SUB, online):
    i = pl.program_id(0)
    qb = qb_ref[i]
    kb = kb_ref[i]

    @pl.when(kb == 0)
    def _():
        m_ref[...] = jnp.full(m_ref.shape, MASK_NEG, F32)
        l_ref[...] = jnp.zeros(l_ref.shape, F32)
        acc_ref[...] = jnp.zeros(acc_ref.shape, F32)

    items = [(kb0, h) for kb0 in range(0, KB, SUB) for h in range(N_HEADS)]

    def qk(item):
        kb0, h = item
        hs = slice(h * HEAD_DIM, (h + 1) * HEAD_DIM)
        return jnp.dot(kn_ref[kb0:kb0 + SUB, hs], qT_ref[hs, :], preferred_element_type=F32)

    ahead = 4
    pending = [qk(it) for it in items[:ahead]]
    bias = {kb0: bias_ref[0, kb0:kb0 + SUB, :].astype(F32) for kb0 in range(0, KB, SUB)}
    for n, (kb0, h) in enumerate(items):
        hs = slice(h * HEAD_DIM, (h + 1) * HEAD_DIM)
        s = bias[kb0] + pending.pop(0)
        if n + ahead < len(items):
            pending.append(qk(items[n + ahead]))
        if online:
            m_old = m_ref[h]
            m_new = jnp.maximum(m_old, s.max(axis=0, keepdims=True))
            alpha = jnp.exp2(m_old - m_new)
            p = jnp.exp2(s - m_new)
            l_ref[h] = alpha * l_ref[h] + p.sum(axis=0, keepdims=True)
            pv = jnp.dot(vT_ref[hs, kb0:kb0 + SUB], p.astype(BF16), preferred_element_type=F32)
            acc_ref[h] = alpha * acc_ref[h] + pv
            m_ref[h] = m_new
        else:
            p = jnp.exp2(s)
            l_ref[h] += p.sum(axis=0, keepdims=True)
            acc_ref[h] += jnp.dot(vT_ref[hs, kb0:kb0 + SUB], p.astype(BF16), preferred_element_type=F32)

    @pl.when(kb == (qb * T + T - 1) // KB)
    def _():
        for h in range(N_HEADS):
            o_ref[:, h * HEAD_DIM:(h + 1) * HEAD_DIM] = (acc_ref[h] / l_ref[h]).T.astype(BF16)


def _attn(nat, tr, bias, *, T, KB, SUB, online):
    S = nat.shape[0]
    pairs = [(q, k) for q in range(S // T) for k in range((q * T + T - 1) // KB + 1)]
    qb_ids = jnp.asarray(np.array([p[0] for p in pairs], np.int32))
    kb_ids = jnp.asarray(np.array([p[1] for p in pairs], np.int32))
    grid_spec = pltpu.PrefetchScalarGridSpec(
        num_scalar_prefetch=2,
        grid=(len(pairs),),
        in_specs=[
            pl.BlockSpec((KB, D_ATTN), lambda i, qb, kb: (kb[i], 1)),
            pl.BlockSpec((D_ATTN, T), lambda i, qb, kb: (0, qb[i])),
            pl.BlockSpec((D_ATTN, KB), lambda i, qb, kb: (1, kb[i])),
            pl.BlockSpec((1, KB, T), lambda i, qb, kb: (qb[i], kb[i], 0)),
        ],
        out_specs=pl.BlockSpec((T, D_ATTN), lambda i, qb, kb: (qb[i], 0)),
        scratch_shapes=[pltpu.VMEM((N_HEADS, 1, T), F32), pltpu.VMEM((N_HEADS, 1, T), F32),
                        pltpu.VMEM((N_HEADS, HEAD_DIM, T), F32)],
    )
    return pl.pallas_call(
        functools.partial(_attn_kernel, T=T, KB=KB, SUB=SUB, online=online),
        out_shape=jax.ShapeDtypeStruct((S, D_ATTN), BF16),
        grid_spec=grid_spec,
        compiler_params=_params(1),
        name="attn_online" if online else "attn",
    )(qb_ids, kb_ids, nat, tr, tr, bias)


def _outproj_kernel(yp_ref, ya_ref, wp_ref, wa_ref, x_ref, o_ref):
    acc = jnp.dot(yp_ref[...], wp_ref[...], preferred_element_type=F32)
    acc = acc + jnp.dot(ya_ref[...], wa_ref[...], preferred_element_type=F32)
    o_ref[...] = x_ref[...] + acc


def _outproj(yp, ya, w_out, x2, *, tm, tn):
    S = x2.shape[0]
    n_j = D_MODEL // tn
    return pl.pallas_call(
        _outproj_kernel,
        out_shape=jax.ShapeDtypeStruct((S, D_MODEL), F32),
        grid=(S // tm, n_j),
        in_specs=[
            pl.BlockSpec((tm, D_POOL), lambda i, j: (i, 0)),
            pl.BlockSpec((tm, D_ATTN), lambda i, j: (i, 0)),
            pl.BlockSpec((D_POOL, tn), lambda i, j: (0, j)),
            pl.BlockSpec((D_ATTN, tn), lambda i, j: (D_POOL // D_ATTN, j)),
            pl.BlockSpec((tm, tn), lambda i, j: (i, j)),
        ],
        out_specs=pl.BlockSpec((tm, tn), lambda i, j: (i, j)),
        compiler_params=_params(2),
        name="outproj",
    )(yp, ya, w_out, w_out, x2)


def _ffn_kernel(x_ref, halo_ref, g_ref, wg_ref, wv_ref, cwg_ref, cwv_ref, cbg_ref, cbv_ref, wd_ref,
                o_ref, h_ref, acc_ref):
    i = pl.program_id(0)
    f = pl.program_id(1)

    @pl.when(f == 0)
    def _():
        hh = jnp.where(i == 0, 0.0, _rms(halo_ref[...], g_ref[...]))
        h_ref[0:HALO, :] = hh.astype(BF16)
        h_ref[HALO:, :] = _rms(x_ref[...], g_ref[...]).astype(BF16)
        acc_ref[...] = jnp.zeros(acc_ref.shape, F32)

    h = h_ref[...]

    def conv(w_ref, cw_ref, cb_ref):
        up = jnp.dot(h, w_ref[...], preferred_element_type=F32)
        c = (cw_ref[0:1, :] * pltpu.roll(up, 2, axis=0) + cw_ref[1:2, :] * pltpu.roll(up, 1, axis=0)
             + cw_ref[2:3, :] * up)
        return cb_ref[...] + c[HALO:, :]

    cg = conv(wg_ref, cwg_ref, cbg_ref)
    cv = conv(wv_ref, cwv_ref, cbv_ref)
    act = cg * (1.0 / (1.0 + jnp.exp(-cg))) * cv
    acc_ref[...] += jnp.dot(act.astype(BF16), wd_ref[...], preferred_element_type=F32)

    @pl.when(f == pl.num_programs(1) - 1)
    def _():
        o_ref[...] = x_ref[...] + acc_ref[...]


def _ffn(x1, g, w_up, conv_w, conv_b, w_down, *, tm, tf):
    S = x1.shape[0]
    n_f = D_FF // tf
    return pl.pallas_call(
        _ffn_kernel,
        out_shape=jax.ShapeDtypeStruct((S, D_MODEL), F32),
        grid=(S // tm, n_f),
        in_specs=[
            pl.BlockSpec((tm, D_MODEL), lambda i, f: (i, 0)),
            pl.BlockSpec((HALO, D_MODEL), lambda i, f: (jnp.maximum(i * (tm // HALO) - 1, 0), 0)),
            pl.BlockSpec((1, D_MODEL), lambda i, f: (0, 0)),
            pl.BlockSpec((D_MODEL, tf), lambda i, f: (0, f)),
            pl.BlockSpec((D_MODEL, tf), lambda i, f: (0, f + n_f)),
            pl.BlockSpec((CONV_WIDTH, tf), lambda i, f: (0, f)),
            pl.BlockSpec((CONV_WIDTH, tf), lambda i, f: (0, f + n_f)),
            pl.BlockSpec((1, tf), lambda i, f: (0, f)),
            pl.BlockSpec((1, tf), lambda i, f: (0, f + n_f)),
            pl.BlockSpec((tf, D_MODEL), lambda i, f: (f, 0)),
        ],
        out_specs=pl.BlockSpec((tm, D_MODEL), lambda i, f: (i, 0)),
        scratch_shapes=[pltpu.VMEM((HALO + tm, D_MODEL), BF16), pltpu.VMEM((tm, D_MODEL), F32)],
        compiler_params=_params(2),
        name="ffn",
    )(x1, x1, g, w_up, w_up, conv_w, conv_w, conv_b, conv_b, w_down)


def kernel(x, attn_norm_g, w_in, pool_w, pool_scale, q_norm_g, k_norm_g, w_out, ffn_norm_g, w_up,
           conv_w, conv_b, w_down):
    B, S, D = x.shape
    assert B == 1 and D == D_MODEL and w_in.shape == (D_MODEL, D_MAIN + D_TAIL)
    T = 256
    assert S % 1024 == 0
    topk = min(TOPK_MAX, S // 4)
    q_scale = HEAD_DIM ** -0.5 * LOG2E
    x2 = x.reshape(S, D)

    o = D_POOL
    w_q, w_k, w_v = (w_in[:, o + n * D_ATTN:o + (n + 1) * D_ATTN] for n in range(3))
    w_qi = w_in[:, o + 3 * D_ATTN:D_MAIN]
    w_nat = jnp.concatenate([w_in[:, :o], w_k], axis=1).astype(BF16)
    w_trT = jnp.concatenate([w_q, w_v, w_qi], axis=1).T.astype(BF16)
    w_kidx = w_in[:, D_MAIN:D_MAIN + IDX_DIM].astype(BF16)
    w_widxT = w_in[:, D_MAIN + IDX_DIM:].T.astype(BF16)
    nat, tr, kidx, wT = _inproj(x2, attn_norm_g.reshape(1, D), w_nat, w_trT, w_kidx, w_widxT,
                                k_norm_g.reshape(1, HEAD_DIM), q_norm_g.reshape(HEAD_DIM, 1),
                                tm=1024, tn=1024, q_scale=q_scale)

    y_pool = _pool(nat, pool_w.astype(BF16), pool_scale.reshape(1, D_POOL), tm=1024)
    bias = _index(kidx, tr, wT, T=T, CK=512, topk=topk)
    score_bound = HEAD_DIM * jnp.max(jnp.abs(q_norm_g)) * jnp.max(jnp.abs(k_norm_g)) * q_scale
    attn = functools.partial(_attn, T=T, KB=1024, SUB=512)
    y_attn = lax.cond(score_bound < MAX_UNSHIFTED_LOG2,
                      functools.partial(attn, online=False), functools.partial(attn, online=True),
                      nat, tr, bias)

    x1 = _outproj(y_pool, y_attn, w_out.astype(BF16), x2, tm=1024, tn=1024)
    out = _ffn(x1, ffn_norm_g.reshape(1, D), w_up.astype(BF16), conv_w, conv_b.reshape(1, 2 * D_FF),
               w_down.astype(BF16), tm=512, tf=512)
    return out.reshape(B, S, D)
```

```python
import functools

import numpy as np
import jax
import jax.numpy as jnp
from jax import lax
from jax.experimental import pallas as pl
from jax.experimental.pallas import tpu as pltpu

D_MODEL = 2048
D_POOL = 1024
POOL_WINDOWS = (2, 4, 8, 16)
POOL_GROUP = D_POOL // len(POOL_WINDOWS)
D_ATTN = 1024
HEAD_DIM = 128
N_HEADS = D_ATTN // HEAD_DIM
IDX_HEADS = 16
IDX_DIM = 64
TOPK_MAX = 256
D_FF = 5632
CONV_WIDTH = 3
EPS = 1e-6

D_MAIN = D_POOL + 3 * D_ATTN + IDX_HEADS * IDX_DIM
D_TAIL = IDX_DIM + IDX_HEADS
LANES = 128
HALO = 16
MASK_NEG = -1e30
LOG2E = 1.4426950408889634
FFN_SUB = 1
MAX_UNSHIFTED_LOG2 = 60.0
VMEM_LIMIT_BYTES = 56 * 1024 * 1024

F32 = jnp.float32
BF16 = jnp.bfloat16


def _params(n_axes):
    return pltpu.CompilerParams(dimension_semantics=("arbitrary",) * n_axes,
                                vmem_limit_bytes=VMEM_LIMIT_BYTES)


def _rms(xf, g):
    return xf * lax.rsqrt(jnp.mean(xf * xf, axis=-1, keepdims=True) + EPS) * g


_NT_DIMS = (((1,), (1,)), ((), ()))


def _inproj_kernel(x_ref, g_ref, wn_ref, wt_ref, wk_ref, ww_ref, kg_ref, qg_ref, nat_ref, tr_ref, kidx_ref,
                   wT_ref, h_ref, *, tn, q_scale):
    j = pl.program_id(1)
    n_nat = (D_POOL + D_ATTN) // tn

    @pl.when(j == 0)
    def _():
        h = _rms(x_ref[...], g_ref[...]).astype(BF16)
        h_ref[...] = h
        kidx_ref[...] = jnp.dot(h, wk_ref[...], preferred_element_type=F32).astype(BF16)
        w = lax.dot_general(ww_ref[...], h, _NT_DIMS, preferred_element_type=F32)
        wT_ref[...] = w * (IDX_HEADS ** -0.5) * (IDX_DIM ** -0.5)

    @pl.when(j < n_nat)
    def _():
        z = jnp.dot(h_ref[...], wn_ref[...], preferred_element_type=F32)

        @pl.when(j < D_POOL // tn)
        def _():
            nat_ref[...] = z.astype(BF16)

        @pl.when(j >= D_POOL // tn)
        def _():
            for c in range(tn // HEAD_DIM):
                sl = slice(c * HEAD_DIM, (c + 1) * HEAD_DIM)
                nat_ref[:, sl] = _rms(z[:, sl], kg_ref[...]).astype(BF16)

    @pl.when(j >= n_nat)
    def _():
        zT = lax.dot_general(wt_ref[...], h_ref[...], _NT_DIMS, preferred_element_type=F32)

        @pl.when(j < n_nat + D_ATTN // tn)
        def _():
            for c in range(tn // HEAD_DIM):
                sl = slice(c * HEAD_DIM, (c + 1) * HEAD_DIM)
                zc = zT[sl, :]
                inv = lax.rsqrt(jnp.mean(zc * zc, axis=0, keepdims=True) + EPS)
                tr_ref[sl, :] = (zc * inv * (qg_ref[...] * q_scale)).astype(BF16)

        @pl.when(j >= n_nat + D_ATTN // tn)
        def _():
            tr_ref[...] = zT.astype(BF16)


def _inproj(x2, g, w_nat, w_trT, w_kidx, w_widxT, kg, qg_col, *, tm, tn, q_scale):
    S = x2.shape[0]
    n_nat, n_tr = w_nat.shape[1] // tn, w_trT.shape[0] // tn
    return pl.pallas_call(
        functools.partial(_inproj_kernel, tn=tn, q_scale=q_scale),
        out_shape=(jax.ShapeDtypeStruct((S, w_nat.shape[1]), BF16),
                   jax.ShapeDtypeStruct((w_trT.shape[0], S), BF16),
                   jax.ShapeDtypeStruct((S, IDX_DIM), BF16),
                   jax.ShapeDtypeStruct((IDX_HEADS, S), F32)),
        grid=(S // tm, n_nat + n_tr),
        in_specs=[
            pl.BlockSpec((tm, D_MODEL), lambda i, j: (i, 0)),
            pl.BlockSpec((1, D_MODEL), lambda i, j: (0, 0)),
            pl.BlockSpec((D_MODEL, tn), lambda i, j: (0, jnp.minimum(j, n_nat - 1))),
            pl.BlockSpec((tn, D_MODEL), lambda i, j: (jnp.maximum(j - n_nat, 0), 0)),
            pl.BlockSpec((D_MODEL, IDX_DIM), lambda i, j: (0, 0)),
            pl.BlockSpec((IDX_HEADS, D_MODEL), lambda i, j: (0, 0)),
            pl.BlockSpec((1, HEAD_DIM), lambda i, j: (0, 0)),
            pl.BlockSpec((HEAD_DIM, 1), lambda i, j: (0, 0)),
        ],
        out_specs=(pl.BlockSpec((tm, tn), lambda i, j: (i, jnp.minimum(j, n_nat - 1))),
                   pl.BlockSpec((tn, tm), lambda i, j: (jnp.maximum(j - n_nat, 0), i)),
                   pl.BlockSpec((tm, IDX_DIM), lambda i, j: (i, 0)),
                   pl.BlockSpec((IDX_HEADS, tm), lambda i, j: (0, i))),
        scratch_shapes=[pltpu.VMEM((tm, D_MODEL), BF16)],
        compiler_params=_params(2),
        name="inproj",
    )(x2, g, w_nat, w_trT, w_kidx, w_widxT, kg, qg_col)


def _pool_kernel(u_ref, halo_ref, pw_ref, ps_ref, o_ref):
    i = pl.program_id(0)
    tm = u_ref.shape[0]
    u = u_ref[...].astype(F32)
    halo = jnp.where(i == 0, 0.0, halo_ref[...].astype(F32))
    ext = jnp.concatenate([halo, u], axis=0)
    t = i * tm + lax.broadcasted_iota(jnp.int32, (tm, 1), 0)
    for gi, w in enumerate(POOL_WINDOWS):
        sl = slice(gi * POOL_GROUP, (gi + 1) * POOL_GROUP)
        s = ext[:, sl]
        step = 1
        while step < w:
            s = s + pltpu.roll(s, step, axis=0)
            step *= 2
        cnt = jnp.minimum(t + 1, w).astype(F32)
        d = s[HALO:, :] / cnt - u[:, sl]
        y = jnp.dot(d.astype(BF16), pw_ref[gi], preferred_element_type=F32)
        o_ref[:, sl] = (y * ps_ref[:, sl]).astype(BF16)


def _pool(zm, pool_w, pool_scale, *, tm):
    S = zm.shape[0]
    return pl.pallas_call(
        _pool_kernel,
        out_shape=jax.ShapeDtypeStruct((S, D_POOL), BF16),
        grid=(S // tm,),
        in_specs=[
            pl.BlockSpec((tm, D_POOL), lambda i: (i, 0)),
            pl.BlockSpec((HALO, D_POOL), lambda i: (jnp.maximum(i * (tm // HALO) - 1, 0), 0)),
            pl.BlockSpec((len(POOL_WINDOWS), POOL_GROUP, POOL_GROUP), lambda i: (0, 0, 0)),
            pl.BlockSpec((1, D_POOL), lambda i: (0, 0)),
        ],
        out_specs=pl.BlockSpec((tm, D_POOL), lambda i: (i, 0)),
        compiler_params=_params(1),
        name="pool",
    )(zm, zm, pool_w, pool_scale)


I16 = jnp.int16
I16_MIN = -2 ** 15
POOL = 8


def _index_kernel(kidx_ref, qiT_ref, wT_ref, bias_ref, hi_ref, lo_ref, pool_ref, L_ref, *, T, CK, S, topk):
    qb = pl.program_id(0)
    nck = (qb * T + T + CK - 1) // CK
    q_pos = qb * T + lax.broadcasted_iota(jnp.int32, (1, T), 1)

    def score_chunk(off):
        kc = kidx_ref[pl.ds(off, CK), :]
        acc = None
        for h in range(IDX_HEADS):
            s = jnp.dot(kc, qiT_ref[h * IDX_DIM:(h + 1) * IDX_DIM, :], preferred_element_type=F32)
            t = wT_ref[h:h + 1, :] * jnp.maximum(s, 0.0)
            acc = t if acc is None else acc + t
        bits = lax.bitcast_convert_type(acc, jnp.int32)
        key = bits ^ ((bits >> 31) & jnp.int32(0x7FFFFFFF))
        key_pos = off + lax.broadcasted_iota(jnp.int32, (CK, 1), 0)
        key = jnp.where(key_pos <= q_pos, key, jnp.int32(-2 ** 31))
        hi_ref[pl.ds(off, CK), :] = (key >> 16).astype(I16)
        lo_ref[pl.ds(off, CK), :] = ((key & jnp.int32(0xFFFF)) + I16_MIN).astype(I16)

    def score_body(c, carry):
        for u in range(2):
            score_chunk(pl.multiple_of((2 * c + u) * CK, CK))
        return carry

    lax.fori_loop(0, (nck + 1) // 2, score_body, 0)

    none16 = jnp.full((), I16_MIN, I16)

    def count_ge(ref, cand, trips):
        c16 = cand.astype(I16)

        def count_body(c, acc):
            off = pl.multiple_of(c * CK, CK)
            ge = jnp.where(ref[pl.ds(off, CK), :] >= c16, jnp.ones((), I16), jnp.zeros((), I16))
            ge = ge.reshape(CK // 16, 16, T)
            parts = [ge[r] for r in range(CK // 16)]
            while len(parts) > 1:
                parts = [parts[i] + parts[i + 1] for i in range(0, len(parts), 2)]
            return acc + parts[0]

        acc = lax.fori_loop(0, trips, count_body, jnp.zeros((16, T), I16))
        return acc.astype(jnp.int32).sum(axis=0, keepdims=True)

    def kth_largest(ref, k, trips):
        def bit_body(i, carry):
            v, n_v = carry
            cand = v + (jnp.int32(1) << (15 - i))
            n_cand = count_ge(ref, cand, trips)
            ok = n_cand >= k
            return jnp.where(ok, cand, v), jnp.where(ok, n_cand, n_v)

        init = (jnp.full((1, T), I16_MIN, jnp.int32), jnp.zeros((1, T), jnp.int32) + trips * CK)
        return lax.fori_loop(0, 16, bit_body, init)

    H, n_ge = kth_largest(hi_ref, topk, nck)
    n_gt = jnp.where(H == -I16_MIN - 1, 0, count_ge(hi_ref, H + 1, nck))
    H16 = H.astype(I16)
    need = topk - n_gt

    def pool_body(c, carry):
        for g in range(CK // (16 * POOL)):
            a = b = jnp.full((16, T), I16_MIN, I16)
            for r in range(POOL):
                rows = pl.ds(pl.multiple_of(c * CK + (g * POOL + r) * 16, 16), 16)
                x = jnp.where(hi_ref[rows, :] == H16, lo_ref[rows, :], none16)
                up = x > a
                t = jnp.where(up, a, x)
                a = jnp.where(up, x, a)
                b = jnp.where(t > b, t, b)
            base = pl.multiple_of(c * POOL_ROWS + g * 32, 32)
            pool_ref[pl.ds(base, 16), :] = a
            pool_ref[pl.ds(base + 16, 16), :] = b
        return carry

    POOL_ROWS = 2 * CK // POOL
    per_trip = CK // POOL_ROWS
    pool_trips = (nck + per_trip - 1) // per_trip
    lax.fori_loop(0, nck, pool_body, 0)

    def pad_body(c, carry):
        pool_ref[pl.ds(pl.multiple_of(c * POOL_ROWS, POOL_ROWS), POOL_ROWS), :] = jnp.full((POOL_ROWS, T), I16_MIN, I16)
        return carry

    lax.fori_loop(nck, pool_trips * per_trip, pad_body, 0)

    n_tied = n_ge - n_gt
    n_kept = count_ge(pool_ref, jnp.full((1, T), I16_MIN + 1, jnp.int32), pool_trips)
    lost = jnp.sum(jnp.where((n_kept == n_tied) | (H == I16_MIN), 0, 1))

    @pl.when(lost == 0)
    def _():
        L_ref[...] = kth_largest(pool_ref, need, pool_trips)[0]

    @pl.when(lost != 0)
    def _():
        def mask_body(c, carry):
            off = pl.multiple_of(c * CK, CK)
            lo_ref[pl.ds(off, CK), :] = jnp.where(hi_ref[pl.ds(off, CK), :] == H16, lo_ref[pl.ds(off, CK), :],
                                                  none16)
            return carry

        lax.fori_loop(0, nck, mask_body, 0)
        L_ref[...] = kth_largest(lo_ref, need, nck)[0]

    L16 = L_ref[...].astype(I16)

    def emit_body(c, carry):
        off = pl.multiple_of(c * CK, CK)
        hi = hi_ref[pl.ds(off, CK), :]
        sel = (hi > H16) | ((hi == H16) & (lo_ref[pl.ds(off, CK), :] >= L16))
        sel = sel & (hi > jnp.full((), I16_MIN, I16))
        bias_ref[0, pl.ds(off, CK), :] = jnp.where(sel, jnp.zeros((), BF16), jnp.full((), MASK_NEG, BF16))
        return carry

    lax.fori_loop(0, nck, emit_body, 0)

    def fill_body(c, carry):
        off = pl.multiple_of(c * CK, CK)
        bias_ref[0, pl.ds(off, CK), :] = jnp.full((CK, T), MASK_NEG, BF16)
        return carry

    lax.fori_loop(nck, S // CK, fill_body, 0)


def _index(kidx, tr, wT, *, T, CK, topk):
    S = kidx.shape[0]
    assert S % (2 * CK) == 0 and CK % T == 0 and (S // CK) % (POOL // 2) == 0
    return pl.pallas_call(
        functools.partial(_index_kernel, T=T, CK=CK, S=S, topk=topk),
        out_shape=jax.ShapeDtypeStruct((S // T, S, T), BF16),
        grid=(S // T,),
        in_specs=[
            pl.BlockSpec((S, IDX_DIM), lambda q: (0, 0)),
            pl.BlockSpec((IDX_HEADS * IDX_DIM, T), lambda q: (2, q)),
            pl.BlockSpec((IDX_HEADS, T), lambda q: (0, q)),
        ],
        out_specs=pl.BlockSpec((1, S, T), lambda q: (q, 0, 0)),
        scratch_shapes=[pltpu.VMEM((S, T), I16), pltpu.VMEM((S, T), I16),
                        pltpu.VMEM((2 * S // POOL, T), I16), pltpu.VMEM((1, T), jnp.int32)],
        compiler_params=_params(1),
        name="index",
    )(kidx, tr, wT)


def _attn_kernel(qb_ref, kb_ref, kn_ref, qT_ref, vT_ref, bias_ref, o_ref, m_ref, l_ref, acc_ref, *, T, KB,
                 SUB, online):
    i = pl.program_id(0)
    qb = qb_ref[i]
    kb = kb_ref[i]

    @pl.when(kb == 0)
    def _():
        m_ref[...] = jnp.full(m_ref.shape, MASK_NEG, F32)
        l_ref[...] = jnp.zeros(l_ref.shape, F32)
        acc_ref[...] = jnp.zeros(acc_ref.shape, F32)

    items = [(kb0, h) for kb0 in range(0, KB, SUB) for h in range(N_HEADS)]

    def qk(item):
        kb0, h = item
        hs = slice(h * HEAD_DIM, (h + 1) * HEAD_DIM)
        return jnp.dot(kn_ref[kb0:kb0 + SUB, hs], qT_ref[hs, :], preferred_element_type=F32)

    ahead = 4
    pending = [qk(it) for it in items[:ahead]]
    bias = {kb0: bias_ref[0, kb0:kb0 + SUB, :].astype(F32) for kb0 in range(0, KB, SUB)}
    for n, (kb0, h) in enumerate(items):
        hs = slice(h * HEAD_DIM, (h + 1) * HEAD_DIM)
        s = bias[kb0] + pending.pop(0)
        if n + ahead < len(items):
            pending.append(qk(items[n + ahead]))
        if online:
            m_old = m_ref[h]
            m_new = jnp.maximum(m_old, s.max(axis=0, keepdims=True))
            alpha = jnp.exp2(m_old - m_new)
            p = jnp.exp2(s - m_new)
            l_ref[h] = alpha * l_ref[h] + p.sum(axis=0, keepdims=True)
            pv = jnp.dot(vT_ref[hs, kb0:kb0 + SUB], p.astype(BF16), preferred_element_type=F32)
            acc_ref[h] = alpha * acc_ref[h] + pv
            m_ref[h] = m_new
        else:
            p = jnp.exp2(s)
            l_ref[h] += p.sum(axis=0, keepdims=True)
            acc_ref[h] += jnp.dot(vT_ref[hs, kb0:kb0 + SUB], p.astype(BF16), preferred_element_type=F32)

    @pl.when(kb == (qb * T + T - 1) // KB)
    def _():
        for h in range(N_HEADS):
            o_ref[:, h * HEAD_DIM:(h + 1) * HEAD_DIM] = (acc_ref[h] / l_ref[h]).T.astype(BF16)


def _attn(nat, tr, bias, *, T, KB, SUB, online):
    S = nat.shape[0]
    pairs = [(q, k) for q in range(S // T) for k in range((q * T + T - 1) // KB + 1)]
    qb_ids = jnp.asarray(np.array([p[0] for p in pairs], np.int32))
    kb_ids = jnp.asarray(np.array([p[1] for p in pairs], np.int32))
    grid_spec = pltpu.PrefetchScalarGridSpec(
        num_scalar_prefetch=2,
        grid=(len(pairs),),
        in_specs=[
            pl.BlockSpec((KB, D_ATTN), lambda i, qb, kb: (kb[i], 1)),
            pl.BlockSpec((D_ATTN, T), lambda i, qb, kb: (0, qb[i])),
            pl.BlockSpec((D_ATTN, KB), lambda i, qb, kb: (1, kb[i])),
            pl.BlockSpec((1, KB, T), lambda i, qb, kb: (qb[i], kb[i], 0)),
        ],
        out_specs=pl.BlockSpec((T, D_ATTN), lambda i, qb, kb: (qb[i], 0)),
        scratch_shapes=[pltpu.VMEM((N_HEADS, 1, T), F32), pltpu.VMEM((N_HEADS, 1, T), F32),
                        pltpu.VMEM((N_HEADS, HEAD_DIM, T), F32)],
    )
    return pl.pallas_call(
        functools.partial(_attn_kernel, T=T, KB=KB, SUB=SUB, online=online),
        out_shape=jax.ShapeDtypeStruct((S, D_ATTN), BF16),
        grid_spec=grid_spec,
        compiler_params=_params(1),
        name="attn_online" if online else "attn",
    )(qb_ids, kb_ids, nat, tr, tr, bias)


def _outproj_kernel(yp_ref, ya_ref, wp_ref, wa_ref, x_ref, o_ref):
    acc = jnp.dot(yp_ref[...], wp_ref[...], preferred_element_type=F32)
    acc = acc + jnp.dot(ya_ref[...], wa_ref[...], preferred_element_type=F32)
    o_ref[...] = x_ref[...] + acc


def _outproj(yp, ya, w_out, x2, *, tm, tn):
    S = x2.shape[0]
    n_j = D_MODEL // tn
    return pl.pallas_call(
        _outproj_kernel,
        out_shape=jax.ShapeDtypeStruct((S, D_MODEL), F32),
        grid=(S // tm, n_j),
        in_specs=[
            pl.BlockSpec((tm, D_POOL), lambda i, j: (i, 0)),
            pl.BlockSpec((tm, D_ATTN), lambda i, j: (i, 0)),
            pl.BlockSpec((D_POOL, tn), lambda i, j: (0, j)),
            pl.BlockSpec((D_ATTN, tn), lambda i, j: (D_POOL // D_ATTN, j)),
            pl.BlockSpec((tm, tn), lambda i, j: (i, j)),
        ],
        out_specs=pl.BlockSpec((tm, tn), lambda i, j: (i, j)),
        compiler_params=_params(2),
        name="outproj",
    )(yp, ya, w_out, w_out, x2)


def _ffn_kernel(x_ref, halo_ref, g_ref, wg_ref, wv_ref, cwg_ref, cwv_ref, cbg_ref, cbv_ref, wd_ref,
                o_ref, h_ref, acc_ref):
    i = pl.program_id(0)
    f = pl.program_id(1)

    @pl.when(f == 0)
    def _():
        hh = jnp.where(i == 0, 0.0, _rms(halo_ref[...], g_ref[...]))
        h_ref[0:HALO, :] = hh.astype(BF16)
        h_ref[HALO:, :] = _rms(x_ref[...], g_ref[...]).astype(BF16)
        acc_ref[...] = jnp.zeros(acc_ref.shape, F32)

    h = h_ref[...]
    tf = wg_ref.shape[1]
    cols = [slice(a, a + tf // FFN_SUB) for a in range(0, tf, tf // FFN_SUB)]

    def conv(up, cw_ref, cb_ref, sl):
        c = (cw_ref[0:1, sl] * pltpu.roll(up, 2, axis=0) + cw_ref[1:2, sl] * pltpu.roll(up, 1, axis=0)
             + cw_ref[2:3, sl] * up)
        return cb_ref[:, sl] + c[HALO:, :]

    ups = [(jnp.dot(h, wg_ref[:, sl], preferred_element_type=F32),
            jnp.dot(h, wv_ref[:, sl], preferred_element_type=F32)) for sl in cols]
    down = None
    for sl, (up_g, up_v) in zip(cols, ups):
        cg = conv(up_g, cwg_ref, cbg_ref, sl)
        cv = conv(up_v, cwv_ref, cbv_ref, sl)
        act = cg * (1.0 / (1.0 + jnp.exp(-cg))) * cv
        d = jnp.dot(act.astype(BF16), wd_ref[sl, :], preferred_element_type=F32)
        down = d if down is None else down + d
    acc_ref[...] += down

    @pl.when(f == pl.num_programs(1) - 1)
    def _():
        o_ref[...] = x_ref[...] + acc_ref[...]


def _ffn(x1, g, w_up, conv_w, conv_b, w_down, *, tm, tf):
    S = x1.shape[0]
    n_f = D_FF // tf
    return pl.pallas_call(
        _ffn_kernel,
        out_shape=jax.ShapeDtypeStruct((S, D_MODEL), F32),
        grid=(S // tm, n_f),
        in_specs=[
            pl.BlockSpec((tm, D_MODEL), lambda i, f: (i, 0)),
            pl.BlockSpec((HALO, D_MODEL), lambda i, f: (jnp.maximum(i * (tm // HALO) - 1, 0), 0)),
            pl.BlockSpec((1, D_MODEL), lambda i, f: (0, 0)),
            pl.BlockSpec((D_MODEL, tf), lambda i, f: (0, f)),
            pl.BlockSpec((D_MODEL, tf), lambda i, f: (0, f + n_f)),
            pl.BlockSpec((CONV_WIDTH, tf), lambda i, f: (0, f)),
            pl.BlockSpec((CONV_WIDTH, tf), lambda i, f: (0, f + n_f)),
            pl.BlockSpec((1, tf), lambda i, f: (0, f)),
            pl.BlockSpec((1, tf), lambda i, f: (0, f + n_f)),
            pl.BlockSpec((tf, D_MODEL), lambda i, f: (f, 0)),
        ],
        out_specs=pl.BlockSpec((tm, D_MODEL), lambda i, f: (i, 0)),
        scratch_shapes=[pltpu.VMEM((HALO + tm, D_MODEL), BF16), pltpu.VMEM((tm, D_MODEL), F32)],
        compiler_params=_params(2),
        name="ffn",
    )(x1, x1, g, w_up, w_up, conv_w, conv_w, conv_b, conv_b, w_down)


def kernel(x, attn_norm_g, w_in, pool_w, pool_scale, q_norm_g, k_norm_g, w_out, ffn_norm_g, w_up,
           conv_w, conv_b, w_down):
    B, S, D = x.shape
    assert B == 1 and D == D_MODEL and w_in.shape == (D_MODEL, D_MAIN + D_TAIL)
    T = 256
    assert S % 1024 == 0
    topk = min(TOPK_MAX, S // 4)
    q_scale = HEAD_DIM ** -0.5 * LOG2E
    x2 = x.reshape(S, D)

    o = D_POOL
    w_q, w_k, w_v = (w_in[:, o + n * D_ATTN:o + (n + 1) * D_ATTN] for n in range(3))
    w_qi = w_in[:, o + 3 * D_ATTN:D_MAIN]
    w_nat = jnp.concatenate([w_in[:, :o], w_k], axis=1).astype(BF16)
    w_trT = jnp.concatenate([w_q, w_v, w_qi], axis=1).T.astype(BF16)
    w_kidx = w_in[:, D_MAIN:D_MAIN + IDX_DIM].astype(BF16)
    w_widxT = w_in[:, D_MAIN + IDX_DIM:].T.astype(BF16)
    nat, tr, kidx, wT = _inproj(x2, attn_norm_g.reshape(1, D), w_nat, w_trT, w_kidx, w_widxT,
                                k_norm_g.reshape(1, HEAD_DIM), q_norm_g.reshape(HEAD_DIM, 1),
                                tm=1024, tn=1024, q_scale=q_scale)

    y_pool = _pool(nat, pool_w.astype(BF16), pool_scale.reshape(1, D_POOL), tm=1024)
    bias = _index(kidx, tr, wT, T=T, CK=512, topk=topk)
    score_bound = HEAD_DIM * jnp.max(jnp.abs(q_norm_g)) * jnp.max(jnp.abs(k_norm_g)) * q_scale
    attn = functools.partial(_attn, T=T, KB=1024, SUB=512)
    y_attn = lax.cond(score_bound < MAX_UNSHIFTED_LOG2,
                      functools.partial(attn, online=False), functools.partial(attn, online=True),
                      nat, tr, bias)

    x1 = _outproj(y_pool, y_attn, w_out.astype(BF16), x2, tm=1024, tn=1024)
    out = _ffn(x1, ffn_norm_g.reshape(1, D), w_up.astype(BF16), conv_w, conv_b.reshape(1, 2 * D_FF),
               w_down.astype(BF16), tm=512, tf=512)
    return out.reshape(B, S, D)
```

```python
import functools

import numpy as np
import jax
import jax.numpy as jnp
from jax import lax
from jax.experimental import pallas as pl
from jax.experimental.pallas import tpu as pltpu

D_MODEL = 2048
D_POOL = 1024
POOL_WINDOWS = (2, 4, 8, 16)
POOL_GROUP = D_POOL // len(POOL_WINDOWS)
D_ATTN = 1024
HEAD_DIM = 128
N_HEADS = D_ATTN // HEAD_DIM
IDX_HEADS = 16
IDX_DIM = 64
TOPK_MAX = 256
D_FF = 5632
CONV_WIDTH = 3
EPS = 1e-6

D_MAIN = D_POOL + 3 * D_ATTN + IDX_HEADS * IDX_DIM
D_TAIL = IDX_DIM + IDX_HEADS
LANES = 128
HALO = 16
MASK_NEG = -1e30
LOG2E = 1.4426950408889634
FFN_SUB = 2
MAX_UNSHIFTED_LOG2 = 60.0
VMEM_LIMIT_BYTES = 56 * 1024 * 1024

F32 = jnp.float32
BF16 = jnp.bfloat16


def _params(n_axes):
    return pltpu.CompilerParams(dimension_semantics=("arbitrary",) * n_axes,
                                vmem_limit_bytes=VMEM_LIMIT_BYTES)


def _rms(xf, g):
    return xf * lax.rsqrt(jnp.mean(xf * xf, axis=-1, keepdims=True) + EPS) * g


_NT_DIMS = (((1,), (1,)), ((), ()))


def _inproj_kernel(x_ref, g_ref, wn_ref, wt_ref, wk_ref, ww_ref, kg_ref, qg_ref, nat_ref, tr_ref, kidx_ref,
                   wT_ref, h_ref, *, tn, q_scale):
    j = pl.program_id(1)
    n_nat = (D_POOL + D_ATTN) // tn

    @pl.when(j == 0)
    def _():
        h = _rms(x_ref[...], g_ref[...]).astype(BF16)
        h_ref[...] = h
        kidx_ref[...] = jnp.dot(h, wk_ref[...], preferred_element_type=F32).astype(BF16)
        w = lax.dot_general(ww_ref[...], h, _NT_DIMS, preferred_element_type=F32)
        wT_ref[...] = w * (IDX_HEADS ** -0.5) * (IDX_DIM ** -0.5)

    @pl.when(j < n_nat)
    def _():
        z = jnp.dot(h_ref[...], wn_ref[...], preferred_element_type=F32)

        @pl.when(j < D_POOL // tn)
        def _():
            nat_ref[...] = z.astype(BF16)

        @pl.when(j >= D_POOL // tn)
        def _():
            for c in range(tn // HEAD_DIM):
                sl = slice(c * HEAD_DIM, (c + 1) * HEAD_DIM)
                nat_ref[:, sl] = _rms(z[:, sl], kg_ref[...]).astype(BF16)

    @pl.when(j >= n_nat)
    def _():
        zT = lax.dot_general(wt_ref[...], h_ref[...], _NT_DIMS, preferred_element_type=F32)

        @pl.when(j < n_nat + D_ATTN // tn)
        def _():
            for c in range(tn // HEAD_DIM):
                sl = slice(c * HEAD_DIM, (c + 1) * HEAD_DIM)
                zc = zT[sl, :]
                inv = lax.rsqrt(jnp.mean(zc * zc, axis=0, keepdims=True) + EPS)
                tr_ref[sl, :] = (zc * inv * (qg_ref[...] * q_scale)).astype(BF16)

        @pl.when(j >= n_nat + D_ATTN // tn)
        def _():
            tr_ref[...] = zT.astype(BF16)


def _inproj(x2, g, w_nat, w_trT, w_kidx, w_widxT, kg, qg_col, *, tm, tn, q_scale):
    S = x2.shape[0]
    n_nat, n_tr = w_nat.shape[1] // tn, w_trT.shape[0] // tn
    return pl.pallas_call(
        functools.partial(_inproj_kernel, tn=tn, q_scale=q_scale),
        out_shape=(jax.ShapeDtypeStruct((S, w_nat.shape[1]), BF16),
                   jax.ShapeDtypeStruct((w_trT.shape[0], S), BF16),
                   jax.ShapeDtypeStruct((S, IDX_DIM), BF16),
                   jax.ShapeDtypeStruct((IDX_HEADS, S), F32)),
        grid=(S // tm, n_nat + n_tr),
        in_specs=[
            pl.BlockSpec((tm, D_MODEL), lambda i, j: (i, 0)),
            pl.BlockSpec((1, D_MODEL), lambda i, j: (0, 0)),
            pl.BlockSpec((D_MODEL, tn), lambda i, j: (0, jnp.minimum(j, n_nat - 1))),
            pl.BlockSpec((tn, D_MODEL), lambda i, j: (jnp.maximum(j - n_nat, 0), 0)),
            pl.BlockSpec((D_MODEL, IDX_DIM), lambda i, j: (0, 0)),
            pl.BlockSpec((IDX_HEADS, D_MODEL), lambda i, j: (0, 0)),
            pl.BlockSpec((1, HEAD_DIM), lambda i, j: (0, 0)),
            pl.BlockSpec((HEAD_DIM, 1), lambda i, j: (0, 0)),
        ],
        out_specs=(pl.BlockSpec((tm, tn), lambda i, j: (i, jnp.minimum(j, n_nat - 1))),
                   pl.BlockSpec((tn, tm), lambda i, j: (jnp.maximum(j - n_nat, 0), i)),
                   pl.BlockSpec((tm, IDX_DIM), lambda i, j: (i, 0)),
                   pl.BlockSpec((IDX_HEADS, tm), lambda i, j: (0, i))),
        scratch_shapes=[pltpu.VMEM((tm, D_MODEL), BF16)],
        compiler_params=_params(2),
        name="inproj",
    )(x2, g, w_nat, w_trT, w_kidx, w_widxT, kg, qg_col)


def _pool_kernel(u_ref, halo_ref, pw_ref, ps_ref, o_ref):
    i = pl.program_id(0)
    tm = u_ref.shape[0]
    u = u_ref[...].astype(F32)
    halo = jnp.where(i == 0, 0.0, halo_ref[...].astype(F32))
    ext = jnp.concatenate([halo, u], axis=0)
    t = i * tm + lax.broadcasted_iota(jnp.int32, (tm, 1), 0)
    for gi, w in enumerate(POOL_WINDOWS):
        sl = slice(gi * POOL_GROUP, (gi + 1) * POOL_GROUP)
        s = ext[:, sl]
        step = 1
        while step < w:
            s = s + pltpu.roll(s, step, axis=0)
            step *= 2
        cnt = jnp.minimum(t + 1, w).astype(F32)
        d = s[HALO:, :] / cnt - u[:, sl]
        y = jnp.dot(d.astype(BF16), pw_ref[gi], preferred_element_type=F32)
        o_ref[:, sl] = (y * ps_ref[:, sl]).astype(BF16)


def _pool(zm, pool_w, pool_scale, *, tm):
    S = zm.shape[0]
    return pl.pallas_call(
        _pool_kernel,
        out_shape=jax.ShapeDtypeStruct((S, D_POOL), BF16),
        grid=(S // tm,),
        in_specs=[
            pl.BlockSpec((tm, D_POOL), lambda i: (i, 0)),
            pl.BlockSpec((HALO, D_POOL), lambda i: (jnp.maximum(i * (tm // HALO) - 1, 0), 0)),
            pl.BlockSpec((len(POOL_WINDOWS), POOL_GROUP, POOL_GROUP), lambda i: (0, 0, 0)),
            pl.BlockSpec((1, D_POOL), lambda i: (0, 0)),
        ],
        out_specs=pl.BlockSpec((tm, D_POOL), lambda i: (i, 0)),
        compiler_params=_params(1),
        name="pool",
    )(zm, zm, pool_w, pool_scale)


I16 = jnp.int16
I16_MIN = -2 ** 15
POOL = 8


def _index_kernel(kidx_ref, qiT_ref, wT_ref, bias_ref, hi_ref, lo_ref, pool_ref, L_ref, n_eq_ref, take_eq_ref,
                  *, T, CK, S, topk):
    qb = pl.program_id(0)
    nck = (qb * T + T + CK - 1) // CK
    q_pos = qb * T + lax.broadcasted_iota(jnp.int32, (1, T), 1)

    def score_chunk(off):
        kc = kidx_ref[pl.ds(off, CK), :]
        acc = None
        for h in range(IDX_HEADS):
            s = jnp.dot(kc, qiT_ref[h * IDX_DIM:(h + 1) * IDX_DIM, :], preferred_element_type=F32)
            t = wT_ref[h:h + 1, :] * jnp.maximum(s, 0.0)
            acc = t if acc is None else acc + t
        bits = lax.bitcast_convert_type(acc, jnp.int32)
        key = bits ^ ((bits >> 31) & jnp.int32(0x7FFFFFFF))
        key_pos = off + lax.broadcasted_iota(jnp.int32, (CK, 1), 0)
        key = jnp.where(key_pos <= q_pos, key, jnp.int32(-2 ** 31))
        hi_ref[pl.ds(off, CK), :] = (key >> 16).astype(I16)
        lo_ref[pl.ds(off, CK), :] = ((key & jnp.int32(0xFFFF)) + I16_MIN).astype(I16)

    def score_body(c, carry):
        for u in range(2):
            score_chunk(pl.multiple_of((2 * c + u) * CK, CK))
        return carry

    lax.fori_loop(0, (nck + 1) // 2, score_body, 0)

    none16 = jnp.full((), I16_MIN, I16)

    def count_ge(ref, cand, trips):
        c16 = cand.astype(I16)

        def count_body(c, acc):
            off = pl.multiple_of(c * CK, CK)
            ge = jnp.where(ref[pl.ds(off, CK), :] >= c16, jnp.ones((), I16), jnp.zeros((), I16))
            ge = ge.reshape(CK // 16, 16, T)
            parts = [ge[r] for r in range(CK // 16)]
            while len(parts) > 1:
                parts = [parts[i] + parts[i + 1] for i in range(0, len(parts), 2)]
            return acc + parts[0]

        acc = lax.fori_loop(0, trips, count_body, jnp.zeros((16, T), I16))
        return acc.astype(jnp.int32).sum(axis=0, keepdims=True)

    def kth_largest(ref, k, trips):
        def bit_body(i, carry):
            v, n_v = carry
            cand = v + (jnp.int32(1) << (15 - i))
            n_cand = count_ge(ref, cand, trips)
            ok = n_cand >= k
            return jnp.where(ok, cand, v), jnp.where(ok, n_cand, n_v)

        init = (jnp.full((1, T), I16_MIN, jnp.int32), jnp.zeros((1, T), jnp.int32) + trips * CK)
        return lax.fori_loop(0, 16, bit_body, init)

    H, n_ge = kth_largest(hi_ref, topk, nck)
    n_gt = jnp.where(H == -I16_MIN - 1, 0, count_ge(hi_ref, H + 1, nck))
    H16 = H.astype(I16)
    need = topk - n_gt

    def pool_body(c, carry):
        for g in range(CK // (16 * POOL)):
            a = b = jnp.full((16, T), I16_MIN, I16)
            for r in range(POOL):
                rows = pl.ds(pl.multiple_of(c * CK + (g * POOL + r) * 16, 16), 16)
                x = jnp.where(hi_ref[rows, :] == H16, lo_ref[rows, :], none16)
                up = x > a
                t = jnp.where(up, a, x)
                a = jnp.where(up, x, a)
                b = jnp.where(t > b, t, b)
            base = pl.multiple_of(c * POOL_ROWS + g * 32, 32)
            pool_ref[pl.ds(base, 16), :] = a
            pool_ref[pl.ds(base + 16, 16), :] = b
        return carry

    POOL_ROWS = 2 * CK // POOL
    per_trip = CK // POOL_ROWS
    pool_trips = (nck + per_trip - 1) // per_trip
    lax.fori_loop(0, nck, pool_body, 0)

    def pad_body(c, carry):
        pool_ref[pl.ds(pl.multiple_of(c * POOL_ROWS, POOL_ROWS), POOL_ROWS), :] = jnp.full((POOL_ROWS, T), I16_MIN, I16)
        return carry

    lax.fori_loop(nck, pool_trips * per_trip, pad_body, 0)

    n_tied = n_ge - n_gt
    n_kept = count_ge(pool_ref, jnp.full((1, T), I16_MIN + 1, jnp.int32), pool_trips)
    lost = jnp.sum(jnp.where((n_kept == n_tied) | (H == I16_MIN), 0, 1))

    def second_level(ref, trips):
        L, n_ge_l = kth_largest(ref, need, trips)
        n_gt_l = jnp.where(L == -I16_MIN - 1, 0, count_ge(ref, L + 1, trips))
        L_ref[...] = L
        n_eq_ref[...] = n_ge_l - n_gt_l
        take_eq_ref[...] = need - n_gt_l

    @pl.when(lost == 0)
    def _():
        second_level(pool_ref, pool_trips)

    @pl.when(lost != 0)
    def _():
        def mask_body(c, carry):
            off = pl.multiple_of(c * CK, CK)
            lo_ref[pl.ds(off, CK), :] = jnp.where(hi_ref[pl.ds(off, CK), :] == H16, lo_ref[pl.ds(off, CK), :],
                                                  none16)
            return carry

        lax.fori_loop(0, nck, mask_body, 0)
        second_level(lo_ref, nck)

    L16 = L_ref[...].astype(I16)
    take_eq = take_eq_ref[...]
    surplus = jnp.sum(jnp.where((n_eq_ref[...] > take_eq) & (H > I16_MIN), 1, 0))

    def emit(off, sel):
        sel = sel & (hi_ref[pl.ds(off, CK), :] > none16)
        bias_ref[0, pl.ds(off, CK), :] = jnp.where(sel, jnp.zeros((), BF16), jnp.full((), MASK_NEG, BF16))

    @pl.when(surplus == 0)
    def _():
        def emit_body(c, carry):
            off = pl.multiple_of(c * CK, CK)
            hi = hi_ref[pl.ds(off, CK), :]
            emit(off, (hi > H16) | ((hi == H16) & (lo_ref[pl.ds(off, CK), :] >= L16)))
            return carry

        lax.fori_loop(0, nck, emit_body, 0)

    @pl.when(surplus != 0)
    def _():
        tri = (lax.broadcasted_iota(jnp.int32, (CK, CK), 0) >= lax.broadcasted_iota(jnp.int32, (CK, CK), 1))
        tri = jnp.where(tri, 1.0, 0.0).astype(BF16)
        take_f = take_eq.astype(F32)

        def emit_body(c, seen):
            off = pl.multiple_of(c * CK, CK)
            hi = hi_ref[pl.ds(off, CK), :]
            lo = lo_ref[pl.ds(off, CK), :]
            eq = (hi == H16) & (lo == L16) & (hi > none16)
            rank = seen + jnp.dot(tri, jnp.where(eq, jnp.ones((), BF16), jnp.zeros((), BF16)),
                                  preferred_element_type=F32)
            first = jnp.where(rank <= take_f, 1.0, 0.0).astype(BF16) > jnp.zeros((), BF16)
            emit(off, (hi > H16) | ((hi == H16) & (lo > L16)) | (eq & first))
            return rank[CK - 1:CK, :]

        lax.fori_loop(0, nck, emit_body, jnp.zeros((1, T), F32))

    def fill_body(c, carry):
        off = pl.multiple_of(c * CK, CK)
        bias_ref[0, pl.ds(off, CK), :] = jnp.full((CK, T), MASK_NEG, BF16)
        return carry

    lax.fori_loop(nck, S // CK, fill_body, 0)


def _index(kidx, tr, wT, *, T, CK, topk):
    S = kidx.shape[0]
    assert S % (2 * CK) == 0 and CK % T == 0 and (S // CK) % (POOL // 2) == 0
    return pl.pallas_call(
        functools.partial(_index_kernel, T=T, CK=CK, S=S, topk=topk),
        out_shape=jax.ShapeDtypeStruct((S // T, S, T), BF16),
        grid=(S // T,),
        in_specs=[
            pl.BlockSpec((S, IDX_DIM), lambda q: (0, 0)),
            pl.BlockSpec((IDX_HEADS * IDX_DIM, T), lambda q: (2, q)),
            pl.BlockSpec((IDX_HEADS, T), lambda q: (0, q)),
        ],
        out_specs=pl.BlockSpec((1, S, T), lambda q: (q, 0, 0)),
        scratch_shapes=[pltpu.VMEM((S, T), I16), pltpu.VMEM((S, T), I16),
                        pltpu.VMEM((2 * S // POOL, T), I16)] + [pltpu.VMEM((1, T), jnp.int32)] * 3,
        compiler_params=_params(1),
        name="index",
    )(kidx, tr, wT)


def _attn_kernel(qb_ref, kb_ref, kn_ref, qT_ref, vT_ref, bias_ref, o_ref, m_ref, l_ref, acc_ref, *, T, KB,
                 SUB, online):
    i = pl.program_id(0)
    qb = qb_ref[i]
    kb = kb_ref[i]

    @pl.when(kb == 0)
    def _():
        m_ref[...] = jnp.full(m_ref.shape, MASK_NEG, F32)
        l_ref[...] = jnp.zeros(l_ref.shape, F32)
        acc_ref[...] = jnp.zeros(acc_ref.shape, F32)

    items = [(kb0, h) for kb0 in range(0, KB, SUB) for h in range(N_HEADS)]

    def qk(item):
        kb0, h = item
        hs = slice(h * HEAD_DIM, (h + 1) * HEAD_DIM)
        return jnp.dot(kn_ref[kb0:kb0 + SUB, hs], qT_ref[hs, :], preferred_element_type=F32)

    ahead = 4
    pending = [qk(it) for it in items[:ahead]]
    bias = {kb0: bias_ref[0, kb0:kb0 + SUB, :].astype(F32) for kb0 in range(0, KB, SUB)}
    for n, (kb0, h) in enumerate(items):
        hs = slice(h * HEAD_DIM, (h + 1) * HEAD_DIM)
        s = bias[kb0] + pending.pop(0)
        if n + ahead < len(items):
            pending.append(qk(items[n + ahead]))
        if online:
            m_old = m_ref[h]
            m_new = jnp.maximum(m_old, s.max(axis=0, keepdims=True))
            alpha = jnp.exp2(m_old - m_new)
            p = jnp.exp2(s - m_new)
            l_ref[h] = alpha * l_ref[h] + p.sum(axis=0, keepdims=True)
            pv = jnp.dot(vT_ref[hs, kb0:kb0 + SUB], p.astype(BF16), preferred_element_type=F32)
            acc_ref[h] = alpha * acc_ref[h] + pv
            m_ref[h] = m_new
        else:
            p = jnp.exp2(s)
            l_ref[h] += p.sum(axis=0, keepdims=True)
            acc_ref[h] += jnp.dot(vT_ref[hs, kb0:kb0 + SUB], p.astype(BF16), preferred_element_type=F32)

    @pl.when(kb == (qb * T + T - 1) // KB)
    def _():
        for h in range(N_HEADS):
            o_ref[:, h * HEAD_DIM:(h + 1) * HEAD_DIM] = (acc_ref[h] / l_ref[h]).T.astype(BF16)


def _attn(nat, tr, bias, *, T, KB, SUB, online):
    S = nat.shape[0]
    pairs = [(q, k) for q in range(S // T) for k in range((q * T + T - 1) // KB + 1)]
    qb_ids = jnp.asarray(np.array([p[0] for p in pairs], np.int32))
    kb_ids = jnp.asarray(np.array([p[1] for p in pairs], np.int32))
    grid_spec = pltpu.PrefetchScalarGridSpec(
        num_scalar_prefetch=2,
        grid=(len(pairs),),
        in_specs=[
            pl.BlockSpec((KB, D_ATTN), lambda i, qb, kb: (kb[i], 1)),
            pl.BlockSpec((D_ATTN, T), lambda i, qb, kb: (0, qb[i])),
            pl.BlockSpec((D_ATTN, KB), lambda i, qb, kb: (1, kb[i])),
            pl.BlockSpec((1, KB, T), lambda i, qb, kb: (qb[i], kb[i], 0)),
        ],
        out_specs=pl.BlockSpec((T, D_ATTN), lambda i, qb, kb: (qb[i], 0)),
        scratch_shapes=[pltpu.VMEM((N_HEADS, 1, T), F32), pltpu.VMEM((N_HEADS, 1, T), F32),
                        pltpu.VMEM((N_HEADS, HEAD_DIM, T), F32)],
    )
    return pl.pallas_call(
        functools.partial(_attn_kernel, T=T, KB=KB, SUB=SUB, online=online),
        out_shape=jax.ShapeDtypeStruct((S, D_ATTN), BF16),
        grid_spec=grid_spec,
        compiler_params=_params(1),
        name="attn_online" if online else "attn",
    )(qb_ids, kb_ids, nat, tr, tr, bias)


def _outproj_kernel(yp_ref, ya_ref, wp_ref, wa_ref, x_ref, o_ref):
    acc = jnp.dot(yp_ref[...], wp_ref[...], preferred_element_type=F32)
    acc = acc + jnp.dot(ya_ref[...], wa_ref[...], preferred_element_type=F32)
    o_ref[...] = x_ref[...] + acc


def _outproj(yp, ya, w_out, x2, *, tm, tn):
    S = x2.shape[0]
    n_j = D_MODEL // tn
    return pl.pallas_call(
        _outproj_kernel,
        out_shape=jax.ShapeDtypeStruct((S, D_MODEL), F32),
        grid=(S // tm, n_j),
        in_specs=[
            pl.BlockSpec((tm, D_POOL), lambda i, j: (i, 0)),
            pl.BlockSpec((tm, D_ATTN), lambda i, j: (i, 0)),
            pl.BlockSpec((D_POOL, tn), lambda i, j: (0, j)),
            pl.BlockSpec((D_ATTN, tn), lambda i, j: (D_POOL // D_ATTN, j)),
            pl.BlockSpec((tm, tn), lambda i, j: (i, j)),
        ],
        out_specs=pl.BlockSpec((tm, tn), lambda i, j: (i, j)),
        compiler_params=_params(2),
        name="outproj",
    )(yp, ya, w_out, w_out, x2)


def _ffn_kernel(x_ref, halo_ref, g_ref, wg_ref, wv_ref, cwg_ref, cwv_ref, cbg_ref, cbv_ref, wd_ref,
                o_ref, h_ref, acc_ref):
    i = pl.program_id(0)
    f = pl.program_id(1)

    @pl.when(f == 0)
    def _():
        hh = jnp.where(i == 0, 0.0, _rms(halo_ref[...], g_ref[...]))
        h_ref[0:HALO, :] = hh.astype(BF16)
        h_ref[HALO:, :] = _rms(x_ref[...], g_ref[...]).astype(BF16)
        acc_ref[...] = jnp.zeros(acc_ref.shape, F32)

    tm = x_ref.shape[0]
    rows = tm // FFN_SUB

    def conv(up, cw_ref, cb_ref):
        c = (cw_ref[0:1, :] * pltpu.roll(up, 2, axis=0) + cw_ref[1:2, :] * pltpu.roll(up, 1, axis=0)
             + cw_ref[2:3, :] * up)
        return cb_ref[...] + c[HALO:, :]

    ups = []
    for r0 in range(0, tm, rows):
        h = h_ref[r0:r0 + HALO + rows, :]
        ups.append((jnp.dot(h, wg_ref[...], preferred_element_type=F32),
                    jnp.dot(h, wv_ref[...], preferred_element_type=F32)))
    for n, (up_g, up_v) in enumerate(ups):
        cg = conv(up_g, cwg_ref, cbg_ref)
        cv = conv(up_v, cwv_ref, cbv_ref)
        act = cg * (1.0 / (1.0 + jnp.exp(-cg))) * cv
        acc_ref[n * rows:(n + 1) * rows, :] += jnp.dot(act.astype(BF16), wd_ref[...], preferred_element_type=F32)

    @pl.when(f == pl.num_programs(1) - 1)
    def _():
        o_ref[...] = x_ref[...] + acc_ref[...]


def _ffn(x1, g, w_up, conv_w, conv_b, w_down, *, tm, tf):
    S = x1.shape[0]
    n_f = D_FF // tf
    return pl.pallas_call(
        _ffn_kernel,
        out_shape=jax.ShapeDtypeStruct((S, D_MODEL), F32),
        grid=(S // tm, n_f),
        in_specs=[
            pl.BlockSpec((tm, D_MODEL), lambda i, f: (i, 0)),
            pl.BlockSpec((HALO, D_MODEL), lambda i, f: (jnp.maximum(i * (tm // HALO) - 1, 0), 0)),
            pl.BlockSpec((1, D_MODEL), lambda i, f: (0, 0)),
            pl.BlockSpec((D_MODEL, tf), lambda i, f: (0, f)),
            pl.BlockSpec((D_MODEL, tf), lambda i, f: (0, f + n_f)),
            pl.BlockSpec((CONV_WIDTH, tf), lambda i, f: (0, f)),
            pl.BlockSpec((CONV_WIDTH, tf), lambda i, f: (0, f + n_f)),
            pl.BlockSpec((1, tf), lambda i, f: (0, f)),
            pl.BlockSpec((1, tf), lambda i, f: (0, f + n_f)),
            pl.BlockSpec((tf, D_MODEL), lambda i, f: (f, 0)),
        ],
        out_specs=pl.BlockSpec((tm, D_MODEL), lambda i, f: (i, 0)),
        scratch_shapes=[pltpu.VMEM((HALO + tm, D_MODEL), BF16), pltpu.VMEM((tm, D_MODEL), F32)],
        compiler_params=_params(2),
        name="ffn",
    )(x1, x1, g, w_up, w_up, conv_w, conv_w, conv_b, conv_b, w_down)


def kernel(x, attn_norm_g, w_in, pool_w, pool_scale, q_norm_g, k_norm_g, w_out, ffn_norm_g, w_up,
           conv_w, conv_b, w_down):
    B, S, D = x.shape
    assert B == 1 and D == D_MODEL and w_in.shape == (D_MODEL, D_MAIN + D_TAIL)
    T = 256
    assert S % 1024 == 0
    topk = min(TOPK_MAX, S // 4)
    q_scale = HEAD_DIM ** -0.5 * LOG2E
    x2 = x.reshape(S, D)

    o = D_POOL
    w_q, w_k, w_v = (w_in[:, o + n * D_ATTN:o + (n + 1) * D_ATTN] for n in range(3))
    w_qi = w_in[:, o + 3 * D_ATTN:D_MAIN]
    w_nat = jnp.concatenate([w_in[:, :o], w_k], axis=1).astype(BF16)
    w_trT = jnp.concatenate([w_q, w_v, w_qi], axis=1).T.astype(BF16)
    w_kidx = w_in[:, D_MAIN:D_MAIN + IDX_DIM].astype(BF16)
    w_widxT = w_in[:, D_MAIN + IDX_DIM:].T.astype(BF16)
    nat, tr, kidx, wT = _inproj(x2, attn_norm_g.reshape(1, D), w_nat, w_trT, w_kidx, w_widxT,
                                k_norm_g.reshape(1, HEAD_DIM), q_norm_g.reshape(HEAD_DIM, 1),
                                tm=1024, tn=1024, q_scale=q_scale)

    y_pool = _pool(nat, pool_w.astype(BF16), pool_scale.reshape(1, D_POOL), tm=1024)
    bias = _index(kidx, tr, wT, T=T, CK=512, topk=topk)
    score_bound = HEAD_DIM * jnp.max(jnp.abs(q_norm_g)) * jnp.max(jnp.abs(k_norm_g)) * q_scale
    attn = functools.partial(_attn, T=T, KB=1024, SUB=512)
    y_attn = lax.cond(score_bound < MAX_UNSHIFTED_LOG2,
                      functools.partial(attn, online=False), functools.partial(attn, online=True),
                      nat, tr, bias)

    x1 = _outproj(y_pool, y_attn, w_out.astype(BF16), x2, tm=1024, tn=1024)
    out = _ffn(x1, ffn_norm_g.reshape(1, D), w_up.astype(BF16), conv_w, conv_b.reshape(1, 2 * D_FF),
               w_down.astype(BF16), tm=512, tf=512)
    return out.reshape(B, S, D)
```

```python
import functools
from typing import NamedTuple

import numpy as np
import jax
import jax.numpy as jnp
from jax import lax
from jax.experimental import pallas as pl
from jax.experimental.pallas import tpu as pltpu

D_MODEL = 2048
D_POOL = 1024
POOL_WINDOWS = (2, 4, 8, 16)
POOL_GROUP = D_POOL // len(POOL_WINDOWS)
D_ATTN = 1024
HEAD_DIM = 128
N_HEADS = D_ATTN // HEAD_DIM
IDX_HEADS = 16
IDX_DIM = 64
TOPK_MAX = 256
D_FF = 5632
CONV_WIDTH = 3
EPS = 1e-6

D_MAIN = D_POOL + 3 * D_ATTN + IDX_HEADS * IDX_DIM
D_TAIL = IDX_DIM + IDX_HEADS
HALO = 16
MASK_NEG = -1e30
LOG2E = 1.4426950408889634
MAX_UNSHIFTED_LOG2 = 60.0
V7X_VMEM_BYTES = 64 * 1024 * 1024
VMEM_LIMIT_BYTES = V7X_VMEM_BYTES * 7 // 8


class _Tiles(NamedTuple):
    inproj_rows: int = 1024
    inproj_cols: int = 1024
    pool_rows: int = 1024
    dsa_queries: int = 256
    index_keys: int = 512
    attn_keys: int = 1024
    attn_sub_keys: int = 512
    outproj_rows: int = 1024
    outproj_cols: int = 1024
    ffn_rows: int = 512
    ffn_cols: int = 512


TILES = _Tiles()

F32 = jnp.float32
BF16 = jnp.bfloat16


def _params(n_axes):
    return pltpu.CompilerParams(dimension_semantics=("arbitrary",) * n_axes,
                                vmem_limit_bytes=VMEM_LIMIT_BYTES)


def _rms(xf, g):
    return xf * lax.rsqrt(jnp.mean(xf * xf, axis=-1, keepdims=True) + EPS) * g


_NT_DIMS = (((1,), (1,)), ((), ()))


def _inproj_kernel(x_ref, g_ref, wn_ref, wt_ref, wk_ref, ww_ref, kg_ref, qg_ref, nat_ref, tr_ref, kidx_ref,
                   wT_ref, h_ref, *, tn, q_scale):
    j = pl.program_id(1)
    n_nat = (D_POOL + D_ATTN) // tn

    @pl.when(j == 0)
    def _():
        h = _rms(x_ref[...], g_ref[...]).astype(BF16)
        h_ref[...] = h
        kidx_ref[...] = jnp.dot(h, wk_ref[...], preferred_element_type=F32).astype(BF16)
        w = lax.dot_general(ww_ref[...], h, _NT_DIMS, preferred_element_type=F32)
        wT_ref[...] = w * (IDX_HEADS ** -0.5) * (IDX_DIM ** -0.5)

    @pl.when(j < n_nat)
    def _():
        z = jnp.dot(h_ref[...], wn_ref[...], preferred_element_type=F32)

        @pl.when(j < D_POOL // tn)
        def _():
            nat_ref[...] = z.astype(BF16)

        @pl.when(j >= D_POOL // tn)
        def _():
            for c in range(tn // HEAD_DIM):
                sl = slice(c * HEAD_DIM, (c + 1) * HEAD_DIM)
                nat_ref[:, sl] = _rms(z[:, sl], kg_ref[...]).astype(BF16)

    @pl.when(j >= n_nat)
    def _():
        zT = lax.dot_general(wt_ref[...], h_ref[...], _NT_DIMS, preferred_element_type=F32)

        @pl.when(j < n_nat + D_ATTN // tn)
        def _():
            for c in range(tn // HEAD_DIM):
                sl = slice(c * HEAD_DIM, (c + 1) * HEAD_DIM)
                zc = zT[sl, :]
                inv = lax.rsqrt(jnp.mean(zc * zc, axis=0, keepdims=True) + EPS)
                tr_ref[sl, :] = (zc * inv * (qg_ref[...] * q_scale)).astype(BF16)

        @pl.when(j >= n_nat + D_ATTN // tn)
        def _():
            tr_ref[...] = zT.astype(BF16)


def _inproj(x2, g, w_nat, w_trT, w_kidx, w_widxT, kg, qg_col, *, tm, tn, q_scale):
    S = x2.shape[0]
    n_nat, n_tr = w_nat.shape[1] // tn, w_trT.shape[0] // tn
    return pl.pallas_call(
        functools.partial(_inproj_kernel, tn=tn, q_scale=q_scale),
        out_shape=(jax.ShapeDtypeStruct((S, w_nat.shape[1]), BF16),
                   jax.ShapeDtypeStruct((w_trT.shape[0], S), BF16),
                   jax.ShapeDtypeStruct((S, IDX_DIM), BF16),
                   jax.ShapeDtypeStruct((IDX_HEADS, S), F32)),
        grid=(S // tm, n_nat + n_tr),
        in_specs=[
            pl.BlockSpec((tm, D_MODEL), lambda i, j: (i, 0)),
            pl.BlockSpec((1, D_MODEL), lambda i, j: (0, 0)),
            pl.BlockSpec((D_MODEL, tn), lambda i, j: (0, jnp.minimum(j, n_nat - 1))),
            pl.BlockSpec((tn, D_MODEL), lambda i, j: (jnp.maximum(j - n_nat, 0), 0)),
            pl.BlockSpec((D_MODEL, IDX_DIM), lambda i, j: (0, 0)),
            pl.BlockSpec((IDX_HEADS, D_MODEL), lambda i, j: (0, 0)),
            pl.BlockSpec((1, HEAD_DIM), lambda i, j: (0, 0)),
            pl.BlockSpec((HEAD_DIM, 1), lambda i, j: (0, 0)),
        ],
        out_specs=(pl.BlockSpec((tm, tn), lambda i, j: (i, jnp.minimum(j, n_nat - 1))),
                   pl.BlockSpec((tn, tm), lambda i, j: (jnp.maximum(j - n_nat, 0), i)),
                   pl.BlockSpec((tm, IDX_DIM), lambda i, j: (i, 0)),
                   pl.BlockSpec((IDX_HEADS, tm), lambda i, j: (0, i))),
        scratch_shapes=[pltpu.VMEM((tm, D_MODEL), BF16)],
        compiler_params=_params(2),
        name="inproj",
    )(x2, g, w_nat, w_trT, w_kidx, w_widxT, kg, qg_col)


def _pool_kernel(u_ref, halo_ref, pw_ref, ps_ref, o_ref):
    i = pl.program_id(0)
    tm = u_ref.shape[0]
    u = u_ref[...].astype(F32)
    halo = jnp.where(i == 0, 0.0, halo_ref[...].astype(F32))
    ext = jnp.concatenate([halo, u], axis=0)
    t = i * tm + lax.broadcasted_iota(jnp.int32, (tm, 1), 0)
    for gi, w in enumerate(POOL_WINDOWS):
        sl = slice(gi * POOL_GROUP, (gi + 1) * POOL_GROUP)
        s = ext[:, sl]
        step = 1
        while step < w:
            s = s + pltpu.roll(s, step, axis=0)
            step *= 2
        cnt = jnp.minimum(t + 1, w).astype(F32)
        d = s[HALO:, :] / cnt - u[:, sl]
        y = jnp.dot(d.astype(BF16), pw_ref[gi], preferred_element_type=F32)
        o_ref[:, sl] = (y * ps_ref[:, sl]).astype(BF16)


def _pool(zm, pool_w, pool_scale, *, tm):
    S = zm.shape[0]
    return pl.pallas_call(
        _pool_kernel,
        out_shape=jax.ShapeDtypeStruct((S, D_POOL), BF16),
        grid=(S // tm,),
        in_specs=[
            pl.BlockSpec((tm, D_POOL), lambda i: (i, 0)),
            pl.BlockSpec((HALO, D_POOL), lambda i: (jnp.maximum(i * (tm // HALO) - 1, 0), 0)),
            pl.BlockSpec((len(POOL_WINDOWS), POOL_GROUP, POOL_GROUP), lambda i: (0, 0, 0)),
            pl.BlockSpec((1, D_POOL), lambda i: (0, 0)),
        ],
        out_specs=pl.BlockSpec((tm, D_POOL), lambda i: (i, 0)),
        compiler_params=_params(1),
        name="pool",
    )(zm, zm, pool_w, pool_scale)


I16 = jnp.int16
I16_MIN = -2 ** 15
POOL = 8


def _index_kernel(kidx_ref, qiT_ref, wT_ref, bias_ref, hi_ref, lo_ref, pool_ref, L_ref, n_eq_ref, take_eq_ref,
                  *, T, CK, S, topk):
    qb = pl.program_id(0)
    nck = (qb * T + T + CK - 1) // CK
    q_pos = qb * T + lax.broadcasted_iota(jnp.int32, (1, T), 1)

    def score_chunk(off):
        kc = kidx_ref[pl.ds(off, CK), :]
        acc = None
        for h in range(IDX_HEADS):
            s = jnp.dot(kc, qiT_ref[h * IDX_DIM:(h + 1) * IDX_DIM, :], preferred_element_type=F32)
            t = wT_ref[h:h + 1, :] * jnp.maximum(s, 0.0)
            acc = t if acc is None else acc + t
        bits = lax.bitcast_convert_type(acc, jnp.int32)
        key = bits ^ ((bits >> 31) & jnp.int32(0x7FFFFFFF))
        key_pos = off + lax.broadcasted_iota(jnp.int32, (CK, 1), 0)
        key = jnp.where(key_pos <= q_pos, key, jnp.int32(-2 ** 31))
        hi_ref[pl.ds(off, CK), :] = (key >> 16).astype(I16)
        lo_ref[pl.ds(off, CK), :] = ((key & jnp.int32(0xFFFF)) + I16_MIN).astype(I16)

    def score_body(c, carry):
        for u in range(2):
            score_chunk(pl.multiple_of((2 * c + u) * CK, CK))
        return carry

    lax.fori_loop(0, (nck + 1) // 2, score_body, 0)

    none16 = jnp.full((), I16_MIN, I16)

    def count_ge(ref, cand, trips):
        c16 = cand.astype(I16)

        def count_body(c, acc):
            off = pl.multiple_of(c * CK, CK)
            ge = jnp.where(ref[pl.ds(off, CK), :] >= c16, jnp.ones((), I16), jnp.zeros((), I16))
            ge = ge.reshape(CK // 16, 16, T)
            parts = [ge[r] for r in range(CK // 16)]
            while len(parts) > 1:
                parts = [parts[i] + parts[i + 1] for i in range(0, len(parts), 2)]
            return acc + parts[0]

        acc = lax.fori_loop(0, trips, count_body, jnp.zeros((16, T), I16))
        return acc.astype(jnp.int32).sum(axis=0, keepdims=True)

    def kth_largest(ref, k, trips):
        def bit_body(i, carry):
            v, n_v = carry
            cand = v + (jnp.int32(1) << (15 - i))
            n_cand = count_ge(ref, cand, trips)
            ok = n_cand >= k
            return jnp.where(ok, cand, v), jnp.where(ok, n_cand, n_v)

        init = (jnp.full((1, T), I16_MIN, jnp.int32), jnp.zeros((1, T), jnp.int32) + trips * CK)
        return lax.fori_loop(0, 16, bit_body, init)

    H, n_ge = kth_largest(hi_ref, topk, nck)
    n_gt = jnp.where(H == -I16_MIN - 1, 0, count_ge(hi_ref, H + 1, nck))
    H16 = H.astype(I16)
    need = topk - n_gt

    def pool_body(c, carry):
        for g in range(CK // (16 * POOL)):
            a = b = jnp.full((16, T), I16_MIN, I16)
            for r in range(POOL):
                rows = pl.ds(pl.multiple_of(c * CK + (g * POOL + r) * 16, 16), 16)
                x = jnp.where(hi_ref[rows, :] == H16, lo_ref[rows, :], none16)
                up = x > a
                t = jnp.where(up, a, x)
                a = jnp.where(up, x, a)
                b = jnp.where(t > b, t, b)
            base = pl.multiple_of(c * POOL_ROWS + g * 32, 32)
            pool_ref[pl.ds(base, 16), :] = a
            pool_ref[pl.ds(base + 16, 16), :] = b
        return carry

    POOL_ROWS = 2 * CK // POOL
    per_trip = CK // POOL_ROWS
    pool_trips = (nck + per_trip - 1) // per_trip
    lax.fori_loop(0, nck, pool_body, 0)

    def pad_body(c, carry):
        pool_ref[pl.ds(pl.multiple_of(c * POOL_ROWS, POOL_ROWS), POOL_ROWS), :] = jnp.full((POOL_ROWS, T), I16_MIN, I16)
        return carry

    lax.fori_loop(nck, pool_trips * per_trip, pad_body, 0)

    n_tied = n_ge - n_gt
    n_kept = count_ge(pool_ref, jnp.full((1, T), I16_MIN + 1, jnp.int32), pool_trips)
    lost = jnp.sum(jnp.where((n_kept == n_tied) | (H == I16_MIN), 0, 1))

    def second_level(ref, trips):
        L, n_ge_l = kth_largest(ref, need, trips)
        n_gt_l = jnp.where(L == -I16_MIN - 1, 0, count_ge(ref, L + 1, trips))
        L_ref[...] = L
        n_eq_ref[...] = n_ge_l - n_gt_l
        take_eq_ref[...] = need - n_gt_l

    @pl.when(lost == 0)
    def _():
        second_level(pool_ref, pool_trips)

    @pl.when(lost != 0)
    def _():
        def mask_body(c, carry):
            off = pl.multiple_of(c * CK, CK)
            lo_ref[pl.ds(off, CK), :] = jnp.where(hi_ref[pl.ds(off, CK), :] == H16, lo_ref[pl.ds(off, CK), :],
                                                  none16)
            return carry

        lax.fori_loop(0, nck, mask_body, 0)
        second_level(lo_ref, nck)

    L16 = L_ref[...].astype(I16)
    take_eq = take_eq_ref[...]
    surplus = jnp.sum(jnp.where((n_eq_ref[...] > take_eq) & (H > I16_MIN), 1, 0))

    def emit(off, sel):
        sel = sel & (hi_ref[pl.ds(off, CK), :] > none16)
        bias_ref[0, pl.ds(off, CK), :] = jnp.where(sel, jnp.zeros((), BF16), jnp.full((), MASK_NEG, BF16))

    @pl.when(surplus == 0)
    def _():
        def emit_body(c, carry):
            off = pl.multiple_of(c * CK, CK)
            hi = hi_ref[pl.ds(off, CK), :]
            emit(off, (hi > H16) | ((hi == H16) & (lo_ref[pl.ds(off, CK), :] >= L16)))
            return carry

        lax.fori_loop(0, nck, emit_body, 0)

    @pl.when(surplus != 0)
    def _():
        tri = (lax.broadcasted_iota(jnp.int32, (CK, CK), 0) >= lax.broadcasted_iota(jnp.int32, (CK, CK), 1))
        tri = jnp.where(tri, 1.0, 0.0).astype(BF16)
        take_f = take_eq.astype(F32)

        def emit_body(c, seen):
            off = pl.multiple_of(c * CK, CK)
            hi = hi_ref[pl.ds(off, CK), :]
            lo = lo_ref[pl.ds(off, CK), :]
            eq = (hi == H16) & (lo == L16) & (hi > none16)
            rank = seen + jnp.dot(tri, jnp.where(eq, jnp.ones((), BF16), jnp.zeros((), BF16)),
                                  preferred_element_type=F32)
            first = jnp.where(rank <= take_f, 1.0, 0.0).astype(BF16) > jnp.zeros((), BF16)
            emit(off, (hi > H16) | ((hi == H16) & (lo > L16)) | (eq & first))
            return rank[CK - 1:CK, :]

        lax.fori_loop(0, nck, emit_body, jnp.zeros((1, T), F32))

    def fill_body(c, carry):
        off = pl.multiple_of(c * CK, CK)
        bias_ref[0, pl.ds(off, CK), :] = jnp.full((CK, T), MASK_NEG, BF16)
        return carry

    lax.fori_loop(nck, S // CK, fill_body, 0)


def _index(kidx, tr, wT, *, T, CK, topk):
    S = kidx.shape[0]
    assert S % (2 * CK) == 0 and CK % T == 0 and (S // CK) % (POOL // 2) == 0
    return pl.pallas_call(
        functools.partial(_index_kernel, T=T, CK=CK, S=S, topk=topk),
        out_shape=jax.ShapeDtypeStruct((S // T, S, T), BF16),
        grid=(S // T,),
        in_specs=[
            pl.BlockSpec((S, IDX_DIM), lambda q: (0, 0)),
            pl.BlockSpec((IDX_HEADS * IDX_DIM, T), lambda q: (2, q)),
            pl.BlockSpec((IDX_HEADS, T), lambda q: (0, q)),
        ],
        out_specs=pl.BlockSpec((1, S, T), lambda q: (q, 0, 0)),
        scratch_shapes=[pltpu.VMEM((S, T), I16), pltpu.VMEM((S, T), I16),
                        pltpu.VMEM((2 * S // POOL, T), I16)] + [pltpu.VMEM((1, T), jnp.int32)] * 3,
        compiler_params=_params(1),
        name="index",
    )(kidx, tr, wT)


def _attn_kernel(qb_ref, kb_ref, kn_ref, qT_ref, vT_ref, bias_ref, o_ref, m_ref, l_ref, acc_ref, *, T, KB,
                 SUB, online):
    i = pl.program_id(0)
    qb = qb_ref[i]
    kb = kb_ref[i]

    @pl.when(kb == 0)
    def _():
        m_ref[...] = jnp.full(m_ref.shape, MASK_NEG, F32)
        l_ref[...] = jnp.zeros(l_ref.shape, F32)
        acc_ref[...] = jnp.zeros(acc_ref.shape, F32)

    items = [(kb0, h) for kb0 in range(0, KB, SUB) for h in range(N_HEADS)]

    def qk(item):
        kb0, h = item
        hs = slice(h * HEAD_DIM, (h + 1) * HEAD_DIM)
        return jnp.dot(kn_ref[kb0:kb0 + SUB, hs], qT_ref[hs, :], preferred_element_type=F32)

    ahead = 4
    pending = [qk(it) for it in items[:ahead]]
    bias = {kb0: bias_ref[0, kb0:kb0 + SUB, :].astype(F32) for kb0 in range(0, KB, SUB)}
    for n, (kb0, h) in enumerate(items):
        hs = slice(h * HEAD_DIM, (h + 1) * HEAD_DIM)
        s = bias[kb0] + pending.pop(0)
        if n + ahead < len(items):
            pending.append(qk(items[n + ahead]))
        if online:
            m_old = m_ref[h]
            m_new = jnp.maximum(m_old, s.max(axis=0, keepdims=True))
            alpha = jnp.exp2(m_old - m_new)
            p = jnp.exp2(s - m_new)
            l_ref[h] = alpha * l_ref[h] + p.sum(axis=0, keepdims=True)
            pv = jnp.dot(vT_ref[hs, kb0:kb0 + SUB], p.astype(BF16), preferred_element_type=F32)
            acc_ref[h] = alpha * acc_ref[h] + pv
            m_ref[h] = m_new
        else:
            p = jnp.exp2(s)
            l_ref[h] += p.sum(axis=0, keepdims=True)
            acc_ref[h] += jnp.dot(vT_ref[hs, kb0:kb0 + SUB], p.astype(BF16), preferred_element_type=F32)

    @pl.when(kb == (qb * T + T - 1) // KB)
    def _():
        for h in range(N_HEADS):
            o_ref[:, h * HEAD_DIM:(h + 1) * HEAD_DIM] = (acc_ref[h] / l_ref[h]).T.astype(BF16)


def _attn(nat, tr, bias, *, T, KB, SUB, online):
    S = nat.shape[0]
    pairs = [(q, k) for q in range(S // T) for k in range((q * T + T - 1) // KB + 1)]
    qb_ids = jnp.asarray(np.array([p[0] for p in pairs], np.int32))
    kb_ids = jnp.asarray(np.array([p[1] for p in pairs], np.int32))
    grid_spec = pltpu.PrefetchScalarGridSpec(
        num_scalar_prefetch=2,
        grid=(len(pairs),),
        in_specs=[
            pl.BlockSpec((KB, D_ATTN), lambda i, qb, kb: (kb[i], 1)),
            pl.BlockSpec((D_ATTN, T), lambda i, qb, kb: (0, qb[i])),
            pl.BlockSpec((D_ATTN, KB), lambda i, qb, kb: (1, kb[i])),
            pl.BlockSpec((1, KB, T), lambda i, qb, kb: (qb[i], kb[i], 0)),
        ],
        out_specs=pl.BlockSpec((T, D_ATTN), lambda i, qb, kb: (qb[i], 0)),
        scratch_shapes=[pltpu.VMEM((N_HEADS, 1, T), F32), pltpu.VMEM((N_HEADS, 1, T), F32),
                        pltpu.VMEM((N_HEADS, HEAD_DIM, T), F32)],
    )
    return pl.pallas_call(
        functools.partial(_attn_kernel, T=T, KB=KB, SUB=SUB, online=online),
        out_shape=jax.ShapeDtypeStruct((S, D_ATTN), BF16),
        grid_spec=grid_spec,
        compiler_params=_params(1),
        name="attn_online" if online else "attn",
    )(qb_ids, kb_ids, nat, tr, tr, bias)


def _outproj_kernel(yp_ref, ya_ref, wp_ref, wa_ref, x_ref, o_ref):
    acc = jnp.dot(yp_ref[...], wp_ref[...], preferred_element_type=F32)
    acc = acc + jnp.dot(ya_ref[...], wa_ref[...], preferred_element_type=F32)
    o_ref[...] = x_ref[...] + acc


def _outproj(yp, ya, w_out, x2, *, tm, tn):
    S = x2.shape[0]
    n_j = D_MODEL // tn
    return pl.pallas_call(
        _outproj_kernel,
        out_shape=jax.ShapeDtypeStruct((S, D_MODEL), F32),
        grid=(S // tm, n_j),
        in_specs=[
            pl.BlockSpec((tm, D_POOL), lambda i, j: (i, 0)),
            pl.BlockSpec((tm, D_ATTN), lambda i, j: (i, 0)),
            pl.BlockSpec((D_POOL, tn), lambda i, j: (0, j)),
            pl.BlockSpec((D_ATTN, tn), lambda i, j: (D_POOL // D_ATTN, j)),
            pl.BlockSpec((tm, tn), lambda i, j: (i, j)),
        ],
        out_specs=pl.BlockSpec((tm, tn), lambda i, j: (i, j)),
        compiler_params=_params(2),
        name="outproj",
    )(yp, ya, w_out, w_out, x2)


def _ffn_kernel(x_ref, halo_ref, g_ref, wg_ref, wv_ref, cwg_ref, cwv_ref, cbg_ref, cbv_ref, wd_ref,
                o_ref, h_ref, acc_ref):
    i = pl.program_id(0)
    f = pl.program_id(1)

    @pl.when(f == 0)
    def _():
        hh = jnp.where(i == 0, 0.0, _rms(halo_ref[...], g_ref[...]))
        h_ref[0:HALO, :] = hh.astype(BF16)
        h_ref[HALO:, :] = _rms(x_ref[...], g_ref[...]).astype(BF16)
        acc_ref[...] = jnp.zeros(acc_ref.shape, F32)

    h = h_ref[...]

    def conv(w_ref, cw_ref, cb_ref):
        up = jnp.dot(h, w_ref[...], preferred_element_type=F32)
        c = (cw_ref[0:1, :] * pltpu.roll(up, 2, axis=0) + cw_ref[1:2, :] * pltpu.roll(up, 1, axis=0)
             + cw_ref[2:3, :] * up)
        return cb_ref[...] + c[HALO:, :]

    cg = conv(wg_ref, cwg_ref, cbg_ref)
    cv = conv(wv_ref, cwv_ref, cbv_ref)
    act = cg * (1.0 / (1.0 + jnp.exp(-cg))) * cv
    acc_ref[...] += jnp.dot(act.astype(BF16), wd_ref[...], preferred_element_type=F32)

    @pl.when(f == pl.num_programs(1) - 1)
    def _():
        o_ref[...] = x_ref[...] + acc_ref[...]


def _ffn(x1, g, w_up, conv_w, conv_b, w_down, *, tm, tf):
    S = x1.shape[0]
    n_f = D_FF // tf
    return pl.pallas_call(
        _ffn_kernel,
        out_shape=jax.ShapeDtypeStruct((S, D_MODEL), F32),
        grid=(S // tm, n_f),
        in_specs=[
            pl.BlockSpec((tm, D_MODEL), lambda i, f: (i, 0)),
            pl.BlockSpec((HALO, D_MODEL), lambda i, f: (jnp.maximum(i * (tm // HALO) - 1, 0), 0)),
            pl.BlockSpec((1, D_MODEL), lambda i, f: (0, 0)),
            pl.BlockSpec((D_MODEL, tf), lambda i, f: (0, f)),
            pl.BlockSpec((D_MODEL, tf), lambda i, f: (0, f + n_f)),
            pl.BlockSpec((CONV_WIDTH, tf), lambda i, f: (0, f)),
            pl.BlockSpec((CONV_WIDTH, tf), lambda i, f: (0, f + n_f)),
            pl.BlockSpec((1, tf), lambda i, f: (0, f)),
            pl.BlockSpec((1, tf), lambda i, f: (0, f + n_f)),
            pl.BlockSpec((tf, D_MODEL), lambda i, f: (f, 0)),
        ],
        out_specs=pl.BlockSpec((tm, D_MODEL), lambda i, f: (i, 0)),
        scratch_shapes=[pltpu.VMEM((HALO + tm, D_MODEL), BF16), pltpu.VMEM((tm, D_MODEL), F32)],
        compiler_params=_params(2),
        name="ffn",
    )(x1, x1, g, w_up, w_up, conv_w, conv_w, conv_b, conv_b, w_down)


def kernel(x, attn_norm_g, w_in, pool_w, pool_scale, q_norm_g, k_norm_g, w_out, ffn_norm_g, w_up,
           conv_w, conv_b, w_down):
    B, S, D = x.shape
    assert B == 1 and D == D_MODEL and w_in.shape == (D_MODEL, D_MAIN + D_TAIL)
    t = TILES
    assert all(S % n == 0 for n in (t.inproj_rows, t.pool_rows, t.outproj_rows, t.ffn_rows, t.attn_keys))
    topk = min(TOPK_MAX, S // 4)
    q_scale = HEAD_DIM ** -0.5 * LOG2E
    x2 = x.reshape(S, D)

    o = D_POOL
    w_q, w_k, w_v = (w_in[:, o + n * D_ATTN:o + (n + 1) * D_ATTN] for n in range(3))
    w_qi = w_in[:, o + 3 * D_ATTN:D_MAIN]
    w_nat = jnp.concatenate([w_in[:, :o], w_k], axis=1).astype(BF16)
    w_trT = jnp.concatenate([w_q, w_v, w_qi], axis=1).T.astype(BF16)
    w_kidx = w_in[:, D_MAIN:D_MAIN + IDX_DIM].astype(BF16)
    w_widxT = w_in[:, D_MAIN + IDX_DIM:].T.astype(BF16)
    nat, tr, kidx, wT = _inproj(x2, attn_norm_g.reshape(1, D), w_nat, w_trT, w_kidx, w_widxT,
                                k_norm_g.reshape(1, HEAD_DIM), q_norm_g.reshape(HEAD_DIM, 1),
                                tm=t.inproj_rows, tn=t.inproj_cols, q_scale=q_scale)

    y_pool = _pool(nat, pool_w.astype(BF16), pool_scale.reshape(1, D_POOL), tm=t.pool_rows)
    bias = _index(kidx, tr, wT, T=t.dsa_queries, CK=t.index_keys, topk=topk)
    score_bound = HEAD_DIM * jnp.max(jnp.abs(q_norm_g)) * jnp.max(jnp.abs(k_norm_g)) * q_scale
    attn = functools.partial(_attn, T=t.dsa_queries, KB=t.attn_keys, SUB=t.attn_sub_keys)
    y_attn = lax.cond(score_bound < MAX_UNSHIFTED_LOG2,
                      functools.partial(attn, online=False), functools.partial(attn, online=True),
                      nat, tr, bias)

    x1 = _outproj(y_pool, y_attn, w_out.astype(BF16), x2, tm=t.outproj_rows, tn=t.outproj_cols)
    out = _ffn(x1, ffn_norm_g.reshape(1, D), w_up.astype(BF16), conv_w, conv_b.reshape(1, 2 * D_FF),
               w_down.astype(BF16), tm=t.ffn_rows, tf=t.ffn_cols)
    return out.reshape(B, S, D)
```

```python
import functools
from typing import NamedTuple

import numpy as np
import jax
import jax.numpy as jnp
from jax import lax
from jax.experimental import pallas as pl
from jax.experimental.pallas import tpu as pltpu

D_MODEL = 2048
D_POOL = 1024
POOL_WINDOWS = (2, 4, 8, 16)
POOL_GROUP = D_POOL // len(POOL_WINDOWS)
D_ATTN = 1024
HEAD_DIM = 128
N_HEADS = D_ATTN // HEAD_DIM
IDX_HEADS = 16
IDX_DIM = 64
TOPK_MAX = 256
D_FF = 5632
CONV_WIDTH = 3
EPS = 1e-6

D_MAIN = D_POOL + 3 * D_ATTN + IDX_HEADS * IDX_DIM
D_TAIL = IDX_DIM + IDX_HEADS
HALO = 16
MASK_NEG = -1e30
LOG2E = 1.4426950408889634
MAX_UNSHIFTED_LOG2 = 60.0
V7X_VMEM_BYTES = 64 * 1024 * 1024
VMEM_LIMIT_BYTES = V7X_VMEM_BYTES * 7 // 8


class _Tiles(NamedTuple):
    inproj_rows: int = 1024
    inproj_cols: int = 1024
    pool_rows: int = 1024
    dsa_queries: int = 256
    index_keys: int = 512
    attn_keys: int = 1024
    attn_sub_keys: int = 512
    outproj_rows: int = 1024
    outproj_cols: int = 1024
    ffn_rows: int = 512
    ffn_cols: int = 512


TILES = _Tiles()

F32 = jnp.float32
BF16 = jnp.bfloat16


def _params(n_axes):
    return pltpu.CompilerParams(dimension_semantics=("arbitrary",) * n_axes,
                                vmem_limit_bytes=VMEM_LIMIT_BYTES)


def _rms(xf, g):
    return xf * lax.rsqrt(jnp.mean(xf * xf, axis=-1, keepdims=True) + EPS) * g


_NT_DIMS = (((1,), (1,)), ((), ()))


def _inproj_kernel(x_ref, g_ref, wn_ref, wt_ref, wk_ref, ww_ref, kg_ref, qg_ref, nat_ref, tr_ref, kidx_ref,
                   wT_ref, h_ref, *, tn, q_scale):
    j = pl.program_id(1)
    n_nat = (D_POOL + D_ATTN) // tn

    @pl.when(j == 0)
    def _():
        h = _rms(x_ref[...], g_ref[...]).astype(BF16)
        h_ref[...] = h
        kidx_ref[...] = jnp.dot(h, wk_ref[...], preferred_element_type=F32).astype(BF16)
        w = lax.dot_general(ww_ref[...], h, _NT_DIMS, preferred_element_type=F32)
        wT_ref[...] = w * (IDX_HEADS ** -0.5) * (IDX_DIM ** -0.5)

    @pl.when(j < n_nat)
    def _():
        z = jnp.dot(h_ref[...], wn_ref[...], preferred_element_type=F32)

        @pl.when(j < D_POOL // tn)
        def _():
            nat_ref[...] = z.astype(BF16)

        @pl.when(j >= D_POOL // tn)
        def _():
            for c in range(tn // HEAD_DIM):
                sl = slice(c * HEAD_DIM, (c + 1) * HEAD_DIM)
                nat_ref[:, sl] = _rms(z[:, sl], kg_ref[...]).astype(BF16)

    @pl.when(j >= n_nat)
    def _():
        zT = lax.dot_general(wt_ref[...], h_ref[...], _NT_DIMS, preferred_element_type=F32)

        @pl.when(j < n_nat + D_ATTN // tn)
        def _():
            for c in range(tn // HEAD_DIM):
                sl = slice(c * HEAD_DIM, (c + 1) * HEAD_DIM)
                zc = zT[sl, :]
                inv = lax.rsqrt(jnp.mean(zc * zc, axis=0, keepdims=True) + EPS)
                tr_ref[sl, :] = (zc * inv * (qg_ref[...] * q_scale)).astype(BF16)

        @pl.when(j >= n_nat + D_ATTN // tn)
        def _():
            tr_ref[...] = zT.astype(BF16)


def _inproj(x2, g, w_nat, w_trT, w_kidx, w_widxT, kg, qg_col, *, tm, tn, q_scale):
    S = x2.shape[0]
    n_nat, n_tr = w_nat.shape[1] // tn, w_trT.shape[0] // tn
    return pl.pallas_call(
        functools.partial(_inproj_kernel, tn=tn, q_scale=q_scale),
        out_shape=(jax.ShapeDtypeStruct((S, w_nat.shape[1]), BF16),
                   jax.ShapeDtypeStruct((w_trT.shape[0], S), BF16),
                   jax.ShapeDtypeStruct((S, IDX_DIM), BF16),
                   jax.ShapeDtypeStruct((IDX_HEADS, S), F32)),
        grid=(S // tm, n_nat + n_tr),
        in_specs=[
            pl.BlockSpec((tm, D_MODEL), lambda i, j: (i, 0)),
            pl.BlockSpec((1, D_MODEL), lambda i, j: (0, 0)),
            pl.BlockSpec((D_MODEL, tn), lambda i, j: (0, jnp.minimum(j, n_nat - 1))),
            pl.BlockSpec((tn, D_MODEL), lambda i, j: (jnp.maximum(j - n_nat, 0), 0)),
            pl.BlockSpec((D_MODEL, IDX_DIM), lambda i, j: (0, 0)),
            pl.BlockSpec((IDX_HEADS, D_MODEL), lambda i, j: (0, 0)),
            pl.BlockSpec((1, HEAD_DIM), lambda i, j: (0, 0)),
            pl.BlockSpec((HEAD_DIM, 1), lambda i, j: (0, 0)),
        ],
        out_specs=(pl.BlockSpec((tm, tn), lambda i, j: (i, jnp.minimum(j, n_nat - 1))),
                   pl.BlockSpec((tn, tm), lambda i, j: (jnp.maximum(j - n_nat, 0), i)),
                   pl.BlockSpec((tm, IDX_DIM), lambda i, j: (i, 0)),
                   pl.BlockSpec((IDX_HEADS, tm), lambda i, j: (0, i))),
        scratch_shapes=[pltpu.VMEM((tm, D_MODEL), BF16)],
        compiler_params=_params(2),
        name="inproj",
    )(x2, g, w_nat, w_trT, w_kidx, w_widxT, kg, qg_col)


def _pool_kernel(u_ref, halo_ref, pw_ref, ps_ref, o_ref):
    i = pl.program_id(0)
    tm = u_ref.shape[0]
    u = u_ref[...].astype(F32)
    halo = jnp.where(i == 0, 0.0, halo_ref[...].astype(F32))
    ext = jnp.concatenate([halo, u], axis=0)
    t = i * tm + lax.broadcasted_iota(jnp.int32, (tm, 1), 0)
    for gi, w in enumerate(POOL_WINDOWS):
        sl = slice(gi * POOL_GROUP, (gi + 1) * POOL_GROUP)
        s = ext[:, sl]
        step = 1
        while step < w:
            s = s + pltpu.roll(s, step, axis=0)
            step *= 2
        cnt = jnp.minimum(t + 1, w).astype(F32)
        d = s[HALO:, :] / cnt - u[:, sl]
        y = jnp.dot(d.astype(BF16), pw_ref[gi], preferred_element_type=F32)
        o_ref[:, sl] = (y * ps_ref[:, sl]).astype(BF16)


def _pool(zm, pool_w, pool_scale, *, tm):
    S = zm.shape[0]
    return pl.pallas_call(
        _pool_kernel,
        out_shape=jax.ShapeDtypeStruct((S, D_POOL), BF16),
        grid=(S // tm,),
        in_specs=[
            pl.BlockSpec((tm, D_POOL), lambda i: (i, 0)),
            pl.BlockSpec((HALO, D_POOL), lambda i: (jnp.maximum(i * (tm // HALO) - 1, 0), 0)),
            pl.BlockSpec((len(POOL_WINDOWS), POOL_GROUP, POOL_GROUP), lambda i: (0, 0, 0)),
            pl.BlockSpec((1, D_POOL), lambda i: (0, 0)),
        ],
        out_specs=pl.BlockSpec((tm, D_POOL), lambda i: (i, 0)),
        compiler_params=_params(1),
        name="pool",
    )(zm, zm, pool_w, pool_scale)


I16 = jnp.int16
I16_MIN = -2 ** 15
POOL = 8


def _index_kernel(kidx_ref, qiT_ref, wT_ref, bias_ref, hi_ref, lo_ref, pool_ref, L_ref, n_eq_ref, take_eq_ref,
                  *, T, CK, S, topk):
    qb = pl.program_id(0)
    nck = (qb * T + T + CK - 1) // CK
    q_pos = qb * T + lax.broadcasted_iota(jnp.int32, (1, T), 1)

    def score_chunk(off):
        kc = kidx_ref[pl.ds(off, CK), :]
        acc = None
        for h in range(IDX_HEADS):
            s = jnp.dot(kc, qiT_ref[h * IDX_DIM:(h + 1) * IDX_DIM, :], preferred_element_type=F32)
            t = wT_ref[h:h + 1, :] * jnp.maximum(s, 0.0)
            acc = t if acc is None else acc + t
        bits = lax.bitcast_convert_type(acc, jnp.int32)
        key = bits ^ ((bits >> 31) & jnp.int32(0x7FFFFFFF))
        key_pos = off + lax.broadcasted_iota(jnp.int32, (CK, 1), 0)
        key = jnp.where(key_pos <= q_pos, key, jnp.int32(-2 ** 31))
        hi_ref[pl.ds(off, CK), :] = (key >> 16).astype(I16)
        lo_ref[pl.ds(off, CK), :] = ((key & jnp.int32(0xFFFF)) + I16_MIN).astype(I16)

    def score_body(c, carry):
        for u in range(2):
            score_chunk(pl.multiple_of((2 * c + u) * CK, CK))
        return carry

    lax.fori_loop(0, (nck + 1) // 2, score_body, 0)

    none16 = jnp.full((), I16_MIN, I16)

    def count_ge(ref, cand, trips):
        c16 = cand.astype(I16)

        def count_body(c, acc):
            off = pl.multiple_of(c * CK, CK)
            ge = jnp.where(ref[pl.ds(off, CK), :] >= c16, jnp.ones((), I16), jnp.zeros((), I16))
            ge = ge.reshape(CK // 16, 16, T)
            parts = [ge[r] for r in range(CK // 16)]
            while len(parts) > 1:
                parts = [parts[i] + parts[i + 1] for i in range(0, len(parts), 2)]
            return acc + parts[0]

        acc = lax.fori_loop(0, trips, count_body, jnp.zeros((16, T), I16))
        return acc.astype(jnp.int32).sum(axis=0, keepdims=True)

    def kth_largest(ref, k, trips):
        def bit_body(i, carry):
            v, n_ge_v, n_gt_v = carry
            cand = v + (jnp.int32(1) << (15 - i))
            n_cand = count_ge(ref, cand, trips)
            ok = n_cand >= k
            return jnp.where(ok, cand, v), jnp.where(ok, n_cand, n_ge_v), jnp.where(ok, n_gt_v, n_cand)

        zero = jnp.zeros((1, T), jnp.int32)
        return lax.fori_loop(0, 16, bit_body, (zero + I16_MIN, zero + trips * CK, zero))

    H, n_ge, n_gt = kth_largest(hi_ref, topk, nck)
    H16 = H.astype(I16)
    need = topk - n_gt

    def pool_body(c, carry):
        for g in range(CK // (16 * POOL)):
            a = b = jnp.full((16, T), I16_MIN, I16)
            for r in range(POOL):
                rows = pl.ds(pl.multiple_of(c * CK + (g * POOL + r) * 16, 16), 16)
                x = jnp.where(hi_ref[rows, :] == H16, lo_ref[rows, :], none16)
                up = x > a
                t = jnp.where(up, a, x)
                a = jnp.where(up, x, a)
                b = jnp.where(t > b, t, b)
            base = pl.multiple_of(c * POOL_ROWS + g * 32, 32)
            pool_ref[pl.ds(base, 16), :] = a
            pool_ref[pl.ds(base + 16, 16), :] = b
        return carry

    POOL_ROWS = 2 * CK // POOL
    per_trip = CK // POOL_ROWS
    pool_trips = (nck + per_trip - 1) // per_trip
    lax.fori_loop(0, nck, pool_body, 0)

    def pad_body(c, carry):
        pool_ref[pl.ds(pl.multiple_of(c * POOL_ROWS, POOL_ROWS), POOL_ROWS), :] = jnp.full((POOL_ROWS, T), I16_MIN, I16)
        return carry

    lax.fori_loop(nck, pool_trips * per_trip, pad_body, 0)

    n_tied = n_ge - n_gt
    n_kept = count_ge(pool_ref, jnp.full((1, T), I16_MIN + 1, jnp.int32), pool_trips)
    lost = jnp.sum(jnp.where((n_kept == n_tied) | (H == I16_MIN), 0, 1))

    def second_level(ref, trips):
        L, n_ge_l, n_gt_l = kth_largest(ref, need, trips)
        L_ref[...] = L
        n_eq_ref[...] = n_ge_l - n_gt_l
        take_eq_ref[...] = need - n_gt_l

    @pl.when(lost == 0)
    def _():
        second_level(pool_ref, pool_trips)

    @pl.when(lost != 0)
    def _():
        def mask_body(c, carry):
            off = pl.multiple_of(c * CK, CK)
            lo_ref[pl.ds(off, CK), :] = jnp.where(hi_ref[pl.ds(off, CK), :] == H16, lo_ref[pl.ds(off, CK), :],
                                                  none16)
            return carry

        lax.fori_loop(0, nck, mask_body, 0)
        second_level(lo_ref, nck)

    L16 = L_ref[...].astype(I16)
    take_eq = take_eq_ref[...]
    surplus = jnp.sum(jnp.where((n_eq_ref[...] > take_eq) & (H > I16_MIN), 1, 0))

    def emit(off, sel):
        sel = sel & (hi_ref[pl.ds(off, CK), :] > none16)
        bias_ref[0, pl.ds(off, CK), :] = jnp.where(sel, jnp.zeros((), BF16), jnp.full((), MASK_NEG, BF16))

    @pl.when(surplus == 0)
    def _():
        def emit_body(c, carry):
            off = pl.multiple_of(c * CK, CK)
            hi = hi_ref[pl.ds(off, CK), :]
            emit(off, (hi > H16) | ((hi == H16) & (lo_ref[pl.ds(off, CK), :] >= L16)))
            return carry

        lax.fori_loop(0, nck, emit_body, 0)

    @pl.when(surplus != 0)
    def _():
        tri = (lax.broadcasted_iota(jnp.int32, (CK, CK), 0) >= lax.broadcasted_iota(jnp.int32, (CK, CK), 1))
        tri = jnp.where(tri, 1.0, 0.0).astype(BF16)
        take_f = take_eq.astype(F32)

        def emit_body(c, seen):
            off = pl.multiple_of(c * CK, CK)
            hi = hi_ref[pl.ds(off, CK), :]
            lo = lo_ref[pl.ds(off, CK), :]
            eq = (hi == H16) & (lo == L16) & (hi > none16)
            rank = seen + jnp.dot(tri, jnp.where(eq, jnp.ones((), BF16), jnp.zeros((), BF16)),
                                  preferred_element_type=F32)
            first = jnp.where(rank <= take_f, 1.0, 0.0).astype(BF16) > jnp.zeros((), BF16)
            emit(off, (hi > H16) | ((hi == H16) & (lo > L16)) | (eq & first))
            return rank[CK - 1:CK, :]

        lax.fori_loop(0, nck, emit_body, jnp.zeros((1, T), F32))

    def fill_body(c, carry):
        off = pl.multiple_of(c * CK, CK)
        bias_ref[0, pl.ds(off, CK), :] = jnp.full((CK, T), MASK_NEG, BF16)
        return carry

    lax.fori_loop(nck, S // CK, fill_body, 0)


def _index(kidx, tr, wT, *, T, CK, topk):
    S = kidx.shape[0]
    assert S % (2 * CK) == 0 and CK % T == 0 and (S // CK) % (POOL // 2) == 0
    return pl.pallas_call(
        functools.partial(_index_kernel, T=T, CK=CK, S=S, topk=topk),
        out_shape=jax.ShapeDtypeStruct((S // T, S, T), BF16),
        grid=(S // T,),
        in_specs=[
            pl.BlockSpec((S, IDX_DIM), lambda q: (0, 0)),
            pl.BlockSpec((IDX_HEADS * IDX_DIM, T), lambda q: (2, q)),
            pl.BlockSpec((IDX_HEADS, T), lambda q: (0, q)),
        ],
        out_specs=pl.BlockSpec((1, S, T), lambda q: (q, 0, 0)),
        scratch_shapes=[pltpu.VMEM((S, T), I16), pltpu.VMEM((S, T), I16),
                        pltpu.VMEM((2 * S // POOL, T), I16)] + [pltpu.VMEM((1, T), jnp.int32)] * 3,
        compiler_params=_params(1),
        name="index",
    )(kidx, tr, wT)


def _attn_kernel(qb_ref, kb_ref, kn_ref, qT_ref, vT_ref, bias_ref, o_ref, m_ref, l_ref, acc_ref, *, T, KB,
                 SUB, online):
    i = pl.program_id(0)
    qb = qb_ref[i]
    kb = kb_ref[i]

    @pl.when(kb == 0)
    def _():
        m_ref[...] = jnp.full(m_ref.shape, MASK_NEG, F32)
        l_ref[...] = jnp.zeros(l_ref.shape, F32)
        acc_ref[...] = jnp.zeros(acc_ref.shape, F32)

    items = [(kb0, h) for h in range(N_HEADS) for kb0 in range(0, KB, SUB)]

    def qk(item):
        kb0, h = item
        hs = slice(h * HEAD_DIM, (h + 1) * HEAD_DIM)
        return jnp.dot(kn_ref[kb0:kb0 + SUB, hs], qT_ref[hs, :], preferred_element_type=F32)

    ahead = 4
    pending = [qk(it) for it in items[:ahead]]
    bias = {kb0: bias_ref[0, kb0:kb0 + SUB, :].astype(F32) for kb0 in range(0, KB, SUB)}
    for n, (kb0, h) in enumerate(items):
        hs = slice(h * HEAD_DIM, (h + 1) * HEAD_DIM)
        s = bias[kb0] + pending.pop(0)
        if n + ahead < len(items):
            pending.append(qk(items[n + ahead]))
        if online:
            m_old = m_ref[h]
            m_new = jnp.maximum(m_old, s.max(axis=0, keepdims=True))
            alpha = jnp.exp2(m_old - m_new)
            p = jnp.exp2(s - m_new)
            l_ref[h] = alpha * l_ref[h] + p.sum(axis=0, keepdims=True)
            pv = jnp.dot(vT_ref[hs, kb0:kb0 + SUB], p.astype(BF16), preferred_element_type=F32)
            acc_ref[h] = alpha * acc_ref[h] + pv
            m_ref[h] = m_new
        else:
            p = jnp.exp2(s)
            l_ref[h] += p.sum(axis=0, keepdims=True)
            acc_ref[h] += jnp.dot(vT_ref[hs, kb0:kb0 + SUB], p.astype(BF16), preferred_element_type=F32)

    @pl.when(kb == (qb * T + T - 1) // KB)
    def _():
        for h in range(N_HEADS):
            o_ref[:, h * HEAD_DIM:(h + 1) * HEAD_DIM] = (acc_ref[h] / l_ref[h]).T.astype(BF16)


def _attn(nat, tr, bias, *, T, KB, SUB, online):
    S = nat.shape[0]
    pairs = [(q, k) for q in range(S // T) for k in range((q * T + T - 1) // KB + 1)]
    qb_ids = jnp.asarray(np.array([p[0] for p in pairs], np.int32))
    kb_ids = jnp.asarray(np.array([p[1] for p in pairs], np.int32))
    grid_spec = pltpu.PrefetchScalarGridSpec(
        num_scalar_prefetch=2,
        grid=(len(pairs),),
        in_specs=[
            pl.BlockSpec((KB, D_ATTN), lambda i, qb, kb: (kb[i], 1)),
            pl.BlockSpec((D_ATTN, T), lambda i, qb, kb: (0, qb[i])),
            pl.BlockSpec((D_ATTN, KB), lambda i, qb, kb: (1, kb[i])),
            pl.BlockSpec((1, KB, T), lambda i, qb, kb: (qb[i], kb[i], 0)),
        ],
        out_specs=pl.BlockSpec((T, D_ATTN), lambda i, qb, kb: (qb[i], 0)),
        scratch_shapes=[pltpu.VMEM((N_HEADS, 1, T), F32), pltpu.VMEM((N_HEADS, 1, T), F32),
                        pltpu.VMEM((N_HEADS, HEAD_DIM, T), F32)],
    )
    return pl.pallas_call(
        functools.partial(_attn_kernel, T=T, KB=KB, SUB=SUB, online=online),
        out_shape=jax.ShapeDtypeStruct((S, D_ATTN), BF16),
        grid_spec=grid_spec,
        compiler_params=_params(1),
        name="attn_online" if online else "attn",
    )(qb_ids, kb_ids, nat, tr, tr, bias)


def _outproj_kernel(yp_ref, ya_ref, wp_ref, wa_ref, x_ref, o_ref):
    acc = jnp.dot(yp_ref[...], wp_ref[...], preferred_element_type=F32)
    acc = acc + jnp.dot(ya_ref[...], wa_ref[...], preferred_element_type=F32)
    o_ref[...] = x_ref[...] + acc


def _outproj(yp, ya, w_out, x2, *, tm, tn):
    S = x2.shape[0]
    n_j = D_MODEL // tn
    return pl.pallas_call(
        _outproj_kernel,
        out_shape=jax.ShapeDtypeStruct((S, D_MODEL), F32),
        grid=(S // tm, n_j),
        in_specs=[
            pl.BlockSpec((tm, D_POOL), lambda i, j: (i, 0)),
            pl.BlockSpec((tm, D_ATTN), lambda i, j: (i, 0)),
            pl.BlockSpec((D_POOL, tn), lambda i, j: (0, j)),
            pl.BlockSpec((D_ATTN, tn), lambda i, j: (D_POOL // D_ATTN, j)),
            pl.BlockSpec((tm, tn), lambda i, j: (i, j)),
        ],
        out_specs=pl.BlockSpec((tm, tn), lambda i, j: (i, j)),
        compiler_params=_params(2),
        name="outproj",
    )(yp, ya, w_out, w_out, x2)


def _ffn_kernel(x_ref, halo_ref, g_ref, wg_ref, wv_ref, cwg_ref, cwv_ref, cbg_ref, cbv_ref, wd_ref,
                o_ref, h_ref, acc_ref):
    i = pl.program_id(0)
    f = pl.program_id(1)

    @pl.when(f == 0)
    def _():
        hh = jnp.where(i == 0, 0.0, _rms(halo_ref[...], g_ref[...]))
        h_ref[0:HALO, :] = hh.astype(BF16)
        h_ref[HALO:, :] = _rms(x_ref[...], g_ref[...]).astype(BF16)
        acc_ref[...] = jnp.zeros(acc_ref.shape, F32)

    h = h_ref[...]

    def conv(w_ref, cw_ref, cb_ref):
        up = jnp.dot(h, w_ref[...], preferred_element_type=F32)
        c = (cw_ref[0:1, :] * pltpu.roll(up, 2, axis=0) + cw_ref[1:2, :] * pltpu.roll(up, 1, axis=0)
             + cw_ref[2:3, :] * up)
        return cb_ref[...] + c[HALO:, :]

    cg = conv(wg_ref, cwg_ref, cbg_ref)
    cv = conv(wv_ref, cwv_ref, cbv_ref)
    act = cg * (1.0 / (1.0 + jnp.exp(-cg))) * cv
    acc_ref[...] += jnp.dot(act.astype(BF16), wd_ref[...], preferred_element_type=F32)

    @pl.when(f == pl.num_programs(1) - 1)
    def _():
        o_ref[...] = x_ref[...] + acc_ref[...]


def _ffn(x1, g, w_up, conv_w, conv_b, w_down, *, tm, tf):
    S = x1.shape[0]
    n_f = D_FF // tf
    return pl.pallas_call(
        _ffn_kernel,
        out_shape=jax.ShapeDtypeStruct((S, D_MODEL), F32),
        grid=(S // tm, n_f),
        in_specs=[
            pl.BlockSpec((tm, D_MODEL), lambda i, f: (i, 0)),
            pl.BlockSpec((HALO, D_MODEL), lambda i, f: (jnp.maximum(i * (tm // HALO) - 1, 0), 0)),
            pl.BlockSpec((1, D_MODEL), lambda i, f: (0, 0)),
            pl.BlockSpec((D_MODEL, tf), lambda i, f: (0, f)),
            pl.BlockSpec((D_MODEL, tf), lambda i, f: (0, f + n_f)),
            pl.BlockSpec((CONV_WIDTH, tf), lambda i, f: (0, f)),
            pl.BlockSpec((CONV_WIDTH, tf), lambda i, f: (0, f + n_f)),
            pl.BlockSpec((1, tf), lambda i, f: (0, f)),
            pl.BlockSpec((1, tf), lambda i, f: (0, f + n_f)),
            pl.BlockSpec((tf, D_MODEL), lambda i, f: (f, 0)),
        ],
        out_specs=pl.BlockSpec((tm, D_MODEL), lambda i, f: (i, 0)),
        scratch_shapes=[pltpu.VMEM((HALO + tm, D_MODEL), BF16), pltpu.VMEM((tm, D_MODEL), F32)],
        compiler_params=_params(2),
        name="ffn",
    )(x1, x1, g, w_up, w_up, conv_w, conv_w, conv_b, conv_b, w_down)


def kernel(x, attn_norm_g, w_in, pool_w, pool_scale, q_norm_g, k_norm_g, w_out, ffn_norm_g, w_up,
           conv_w, conv_b, w_down):
    B, S, D = x.shape
    assert B == 1 and D == D_MODEL and w_in.shape == (D_MODEL, D_MAIN + D_TAIL)
    t = TILES
    assert all(S % n == 0 for n in (t.inproj_rows, t.pool_rows, t.outproj_rows, t.ffn_rows, t.attn_keys))
    topk = min(TOPK_MAX, S // 4)
    q_scale = HEAD_DIM ** -0.5 * LOG2E
    x2 = x.reshape(S, D)

    o = D_POOL
    w_q, w_k, w_v = (w_in[:, o + n * D_ATTN:o + (n + 1) * D_ATTN] for n in range(3))
    w_qi = w_in[:, o + 3 * D_ATTN:D_MAIN]
    w_nat = jnp.concatenate([w_in[:, :o], w_k], axis=1).astype(BF16)
    w_trT = jnp.concatenate([w_q, w_v, w_qi], axis=1).T.astype(BF16)
    w_kidx = w_in[:, D_MAIN:D_MAIN + IDX_DIM].astype(BF16)
    w_widxT = w_in[:, D_MAIN + IDX_DIM:].T.astype(BF16)
    nat, tr, kidx, wT = _inproj(x2, attn_norm_g.reshape(1, D), w_nat, w_trT, w_kidx, w_widxT,
                                k_norm_g.reshape(1, HEAD_DIM), q_norm_g.reshape(HEAD_DIM, 1),
                                tm=t.inproj_rows, tn=t.inproj_cols, q_scale=q_scale)

    y_pool = _pool(nat, pool_w.astype(BF16), pool_scale.reshape(1, D_POOL), tm=t.pool_rows)
    bias = _index(kidx, tr, wT, T=t.dsa_queries, CK=t.index_keys, topk=topk)
    score_bound = HEAD_DIM * jnp.max(jnp.abs(q_norm_g)) * jnp.max(jnp.abs(k_norm_g)) * q_scale
    attn = functools.partial(_attn, T=t.dsa_queries, KB=t.attn_keys, SUB=t.attn_sub_keys)
    y_attn = lax.cond(score_bound < MAX_UNSHIFTED_LOG2,
                      functools.partial(attn, online=False), functools.partial(attn, online=True),
                      nat, tr, bias)

    x1 = _outproj(y_pool, y_attn, w_out.astype(BF16), x2, tm=t.outproj_rows, tn=t.outproj_cols)
    out = _ffn(x1, ffn_norm_g.reshape(1, D), w_up.astype(BF16), conv_w, conv_b.reshape(1, 2 * D_FF),
               w_down.astype(BF16), tm=t.ffn_rows, tf=t.ffn_cols)
    return out.reshape(B, S, D)
```

```python
import functools
from typing import NamedTuple

import numpy as np
import jax
import jax.numpy as jnp
from jax import lax
from jax.experimental import pallas as pl
from jax.experimental.pallas import tpu as pltpu

D_MODEL = 2048
D_POOL = 1024
POOL_WINDOWS = (2, 4, 8, 16)
POOL_GROUP = D_POOL // len(POOL_WINDOWS)
D_ATTN = 1024
HEAD_DIM = 128
N_HEADS = D_ATTN // HEAD_DIM
IDX_HEADS = 16
IDX_DIM = 64
TOPK_MAX = 256
D_FF = 5632
CONV_WIDTH = 3
EPS = 1e-6

D_MAIN = D_POOL + 3 * D_ATTN + IDX_HEADS * IDX_DIM
D_TAIL = IDX_DIM + IDX_HEADS
HALO = 16
MASK_NEG = -1e30
LOG2E = 1.4426950408889634
MAX_UNSHIFTED_LOG2 = 60.0
V7X_VMEM_BYTES = 64 * 1024 * 1024
VMEM_LIMIT_BYTES = V7X_VMEM_BYTES * 7 // 8


class _Tiles(NamedTuple):
    inproj_rows: int = 1024
    inproj_cols: int = 1024
    pool_rows: int = 1024
    dsa_queries: int = 256
    index_keys: int = 512
    attn_keys: int = 1024
    attn_sub_keys: int = 512
    outproj_rows: int = 1024
    outproj_cols: int = 1024
    ffn_rows: int = 1024
    ffn_cols: int = 512


TILES = _Tiles()

F32 = jnp.float32
BF16 = jnp.bfloat16


def _params(n_axes):
    return pltpu.CompilerParams(dimension_semantics=("arbitrary",) * n_axes,
                                vmem_limit_bytes=VMEM_LIMIT_BYTES)


def _rms(xf, g):
    return xf * lax.rsqrt(jnp.mean(xf * xf, axis=-1, keepdims=True) + EPS) * g


_NT_DIMS = (((1,), (1,)), ((), ()))


def _inproj_kernel(x_ref, g_ref, wn_ref, wt_ref, wk_ref, ww_ref, kg_ref, qg_ref, nat_ref, tr_ref, kidx_ref,
                   wT_ref, h_ref, *, tn, q_scale):
    j = pl.program_id(1)
    n_nat = (D_POOL + D_ATTN) // tn

    @pl.when(j == 0)
    def _():
        h = _rms(x_ref[...], g_ref[...]).astype(BF16)
        h_ref[...] = h
        kidx_ref[...] = jnp.dot(h, wk_ref[...], preferred_element_type=F32).astype(BF16)
        w = lax.dot_general(ww_ref[...], h, _NT_DIMS, preferred_element_type=F32)
        wT_ref[...] = w * (IDX_HEADS ** -0.5) * (IDX_DIM ** -0.5)

    @pl.when(j < n_nat)
    def _():
        z = jnp.dot(h_ref[...], wn_ref[...], preferred_element_type=F32)

        @pl.when(j < D_POOL // tn)
        def _():
            nat_ref[...] = z.astype(BF16)

        @pl.when(j >= D_POOL // tn)
        def _():
            for c in range(tn // HEAD_DIM):
                sl = slice(c * HEAD_DIM, (c + 1) * HEAD_DIM)
                nat_ref[:, sl] = _rms(z[:, sl], kg_ref[...]).astype(BF16)

    @pl.when(j >= n_nat)
    def _():
        zT = lax.dot_general(wt_ref[...], h_ref[...], _NT_DIMS, preferred_element_type=F32)

        @pl.when(j < n_nat + D_ATTN // tn)
        def _():
            for c in range(tn // HEAD_DIM):
                sl = slice(c * HEAD_DIM, (c + 1) * HEAD_DIM)
                zc = zT[sl, :]
                inv = lax.rsqrt(jnp.mean(zc * zc, axis=0, keepdims=True) + EPS)
                tr_ref[sl, :] = (zc * inv * (qg_ref[...] * q_scale)).astype(BF16)

        @pl.when(j >= n_nat + D_ATTN // tn)
        def _():
            tr_ref[...] = zT.astype(BF16)


def _inproj(x2, g, w_nat, w_trT, w_kidx, w_widxT, kg, qg_col, *, tm, tn, q_scale):
    S = x2.shape[0]
    n_nat, n_tr = w_nat.shape[1] // tn, w_trT.shape[0] // tn
    return pl.pallas_call(
        functools.partial(_inproj_kernel, tn=tn, q_scale=q_scale),
        out_shape=(jax.ShapeDtypeStruct((S, w_nat.shape[1]), BF16),
                   jax.ShapeDtypeStruct((w_trT.shape[0], S), BF16),
                   jax.ShapeDtypeStruct((S, IDX_DIM), BF16),
                   jax.ShapeDtypeStruct((IDX_HEADS, S), F32)),
        grid=(S // tm, n_nat + n_tr),
        in_specs=[
            pl.BlockSpec((tm, D_MODEL), lambda i, j: (i, 0)),
            pl.BlockSpec((1, D_MODEL), lambda i, j: (0, 0)),
            pl.BlockSpec((D_MODEL, tn), lambda i, j: (0, jnp.minimum(j, n_nat - 1))),
            pl.BlockSpec((tn, D_MODEL), lambda i, j: (jnp.maximum(j - n_nat, 0), 0)),
            pl.BlockSpec((D_MODEL, IDX_DIM), lambda i, j: (0, 0)),
            pl.BlockSpec((IDX_HEADS, D_MODEL), lambda i, j: (0, 0)),
            pl.BlockSpec((1, HEAD_DIM), lambda i, j: (0, 0)),
            pl.BlockSpec((HEAD_DIM, 1), lambda i, j: (0, 0)),
        ],
        out_specs=(pl.BlockSpec((tm, tn), lambda i, j: (i, jnp.minimum(j, n_nat - 1))),
                   pl.BlockSpec((tn, tm), lambda i, j: (jnp.maximum(j - n_nat, 0), i)),
                   pl.BlockSpec((tm, IDX_DIM), lambda i, j: (i, 0)),
                   pl.BlockSpec((IDX_HEADS, tm), lambda i, j: (0, i))),
        scratch_shapes=[pltpu.VMEM((tm, D_MODEL), BF16)],
        compiler_params=_params(2),
        name="inproj",
    )(x2, g, w_nat, w_trT, w_kidx, w_widxT, kg, qg_col)


def _pool_kernel(u_ref, halo_ref, pw_ref, ps_ref, o_ref):
    i = pl.program_id(0)
    tm = u_ref.shape[0]
    u = u_ref[...].astype(F32)
    halo = jnp.where(i == 0, 0.0, halo_ref[...].astype(F32))
    ext = jnp.concatenate([halo, u], axis=0)
    t = i * tm + lax.broadcasted_iota(jnp.int32, (tm, 1), 0)
    for gi, w in enumerate(POOL_WINDOWS):
        sl = slice(gi * POOL_GROUP, (gi + 1) * POOL_GROUP)
        s = ext[:, sl]
        step = 1
        while step < w:
            s = s + pltpu.roll(s, step, axis=0)
            step *= 2
        cnt = jnp.minimum(t + 1, w).astype(F32)
        d = s[HALO:, :] / cnt - u[:, sl]
        y = jnp.dot(d.astype(BF16), pw_ref[gi], preferred_element_type=F32)
        o_ref[:, sl] = (y * ps_ref[:, sl]).astype(BF16)


def _pool(zm, pool_w, pool_scale, *, tm):
    S = zm.shape[0]
    return pl.pallas_call(
        _pool_kernel,
        out_shape=jax.ShapeDtypeStruct((S, D_POOL), BF16),
        grid=(S // tm,),
        in_specs=[
            pl.BlockSpec((tm, D_POOL), lambda i: (i, 0)),
            pl.BlockSpec((HALO, D_POOL), lambda i: (jnp.maximum(i * (tm // HALO) - 1, 0), 0)),
            pl.BlockSpec((len(POOL_WINDOWS), POOL_GROUP, POOL_GROUP), lambda i: (0, 0, 0)),
            pl.BlockSpec((1, D_POOL), lambda i: (0, 0)),
        ],
        out_specs=pl.BlockSpec((tm, D_POOL), lambda i: (i, 0)),
        compiler_params=_params(1),
        name="pool",
    )(zm, zm, pool_w, pool_scale)


I16 = jnp.int16
I16_MIN = -2 ** 15
POOL = 8


def _index_kernel(kidx_ref, qiT_ref, wT_ref, bias_ref, hi_ref, lo_ref, pool_ref, L_ref, n_eq_ref, take_eq_ref,
                  *, T, CK, S, topk):
    qb = pl.program_id(0)
    nck = (qb * T + T + CK - 1) // CK
    q_pos = qb * T + lax.broadcasted_iota(jnp.int32, (1, T), 1)

    def score_chunk(off):
        kc = kidx_ref[pl.ds(off, CK), :]
        acc = None
        for h in range(IDX_HEADS):
            s = jnp.dot(kc, qiT_ref[h * IDX_DIM:(h + 1) * IDX_DIM, :], preferred_element_type=F32)
            t = wT_ref[h:h + 1, :] * jnp.maximum(s, 0.0)
            acc = t if acc is None else acc + t
        bits = lax.bitcast_convert_type(acc, jnp.int32)
        key = bits ^ ((bits >> 31) & jnp.int32(0x7FFFFFFF))
        key_pos = off + lax.broadcasted_iota(jnp.int32, (CK, 1), 0)
        key = jnp.where(key_pos <= q_pos, key, jnp.int32(-2 ** 31))
        hi_ref[pl.ds(off, CK), :] = (key >> 16).astype(I16)
        lo_ref[pl.ds(off, CK), :] = ((key & jnp.int32(0xFFFF)) + I16_MIN).astype(I16)

    def score_body(c, carry):
        for u in range(2):
            score_chunk(pl.multiple_of((2 * c + u) * CK, CK))
        return carry

    lax.fori_loop(0, (nck + 1) // 2, score_body, 0)

    none16 = jnp.full((), I16_MIN, I16)

    def count_ge(ref, cand, trips):
        c16 = cand.astype(I16)

        def count_body(c, acc):
            off = pl.multiple_of(c * CK, CK)
            ge = jnp.where(ref[pl.ds(off, CK), :] >= c16, jnp.ones((), I16), jnp.zeros((), I16))
            ge = ge.reshape(CK // 16, 16, T)
            parts = [ge[r] for r in range(CK // 16)]
            while len(parts) > 1:
                parts = [parts[i] + parts[i + 1] for i in range(0, len(parts), 2)]
            return acc + parts[0]

        acc = lax.fori_loop(0, trips, count_body, jnp.zeros((16, T), I16))
        return acc.astype(jnp.int32).sum(axis=0, keepdims=True)

    def kth_largest(ref, k, trips):
        def bit_body(i, carry):
            v, n_ge_v, n_gt_v = carry
            cand = v + (jnp.int32(1) << (15 - i))
            n_cand = count_ge(ref, cand, trips)
            ok = n_cand >= k
            return jnp.where(ok, cand, v), jnp.where(ok, n_cand, n_ge_v), jnp.where(ok, n_gt_v, n_cand)

        zero = jnp.zeros((1, T), jnp.int32)
        return lax.fori_loop(0, 16, bit_body, (zero + I16_MIN, zero + trips * CK, zero))

    H, n_ge, n_gt = kth_largest(hi_ref, topk, nck)
    H16 = H.astype(I16)
    need = topk - n_gt

    def pool_body(c, carry):
        for g in range(CK // (16 * POOL)):
            a = b = jnp.full((16, T), I16_MIN, I16)
            for r in range(POOL):
                rows = pl.ds(pl.multiple_of(c * CK + (g * POOL + r) * 16, 16), 16)
                x = jnp.where(hi_ref[rows, :] == H16, lo_ref[rows, :], none16)
                up = x > a
                t = jnp.where(up, a, x)
                a = jnp.where(up, x, a)
                b = jnp.where(t > b, t, b)
            base = pl.multiple_of(c * POOL_ROWS + g * 32, 32)
            pool_ref[pl.ds(base, 16), :] = a
            pool_ref[pl.ds(base + 16, 16), :] = b
        return carry

    POOL_ROWS = 2 * CK // POOL
    per_trip = CK // POOL_ROWS
    pool_trips = (nck + per_trip - 1) // per_trip
    lax.fori_loop(0, nck, pool_body, 0)

    def pad_body(c, carry):
        pool_ref[pl.ds(pl.multiple_of(c * POOL_ROWS, POOL_ROWS), POOL_ROWS), :] = jnp.full((POOL_ROWS, T), I16_MIN, I16)
        return carry

    lax.fori_loop(nck, pool_trips * per_trip, pad_body, 0)

    n_tied = n_ge - n_gt
    n_kept = count_ge(pool_ref, jnp.full((1, T), I16_MIN + 1, jnp.int32), pool_trips)
    lost = jnp.sum(jnp.where((n_kept == n_tied) | (H == I16_MIN), 0, 1))

    def second_level(ref, trips):
        L, n_ge_l, n_gt_l = kth_largest(ref, need, trips)
        L_ref[...] = L
        n_eq_ref[...] = n_ge_l - n_gt_l
        take_eq_ref[...] = need - n_gt_l

    @pl.when(lost == 0)
    def _():
        second_level(pool_ref, pool_trips)

    @pl.when(lost != 0)
    def _():
        def mask_body(c, carry):
            off = pl.multiple_of(c * CK, CK)
            lo_ref[pl.ds(off, CK), :] = jnp.where(hi_ref[pl.ds(off, CK), :] == H16, lo_ref[pl.ds(off, CK), :],
                                                  none16)
            return carry

        lax.fori_loop(0, nck, mask_body, 0)
        second_level(lo_ref, nck)

    L16 = L_ref[...].astype(I16)
    take_eq = take_eq_ref[...]
    surplus = jnp.sum(jnp.where((n_eq_ref[...] > take_eq) & (H > I16_MIN), 1, 0))

    def emit(off, sel):
        sel = sel & (hi_ref[pl.ds(off, CK), :] > none16)
        bias_ref[0, pl.ds(off, CK), :] = jnp.where(sel, jnp.zeros((), BF16), jnp.full((), MASK_NEG, BF16))

    @pl.when(surplus == 0)
    def _():
        def emit_body(c, carry):
            off = pl.multiple_of(c * CK, CK)
            hi = hi_ref[pl.ds(off, CK), :]
            emit(off, (hi > H16) | ((hi == H16) & (lo_ref[pl.ds(off, CK), :] >= L16)))
            return carry

        lax.fori_loop(0, nck, emit_body, 0)

    @pl.when(surplus != 0)
    def _():
        tri = (lax.broadcasted_iota(jnp.int32, (CK, CK), 0) >= lax.broadcasted_iota(jnp.int32, (CK, CK), 1))
        tri = jnp.where(tri, 1.0, 0.0).astype(BF16)
        take_f = take_eq.astype(F32)

        def emit_body(c, seen):
            off = pl.multiple_of(c * CK, CK)
            hi = hi_ref[pl.ds(off, CK), :]
            lo = lo_ref[pl.ds(off, CK), :]
            eq = (hi == H16) & (lo == L16) & (hi > none16)
            rank = seen + jnp.dot(tri, jnp.where(eq, jnp.ones((), BF16), jnp.zeros((), BF16)),
                                  preferred_element_type=F32)
            first = jnp.where(rank <= take_f, 1.0, 0.0).astype(BF16) > jnp.zeros((), BF16)
            emit(off, (hi > H16) | ((hi == H16) & (lo > L16)) | (eq & first))
            return rank[CK - 1:CK, :]

        lax.fori_loop(0, nck, emit_body, jnp.zeros((1, T), F32))

    def fill_body(c, carry):
        off = pl.multiple_of(c * CK, CK)
        bias_ref[0, pl.ds(off, CK), :] = jnp.full((CK, T), MASK_NEG, BF16)
        return carry

    lax.fori_loop(nck, S // CK, fill_body, 0)


def _index(kidx, tr, wT, *, T, CK, topk):
    S = kidx.shape[0]
    assert S % (2 * CK) == 0 and CK % T == 0 and (S // CK) % (POOL // 2) == 0
    return pl.pallas_call(
        functools.partial(_index_kernel, T=T, CK=CK, S=S, topk=topk),
        out_shape=jax.ShapeDtypeStruct((S // T, S, T), BF16),
        grid=(S // T,),
        in_specs=[
            pl.BlockSpec((S, IDX_DIM), lambda q: (0, 0)),
            pl.BlockSpec((IDX_HEADS * IDX_DIM, T), lambda q: (2, q)),
            pl.BlockSpec((IDX_HEADS, T), lambda q: (0, q)),
        ],
        out_specs=pl.BlockSpec((1, S, T), lambda q: (q, 0, 0)),
        scratch_shapes=[pltpu.VMEM((S, T), I16), pltpu.VMEM((S, T), I16),
                        pltpu.VMEM((2 * S // POOL, T), I16)] + [pltpu.VMEM((1, T), jnp.int32)] * 3,
        compiler_params=_params(1),
        name="index",
    )(kidx, tr, wT)


def _attn_kernel(qb_ref, kb_ref, kn_ref, qT_ref, vT_ref, bias_ref, o_ref, m_ref, l_ref, acc_ref, *, T, KB,
                 SUB, online):
    i = pl.program_id(0)
    qb = qb_ref[i]
    kb = kb_ref[i]

    @pl.when(kb == 0)
    def _():
        m_ref[...] = jnp.full(m_ref.shape, MASK_NEG, F32)
        l_ref[...] = jnp.zeros(l_ref.shape, F32)
        acc_ref[...] = jnp.zeros(acc_ref.shape, F32)

    items = [(kb0, h) for h in range(N_HEADS) for kb0 in range(0, KB, SUB)]

    def qk(item):
        kb0, h = item
        hs = slice(h * HEAD_DIM, (h + 1) * HEAD_DIM)
        return jnp.dot(kn_ref[kb0:kb0 + SUB, hs], qT_ref[hs, :], preferred_element_type=F32)

    ahead = 4
    pending = [qk(it) for it in items[:ahead]]
    bias = {kb0: bias_ref[0, kb0:kb0 + SUB, :].astype(F32) for kb0 in range(0, KB, SUB)}
    for n, (kb0, h) in enumerate(items):
        hs = slice(h * HEAD_DIM, (h + 1) * HEAD_DIM)
        s = bias[kb0] + pending.pop(0)
        if n + ahead < len(items):
            pending.append(qk(items[n + ahead]))
        if online:
            m_old = m_ref[h]
            m_new = jnp.maximum(m_old, s.max(axis=0, keepdims=True))
            alpha = jnp.exp2(m_old - m_new)
            p = jnp.exp2(s - m_new)
            l_ref[h] = alpha * l_ref[h] + p.sum(axis=0, keepdims=True)
            pv = jnp.dot(vT_ref[hs, kb0:kb0 + SUB], p.astype(BF16), preferred_element_type=F32)
            acc_ref[h] = alpha * acc_ref[h] + pv
            m_ref[h] = m_new
        else:
            p = jnp.exp2(s)
            l_ref[h] += p.sum(axis=0, keepdims=True)
            acc_ref[h] += jnp.dot(vT_ref[hs, kb0:kb0 + SUB], p.astype(BF16), preferred_element_type=F32)

    @pl.when(kb == (qb * T + T - 1) // KB)
    def _():
        for h in range(N_HEADS):
            o_ref[:, h * HEAD_DIM:(h + 1) * HEAD_DIM] = (acc_ref[h] / l_ref[h]).T.astype(BF16)


def _attn(nat, tr, bias, *, T, KB, SUB, online):
    S = nat.shape[0]
    pairs = [(q, k) for q in range(S // T) for k in range((q * T + T - 1) // KB + 1)]
    qb_ids = jnp.asarray(np.array([p[0] for p in pairs], np.int32))
    kb_ids = jnp.asarray(np.array([p[1] for p in pairs], np.int32))
    grid_spec = pltpu.PrefetchScalarGridSpec(
        num_scalar_prefetch=2,
        grid=(len(pairs),),
        in_specs=[
            pl.BlockSpec((KB, D_ATTN), lambda i, qb, kb: (kb[i], 1)),
            pl.BlockSpec((D_ATTN, T), lambda i, qb, kb: (0, qb[i])),
            pl.BlockSpec((D_ATTN, KB), lambda i, qb, kb: (1, kb[i])),
            pl.BlockSpec((1, KB, T), lambda i, qb, kb: (qb[i], kb[i], 0)),
        ],
        out_specs=pl.BlockSpec((T, D_ATTN), lambda i, qb, kb: (qb[i], 0)),
        scratch_shapes=[pltpu.VMEM((N_HEADS, 1, T), F32), pltpu.VMEM((N_HEADS, 1, T), F32),
                        pltpu.VMEM((N_HEADS, HEAD_DIM, T), F32)],
    )
    return pl.pallas_call(
        functools.partial(_attn_kernel, T=T, KB=KB, SUB=SUB, online=online),
        out_shape=jax.ShapeDtypeStruct((S, D_ATTN), BF16),
        grid_spec=grid_spec,
        compiler_params=_params(1),
        name="attn_online" if online else "attn",
    )(qb_ids, kb_ids, nat, tr, tr, bias)


def _outproj_kernel(yp_ref, ya_ref, wp_ref, wa_ref, x_ref, o_ref):
    acc = jnp.dot(yp_ref[...], wp_ref[...], preferred_element_type=F32)
    acc = acc + jnp.dot(ya_ref[...], wa_ref[...], preferred_element_type=F32)
    o_ref[...] = x_ref[...] + acc


def _outproj(yp, ya, w_out, x2, *, tm, tn):
    S = x2.shape[0]
    n_j = D_MODEL // tn
    return pl.pallas_call(
        _outproj_kernel,
        out_shape=jax.ShapeDtypeStruct((S, D_MODEL), F32),
        grid=(S // tm, n_j),
        in_specs=[
            pl.BlockSpec((tm, D_POOL), lambda i, j: (i, 0)),
            pl.BlockSpec((tm, D_ATTN), lambda i, j: (i, 0)),
            pl.BlockSpec((D_POOL, tn), lambda i, j: (0, j)),
            pl.BlockSpec((D_ATTN, tn), lambda i, j: (D_POOL // D_ATTN, j)),
            pl.BlockSpec((tm, tn), lambda i, j: (i, j)),
        ],
        out_specs=pl.BlockSpec((tm, tn), lambda i, j: (i, j)),
        compiler_params=_params(2),
        name="outproj",
    )(yp, ya, w_out, w_out, x2)


def _ffn_kernel(x_ref, halo_ref, g_ref, wg_ref, wv_ref, cwg_ref, cwv_ref, cbg_ref, cbv_ref, wd_ref,
                o_ref, h_ref):
    i = pl.program_id(0)
    f = pl.program_id(1)

    @pl.when(f == 0)
    def _():
        hh = jnp.where(i == 0, 0.0, _rms(halo_ref[...], g_ref[...]))
        h_ref[0:HALO, :] = hh.astype(BF16)
        x = x_ref[...]
        h_ref[HALO:, :] = _rms(x, g_ref[...]).astype(BF16)
        o_ref[...] = x

    h = h_ref[...]

    def conv(w_ref, cw_ref, cb_ref):
        up = jnp.dot(h, w_ref[...], preferred_element_type=F32)
        c = (cw_ref[0:1, :] * pltpu.roll(up, 2, axis=0) + cw_ref[1:2, :] * pltpu.roll(up, 1, axis=0)
             + cw_ref[2:3, :] * up)
        return cb_ref[...] + c[HALO:, :]

    cg = conv(wg_ref, cwg_ref, cbg_ref)
    cv = conv(wv_ref, cwv_ref, cbv_ref)
    act = cg * (1.0 / (1.0 + jnp.exp(-cg))) * cv
    o_ref[...] += jnp.dot(act.astype(BF16), wd_ref[...], preferred_element_type=F32)


def _ffn(x1, g, w_up, conv_w, conv_b, w_down, *, tm, tf):
    S = x1.shape[0]
    n_f = D_FF // tf
    return pl.pallas_call(
        _ffn_kernel,
        out_shape=jax.ShapeDtypeStruct((S, D_MODEL), F32),
        grid=(S // tm, n_f),
        in_specs=[
            pl.BlockSpec((tm, D_MODEL), lambda i, f: (i, 0)),
            pl.BlockSpec((HALO, D_MODEL), lambda i, f: (jnp.maximum(i * (tm // HALO) - 1, 0), 0)),
            pl.BlockSpec((1, D_MODEL), lambda i, f: (0, 0)),
            pl.BlockSpec((D_MODEL, tf), lambda i, f: (0, f)),
            pl.BlockSpec((D_MODEL, tf), lambda i, f: (0, f + n_f)),
            pl.BlockSpec((CONV_WIDTH, tf), lambda i, f: (0, f)),
            pl.BlockSpec((CONV_WIDTH, tf), lambda i, f: (0, f + n_f)),
            pl.BlockSpec((1, tf), lambda i, f: (0, f)),
            pl.BlockSpec((1, tf), lambda i, f: (0, f + n_f)),
            pl.BlockSpec((tf, D_MODEL), lambda i, f: (f, 0)),
        ],
        out_specs=pl.BlockSpec((tm, D_MODEL), lambda i, f: (i, 0)),
        scratch_shapes=[pltpu.VMEM((HALO + tm, D_MODEL), BF16)],
        compiler_params=_params(2),
        name="ffn",
    )(x1, x1, g, w_up, w_up, conv_w, conv_w, conv_b, conv_b, w_down)


def kernel(x, attn_norm_g, w_in, pool_w, pool_scale, q_norm_g, k_norm_g, w_out, ffn_norm_g, w_up,
           conv_w, conv_b, w_down):
    B, S, D = x.shape
    assert B == 1 and D == D_MODEL and w_in.shape == (D_MODEL, D_MAIN + D_TAIL)
    t = TILES
    assert all(S % n == 0 for n in (t.inproj_rows, t.pool_rows, t.outproj_rows, t.ffn_rows, t.attn_keys))
    topk = min(TOPK_MAX, S // 4)
    q_scale = HEAD_DIM ** -0.5 * LOG2E
    x2 = x.reshape(S, D)

    o = D_POOL
    w_q, w_k, w_v = (w_in[:, o + n * D_ATTN:o + (n + 1) * D_ATTN] for n in range(3))
    w_qi = w_in[:, o + 3 * D_ATTN:D_MAIN]
    w_nat = jnp.concatenate([w_in[:, :o], w_k], axis=1).astype(BF16)
    w_trT = jnp.concatenate([w_q, w_v, w_qi], axis=1).T.astype(BF16)
    w_kidx = w_in[:, D_MAIN:D_MAIN + IDX_DIM].astype(BF16)
    w_widxT = w_in[:, D_MAIN + IDX_DIM:].T.astype(BF16)
    nat, tr, kidx, wT = _inproj(x2, attn_norm_g.reshape(1, D), w_nat, w_trT, w_kidx, w_widxT,
                                k_norm_g.reshape(1, HEAD_DIM), q_norm_g.reshape(HEAD_DIM, 1),
                                tm=t.inproj_rows, tn=t.inproj_cols, q_scale=q_scale)

    y_pool = _pool(nat, pool_w.astype(BF16), pool_scale.reshape(1, D_POOL), tm=t.pool_rows)
    bias = _index(kidx, tr, wT, T=t.dsa_queries, CK=t.index_keys, topk=topk)
    score_bound = HEAD_DIM * jnp.max(jnp.abs(q_norm_g)) * jnp.max(jnp.abs(k_norm_g)) * q_scale
    attn = functools.partial(_attn, T=t.dsa_queries, KB=t.attn_keys, SUB=t.attn_sub_keys)
    y_attn = lax.cond(score_bound < MAX_UNSHIFTED_LOG2,
                      functools.partial(attn, online=False), functools.partial(attn, online=True),
                      nat, tr, bias)

    x1 = _outproj(y_pool, y_attn, w_out.astype(BF16), x2, tm=t.outproj_rows, tn=t.outproj_cols)
    out = _ffn(x1, ffn_norm_g.reshape(1, D), w_up.astype(BF16), conv_w, conv_b.reshape(1, 2 * D_FF),
               w_down.astype(BF16), tm=t.ffn_rows, tf=t.ffn_cols)
    return out.reshape(B, S, D)
```

```python
import functools
from typing import NamedTuple

import numpy as np
import jax
import jax.numpy as jnp
from jax import lax
from jax.experimental import pallas as pl
from jax.experimental.pallas import tpu as pltpu

D_MODEL = 2048
D_POOL = 1024
POOL_WINDOWS = (2, 4, 8, 16)
POOL_GROUP = D_POOL // len(POOL_WINDOWS)
D_ATTN = 1024
HEAD_DIM = 128
N_HEADS = D_ATTN // HEAD_DIM
IDX_HEADS = 16
IDX_DIM = 64
TOPK_MAX = 256
D_FF = 5632
CONV_WIDTH = 3
EPS = 1e-6

D_MAIN = D_POOL + 3 * D_ATTN + IDX_HEADS * IDX_DIM
D_TAIL = IDX_DIM + IDX_HEADS
TAIL_PAD = 128
HALO = 16
MASK_NEG = -1e30
LOG2E = 1.4426950408889634
MAX_UNSHIFTED_LOG2 = 60.0
V7X_VMEM_BYTES = 64 * 1024 * 1024
VMEM_LIMIT_BYTES = V7X_VMEM_BYTES * 7 // 8


class _Tiles(NamedTuple):
    inproj_rows: int = 1024
    inproj_cols: int = 1024
    pool_rows: int = 1024
    dsa_queries: int = 256
    index_keys: int = 512
    attn_keys: int = 1024
    attn_sub_keys: int = 512
    outproj_rows: int = 512
    outproj_cols: int = 2048
    ffn_rows: int = 1024
    ffn_cols: int = 512


TILES = _Tiles()

F32 = jnp.float32
BF16 = jnp.bfloat16


def _params(n_axes):
    return pltpu.CompilerParams(dimension_semantics=("arbitrary",) * n_axes,
                                vmem_limit_bytes=VMEM_LIMIT_BYTES)


def _rms(xf, g):
    return xf * lax.rsqrt(jnp.mean(xf * xf, axis=-1, keepdims=True) + EPS) * g


_NT_DIMS = (((1,), (1,)), ((), ()))


def _inproj_kernel(x_ref, g_ref, wn_ref, wt_ref, wtail_ref, kg_ref, qg_ref, nat_ref, tr_ref, kidx_ref,
                   wT_ref, h_ref, *, tn, q_scale):
    j = pl.program_id(1)
    n_nat = (D_POOL + D_ATTN) // tn

    @pl.when(j == 0)
    def _():
        h = _rms(x_ref[...], g_ref[...]).astype(BF16)
        h_ref[...] = h
        tail = jnp.dot(h, wtail_ref[...], preferred_element_type=F32)
        kidx_ref[...] = tail[:, :IDX_DIM].astype(BF16)
        wT_ref[...] = tail.T[IDX_DIM:D_TAIL, :] * (IDX_HEADS ** -0.5) * (IDX_DIM ** -0.5)

    @pl.when(j < n_nat)
    def _():
        z = jnp.dot(h_ref[...], wn_ref[...], preferred_element_type=F32)

        @pl.when(j < D_POOL // tn)
        def _():
            nat_ref[...] = z.astype(BF16)

        @pl.when(j >= D_POOL // tn)
        def _():
            for c in range(tn // HEAD_DIM):
                sl = slice(c * HEAD_DIM, (c + 1) * HEAD_DIM)
                nat_ref[:, sl] = _rms(z[:, sl], kg_ref[...]).astype(BF16)

    @pl.when(j >= n_nat)
    def _():
        zT = lax.dot_general(wt_ref[...], h_ref[...], _NT_DIMS, preferred_element_type=F32)

        @pl.when(j < n_nat + D_ATTN // tn)
        def _():
            for c in range(tn // HEAD_DIM):
                sl = slice(c * HEAD_DIM, (c + 1) * HEAD_DIM)
                zc = zT[sl, :]
                inv = lax.rsqrt(jnp.mean(zc * zc, axis=0, keepdims=True) + EPS)
                tr_ref[sl, :] = (zc * inv * (qg_ref[...] * q_scale)).astype(BF16)

        @pl.when(j >= n_nat + D_ATTN // tn)
        def _():
            tr_ref[...] = zT.astype(BF16)


def _inproj(x2, g, w_nat, w_trT, w_tail, kg, qg_col, *, tm, tn, q_scale):
    S = x2.shape[0]
    n_nat, n_tr = w_nat.shape[1] // tn, w_trT.shape[0] // tn
    return pl.pallas_call(
        functools.partial(_inproj_kernel, tn=tn, q_scale=q_scale),
        out_shape=(jax.ShapeDtypeStruct((S, w_nat.shape[1]), BF16),
                   jax.ShapeDtypeStruct((w_trT.shape[0], S), BF16),
                   jax.ShapeDtypeStruct((S, IDX_DIM), BF16),
                   jax.ShapeDtypeStruct((IDX_HEADS, S), F32)),
        grid=(S // tm, n_nat + n_tr),
        in_specs=[
            pl.BlockSpec((tm, D_MODEL), lambda i, j: (i, 0)),
            pl.BlockSpec((1, D_MODEL), lambda i, j: (0, 0)),
            pl.BlockSpec((D_MODEL, tn), lambda i, j: (0, jnp.minimum(j, n_nat - 1))),
            pl.BlockSpec((tn, D_MODEL), lambda i, j: (jnp.maximum(j - n_nat, 0), 0)),
            pl.BlockSpec((D_MODEL, TAIL_PAD), lambda i, j: (0, 0)),
            pl.BlockSpec((1, HEAD_DIM), lambda i, j: (0, 0)),
            pl.BlockSpec((HEAD_DIM, 1), lambda i, j: (0, 0)),
        ],
        out_specs=(pl.BlockSpec((tm, tn), lambda i, j: (i, jnp.minimum(j, n_nat - 1))),
                   pl.BlockSpec((tn, tm), lambda i, j: (jnp.maximum(j - n_nat, 0), i)),
                   pl.BlockSpec((tm, IDX_DIM), lambda i, j: (i, 0)),
                   pl.BlockSpec((IDX_HEADS, tm), lambda i, j: (0, i))),
        scratch_shapes=[pltpu.VMEM((tm, D_MODEL), BF16)],
        compiler_params=_params(2),
        name="inproj",
    )(x2, g, w_nat, w_trT, w_tail, kg, qg_col)


def _pool_kernel(u_ref, halo_ref, pw_ref, ps_ref, o_ref):
    i = pl.program_id(0)
    tm = u_ref.shape[0]
    u = u_ref[...].astype(F32)
    halo = jnp.where(i == 0, 0.0, halo_ref[...].astype(F32))
    ext = jnp.concatenate([halo, u], axis=0)
    t = i * tm + lax.broadcasted_iota(jnp.int32, (tm, 1), 0)
    for gi, w in enumerate(POOL_WINDOWS):
        sl = slice(gi * POOL_GROUP, (gi + 1) * POOL_GROUP)
        s = ext[:, sl]
        step = 1
        while step < w:
            s = s + pltpu.roll(s, step, axis=0)
            step *= 2
        cnt = jnp.minimum(t + 1, w).astype(F32)
        d = s[HALO:, :] / cnt - u[:, sl]
        y = jnp.dot(d.astype(BF16), pw_ref[gi], preferred_element_type=F32)
        o_ref[:, sl] = (y * ps_ref[:, sl]).astype(BF16)


def _pool(zm, pool_w, pool_scale, *, tm):
    S = zm.shape[0]
    return pl.pallas_call(
        _pool_kernel,
        out_shape=jax.ShapeDtypeStruct((S, D_POOL), BF16),
        grid=(S // tm,),
        in_specs=[
            pl.BlockSpec((tm, D_POOL), lambda i: (i, 0)),
            pl.BlockSpec((HALO, D_POOL), lambda i: (jnp.maximum(i * (tm // HALO) - 1, 0), 0)),
            pl.BlockSpec((len(POOL_WINDOWS), POOL_GROUP, POOL_GROUP), lambda i: (0, 0, 0)),
            pl.BlockSpec((1, D_POOL), lambda i: (0, 0)),
        ],
        out_specs=pl.BlockSpec((tm, D_POOL), lambda i: (i, 0)),
        compiler_params=_params(1),
        name="pool",
    )(zm, zm, pool_w, pool_scale)


I16 = jnp.int16
I16_MIN = -2 ** 15
POOL = 8


def _index_kernel(kidx_ref, qiT_ref, wT_ref, bias_ref, hi_ref, lo_ref, pool_ref, L_ref, n_eq_ref, take_eq_ref,
                  *, T, CK, S, topk):
    qb = pl.program_id(0)
    nck = (qb * T + T + CK - 1) // CK
    q_pos = qb * T + lax.broadcasted_iota(jnp.int32, (1, T), 1)

    def score_chunk(off):
        kc = kidx_ref[pl.ds(off, CK), :]
        acc = None
        for h in range(IDX_HEADS):
            s = jnp.dot(kc, qiT_ref[h * IDX_DIM:(h + 1) * IDX_DIM, :], preferred_element_type=F32)
            t = wT_ref[h:h + 1, :] * jnp.maximum(s, 0.0)
            acc = t if acc is None else acc + t
        bits = lax.bitcast_convert_type(acc, jnp.int32)
        key = bits ^ ((bits >> 31) & jnp.int32(0x7FFFFFFF))
        key_pos = off + lax.broadcasted_iota(jnp.int32, (CK, 1), 0)
        key = jnp.where(key_pos <= q_pos, key, jnp.int32(-2 ** 31))
        hi_ref[pl.ds(off, CK), :] = (key >> 16).astype(I16)
        lo_ref[pl.ds(off, CK), :] = ((key & jnp.int32(0xFFFF)) + I16_MIN).astype(I16)

    def score_body(c, carry):
        for u in range(2):
            score_chunk(pl.multiple_of((2 * c + u) * CK, CK))
        return carry

    lax.fori_loop(0, (nck + 1) // 2, score_body, 0)

    none16 = jnp.full((), I16_MIN, I16)

    def count_ge(ref, cand, trips):
        c16 = cand.astype(I16)

        def count_body(c, acc):
            off = pl.multiple_of(c * CK, CK)
            ge = jnp.where(ref[pl.ds(off, CK), :] >= c16, jnp.ones((), I16), jnp.zeros((), I16))
            ge = ge.reshape(CK // 16, 16, T)
            parts = [ge[r] for r in range(CK // 16)]
            while len(parts) > 1:
                parts = [parts[i] + parts[i + 1] for i in range(0, len(parts), 2)]
            return acc + parts[0]

        acc = lax.fori_loop(0, trips, count_body, jnp.zeros((16, T), I16))
        return acc.astype(jnp.int32).sum(axis=0, keepdims=True)

    def kth_largest(ref, k, trips):
        def bit_body(i, carry):
            v, n_ge_v, n_gt_v = carry
            cand = v + (jnp.int32(1) << (15 - i))
            n_cand = count_ge(ref, cand, trips)
            ok = n_cand >= k
            return jnp.where(ok, cand, v), jnp.where(ok, n_cand, n_ge_v), jnp.where(ok, n_gt_v, n_cand)

        zero = jnp.zeros((1, T), jnp.int32)
        return lax.fori_loop(0, 16, bit_body, (zero + I16_MIN, zero + trips * CK, zero))

    H, n_ge, n_gt = kth_largest(hi_ref, topk, nck)
    H16 = H.astype(I16)
    need = topk - n_gt

    def pool_body(c, carry):
        for g in range(CK // (16 * POOL)):
            a = b = jnp.full((16, T), I16_MIN, I16)
            for r in range(POOL):
                rows = pl.ds(pl.multiple_of(c * CK + (g * POOL + r) * 16, 16), 16)
                x = jnp.where(hi_ref[rows, :] == H16, lo_ref[rows, :], none16)
                up = x > a
                t = jnp.where(up, a, x)
                a = jnp.where(up, x, a)
                b = jnp.where(t > b, t, b)
            base = pl.multiple_of(c * POOL_ROWS + g * 32, 32)
            pool_ref[pl.ds(base, 16), :] = a
            pool_ref[pl.ds(base + 16, 16), :] = b
        return carry

    POOL_ROWS = 2 * CK // POOL
    per_trip = CK // POOL_ROWS
    pool_trips = (nck + per_trip - 1) // per_trip
    lax.fori_loop(0, nck, pool_body, 0)

    def pad_body(c, carry):
        pool_ref[pl.ds(pl.multiple_of(c * POOL_ROWS, POOL_ROWS), POOL_ROWS), :] = jnp.full((POOL_ROWS, T), I16_MIN, I16)
        return carry

    lax.fori_loop(nck, pool_trips * per_trip, pad_body, 0)

    n_tied = n_ge - n_gt
    n_kept = count_ge(pool_ref, jnp.full((1, T), I16_MIN + 1, jnp.int32), pool_trips)
    lost = jnp.sum(jnp.where((n_kept == n_tied) | (H == I16_MIN), 0, 1))

    def second_level(ref, trips):
        L, n_ge_l, n_gt_l = kth_largest(ref, need, trips)
        L_ref[...] = L
        n_eq_ref[...] = n_ge_l - n_gt_l
        take_eq_ref[...] = need - n_gt_l

    @pl.when(lost == 0)
    def _():
        second_level(pool_ref, pool_trips)

    @pl.when(lost != 0)
    def _():
        def mask_body(c, carry):
            off = pl.multiple_of(c * CK, CK)
            lo_ref[pl.ds(off, CK), :] = jnp.where(hi_ref[pl.ds(off, CK), :] == H16, lo_ref[pl.ds(off, CK), :],
                                                  none16)
            return carry

        lax.fori_loop(0, nck, mask_body, 0)
        second_level(lo_ref, nck)

    L16 = L_ref[...].astype(I16)
    take_eq = take_eq_ref[...]
    surplus = jnp.sum(jnp.where((n_eq_ref[...] > take_eq) & (H > I16_MIN), 1, 0))

    def emit(off, sel):
        sel = sel & (hi_ref[pl.ds(off, CK), :] > none16)
        bias_ref[0, pl.ds(off, CK), :] = jnp.where(sel, jnp.zeros((), BF16), jnp.full((), MASK_NEG, BF16))

    @pl.when(surplus == 0)
    def _():
        def emit_body(c, carry):
            off = pl.multiple_of(c * CK, CK)
            hi = hi_ref[pl.ds(off, CK), :]
            emit(off, (hi > H16) | ((hi == H16) & (lo_ref[pl.ds(off, CK), :] >= L16)))
            return carry

        lax.fori_loop(0, nck, emit_body, 0)

    @pl.when(surplus != 0)
    def _():
        tri = (lax.broadcasted_iota(jnp.int32, (CK, CK), 0) >= lax.broadcasted_iota(jnp.int32, (CK, CK), 1))
        tri = jnp.where(tri, 1.0, 0.0).astype(BF16)
        take_f = take_eq.astype(F32)

        def emit_body(c, seen):
            off = pl.multiple_of(c * CK, CK)
            hi = hi_ref[pl.ds(off, CK), :]
            lo = lo_ref[pl.ds(off, CK), :]
            eq = (hi == H16) & (lo == L16) & (hi > none16)
            rank = seen + jnp.dot(tri, jnp.where(eq, jnp.ones((), BF16), jnp.zeros((), BF16)),
                                  preferred_element_type=F32)
            first = jnp.where(rank <= take_f, 1.0, 0.0).astype(BF16) > jnp.zeros((), BF16)
            emit(off, (hi > H16) | ((hi == H16) & (lo > L16)) | (eq & first))
            return rank[CK - 1:CK, :]

        lax.fori_loop(0, nck, emit_body, jnp.zeros((1, T), F32))

    def fill_body(c, carry):
        off = pl.multiple_of(c * CK, CK)
        bias_ref[0, pl.ds(off, CK), :] = jnp.full((CK, T), MASK_NEG, BF16)
        return carry

    lax.fori_loop(nck, S // CK, fill_body, 0)


def _index(kidx, tr, wT, *, T, CK, topk):
    S = kidx.shape[0]
    assert S % (2 * CK) == 0 and CK % T == 0 and (S // CK) % (POOL // 2) == 0
    return pl.pallas_call(
        functools.partial(_index_kernel, T=T, CK=CK, S=S, topk=topk),
        out_shape=jax.ShapeDtypeStruct((S // T, S, T), BF16),
        grid=(S // T,),
        in_specs=[
            pl.BlockSpec((S, IDX_DIM), lambda q: (0, 0)),
            pl.BlockSpec((IDX_HEADS * IDX_DIM, T), lambda q: (2, q)),
            pl.BlockSpec((IDX_HEADS, T), lambda q: (0, q)),
        ],
        out_specs=pl.BlockSpec((1, S, T), lambda q: (q, 0, 0)),
        scratch_shapes=[pltpu.VMEM((S, T), I16), pltpu.VMEM((S, T), I16),
                        pltpu.VMEM((2 * S // POOL, T), I16)] + [pltpu.VMEM((1, T), jnp.int32)] * 3,
        compiler_params=_params(1),
        name="index",
    )(kidx, tr, wT)


def _attn_kernel(qb_ref, kb_ref, kn_ref, qT_ref, vT_ref, bias_ref, o_ref, m_ref, l_ref, acc_ref, *, T, KB,
                 SUB, online):
    i = pl.program_id(0)
    qb = qb_ref[i]
    kb = kb_ref[i]

    @pl.when(kb == 0)
    def _():
        m_ref[...] = jnp.full(m_ref.shape, MASK_NEG, F32)
        l_ref[...] = jnp.zeros(l_ref.shape, F32)
        acc_ref[...] = jnp.zeros(acc_ref.shape, F32)

    items = [(kb0, h) for h in range(N_HEADS) for kb0 in range(0, KB, SUB)]

    def qk(item):
        kb0, h = item
        hs = slice(h * HEAD_DIM, (h + 1) * HEAD_DIM)
        return jnp.dot(kn_ref[kb0:kb0 + SUB, hs], qT_ref[hs, :], preferred_element_type=F32)

    ahead = 4
    pending = [qk(it) for it in items[:ahead]]
    bias = {kb0: bias_ref[0, kb0:kb0 + SUB, :].astype(F32) for kb0 in range(0, KB, SUB)}
    for n, (kb0, h) in enumerate(items):
        hs = slice(h * HEAD_DIM, (h + 1) * HEAD_DIM)
        s = bias[kb0] + pending.pop(0)
        if n + ahead < len(items):
            pending.append(qk(items[n + ahead]))
        if online:
            m_old = m_ref[h]
            m_new = jnp.maximum(m_old, s.max(axis=0, keepdims=True))
            alpha = jnp.exp2(m_old - m_new)
            p = jnp.exp2(s - m_new)
            l_ref[h] = alpha * l_ref[h] + p.sum(axis=0, keepdims=True)
            pv = jnp.dot(vT_ref[hs, kb0:kb0 + SUB], p.astype(BF16), preferred_element_type=F32)
            acc_ref[h] = alpha * acc_ref[h] + pv
            m_ref[h] = m_new
        else:
            p = jnp.exp2(s)
            l_ref[h] += p.sum(axis=0, keepdims=True)
            acc_ref[h] += jnp.dot(vT_ref[hs, kb0:kb0 + SUB], p.astype(BF16), preferred_element_type=F32)

    @pl.when(kb == (qb * T + T - 1) // KB)
    def _():
        for h in range(N_HEADS):
            o_ref[:, h * HEAD_DIM:(h + 1) * HEAD_DIM] = (acc_ref[h] / l_ref[h]).T.astype(BF16)


def _attn(nat, tr, bias, *, T, KB, SUB, online):
    S = nat.shape[0]
    pairs = [(q, k) for q in range(S // T) for k in range((q * T + T - 1) // KB + 1)]
    qb_ids = jnp.asarray(np.array([p[0] for p in pairs], np.int32))
    kb_ids = jnp.asarray(np.array([p[1] for p in pairs], np.int32))
    grid_spec = pltpu.PrefetchScalarGridSpec(
        num_scalar_prefetch=2,
        grid=(len(pairs),),
        in_specs=[
            pl.BlockSpec((KB, D_ATTN), lambda i, qb, kb: (kb[i], 1)),
            pl.BlockSpec((D_ATTN, T), lambda i, qb, kb: (0, qb[i])),
            pl.BlockSpec((D_ATTN, KB), lambda i, qb, kb: (1, kb[i])),
            pl.BlockSpec((1, KB, T), lambda i, qb, kb: (qb[i], kb[i], 0)),
        ],
        out_specs=pl.BlockSpec((T, D_ATTN), lambda i, qb, kb: (qb[i], 0)),
        scratch_shapes=[pltpu.VMEM((N_HEADS, 1, T), F32), pltpu.VMEM((N_HEADS, 1, T), F32),
                        pltpu.VMEM((N_HEADS, HEAD_DIM, T), F32)],
    )
    return pl.pallas_call(
        functools.partial(_attn_kernel, T=T, KB=KB, SUB=SUB, online=online),
        out_shape=jax.ShapeDtypeStruct((S, D_ATTN), BF16),
        grid_spec=grid_spec,
        compiler_params=_params(1),
        name="attn_online" if online else "attn",
    )(qb_ids, kb_ids, nat, tr, tr, bias)


def _outproj_kernel(yp_ref, ya_ref, wp_ref, wa_ref, x_ref, o_ref):
    acc = jnp.dot(yp_ref[...], wp_ref[...], preferred_element_type=F32)
    acc = acc + jnp.dot(ya_ref[...], wa_ref[...], preferred_element_type=F32)
    o_ref[...] = x_ref[...] + acc


def _outproj(yp, ya, w_out, x2, *, tm, tn):
    S = x2.shape[0]
    n_j = D_MODEL // tn
    return pl.pallas_call(
        _outproj_kernel,
        out_shape=jax.ShapeDtypeStruct((S, D_MODEL), F32),
        grid=(S // tm, n_j),
        in_specs=[
            pl.BlockSpec((tm, D_POOL), lambda i, j: (i, 0)),
            pl.BlockSpec((tm, D_ATTN), lambda i, j: (i, 0)),
            pl.BlockSpec((D_POOL, tn), lambda i, j: (0, j)),
            pl.BlockSpec((D_ATTN, tn), lambda i, j: (D_POOL // D_ATTN, j)),
            pl.BlockSpec((tm, tn), lambda i, j: (i, j)),
        ],
        out_specs=pl.BlockSpec((tm, tn), lambda i, j: (i, j)),
        compiler_params=_params(2),
        name="outproj",
    )(yp, ya, w_out, w_out, x2)


def _ffn_kernel(x_ref, halo_ref, g_ref, wg_ref, wv_ref, cwg_ref, cwv_ref, cbg_ref, cbv_ref, wd_ref,
                o_ref, h_ref):
    i = pl.program_id(0)
    f = pl.program_id(1)

    @pl.when(f == 0)
    def _():
        hh = jnp.where(i == 0, 0.0, _rms(halo_ref[...], g_ref[...]))
        h_ref[0:HALO, :] = hh.astype(BF16)
        x = x_ref[...]
        h_ref[HALO:, :] = _rms(x, g_ref[...]).astype(BF16)
        o_ref[...] = x

    h = h_ref[...]

    def conv(w_ref, cw_ref, cb_ref):
        up = jnp.dot(h, w_ref[...], preferred_element_type=F32)
        c = (cw_ref[0:1, :] * pltpu.roll(up, 2, axis=0) + cw_ref[1:2, :] * pltpu.roll(up, 1, axis=0)
             + cw_ref[2:3, :] * up)
        return cb_ref[...] + c[HALO:, :]

    cg = conv(wg_ref, cwg_ref, cbg_ref)
    cv = conv(wv_ref, cwv_ref, cbv_ref)
    act = cg * (1.0 / (1.0 + jnp.exp(-cg))) * cv
    o_ref[...] += jnp.dot(act.astype(BF16), wd_ref[...], preferred_element_type=F32)


def _ffn(x1, g, w_up, conv_w, conv_b, w_down, *, tm, tf):
    S = x1.shape[0]
    n_f = D_FF // tf
    return pl.pallas_call(
        _ffn_kernel,
        out_shape=jax.ShapeDtypeStruct((S, D_MODEL), F32),
        grid=(S // tm, n_f),
        in_specs=[
            pl.BlockSpec((tm, D_MODEL), lambda i, f: (i, 0)),
            pl.BlockSpec((HALO, D_MODEL), lambda i, f: (jnp.maximum(i * (tm // HALO) - 1, 0), 0)),
            pl.BlockSpec((1, D_MODEL), lambda i, f: (0, 0)),
            pl.BlockSpec((D_MODEL, tf), lambda i, f: (0, f)),
            pl.BlockSpec((D_MODEL, tf), lambda i, f: (0, f + n_f)),
            pl.BlockSpec((CONV_WIDTH, tf), lambda i, f: (0, f)),
            pl.BlockSpec((CONV_WIDTH, tf), lambda i, f: (0, f + n_f)),
            pl.BlockSpec((1, tf), lambda i, f: (0, f)),
            pl.BlockSpec((1, tf), lambda i, f: (0, f + n_f)),
            pl.BlockSpec((tf, D_MODEL), lambda i, f: (f, 0)),
        ],
        out_specs=pl.BlockSpec((tm, D_MODEL), lambda i, f: (i, 0)),
        scratch_shapes=[pltpu.VMEM((HALO + tm, D_MODEL), BF16)],
        compiler_params=_params(2),
        name="ffn",
    )(x1, x1, g, w_up, w_up, conv_w, conv_w, conv_b, conv_b, w_down)


def kernel(x, attn_norm_g, w_in, pool_w, pool_scale, q_norm_g, k_norm_g, w_out, ffn_norm_g, w_up,
           conv_w, conv_b, w_down):
    B, S, D = x.shape
    assert B == 1 and D == D_MODEL and w_in.shape == (D_MODEL, D_MAIN + D_TAIL)
    t = TILES
    assert all(S % n == 0 for n in (t.inproj_rows, t.pool_rows, t.outproj_rows, t.ffn_rows, t.attn_keys))
    topk = min(TOPK_MAX, S // 4)
    q_scale = HEAD_DIM ** -0.5 * LOG2E
    x2 = x.reshape(S, D)

    o = D_POOL
    w_q, w_k, w_v = (w_in[:, o + n * D_ATTN:o + (n + 1) * D_ATTN] for n in range(3))
    w_qi = w_in[:, o + 3 * D_ATTN:D_MAIN]
    w_nat = jnp.concatenate([w_in[:, :o], w_k], axis=1).astype(BF16)
    w_trT = jnp.concatenate([w_q, w_v, w_qi], axis=1).T.astype(BF16)
    w_tail = jnp.pad(w_in[:, D_MAIN:], ((0, 0), (0, TAIL_PAD - D_TAIL))).astype(BF16)
    nat, tr, kidx, wT = _inproj(x2, attn_norm_g.reshape(1, D), w_nat, w_trT, w_tail,
                                k_norm_g.reshape(1, HEAD_DIM), q_norm_g.reshape(HEAD_DIM, 1),
                                tm=t.inproj_rows, tn=t.inproj_cols, q_scale=q_scale)

    y_pool = _pool(nat, pool_w.astype(BF16), pool_scale.reshape(1, D_POOL), tm=t.pool_rows)
    bias = _index(kidx, tr, wT, T=t.dsa_queries, CK=t.index_keys, topk=topk)
    score_bound = HEAD_DIM * jnp.max(jnp.abs(q_norm_g)) * jnp.max(jnp.abs(k_norm_g)) * q_scale
    attn = functools.partial(_attn, T=t.dsa_queries, KB=t.attn_keys, SUB=t.attn_sub_keys)
    y_attn = lax.cond(score_bound < MAX_UNSHIFTED_LOG2,
                      functools.partial(attn, online=False), functools.partial(attn, online=True),
                      nat, tr, bias)

    x1 = _outproj(y_pool, y_attn, w_out.astype(BF16), x2, tm=t.outproj_rows, tn=t.outproj_cols)
    out = _ffn(x1, ffn_norm_g.reshape(1, D), w_up.astype(BF16), conv_w, conv_b.reshape(1, 2 * D_FF),
               w_down.astype(BF16), tm=t.ffn_rows, tf=t.ffn_cols)
    return out.reshape(B, S, D)
```

```python
import functools
from typing import NamedTuple

import numpy as np
import jax
import jax.numpy as jnp
from jax import lax
from jax.experimental import pallas as pl
from jax.experimental.pallas import tpu as pltpu

D_MODEL = 2048
D_POOL = 1024
POOL_WINDOWS = (2, 4, 8, 16)
POOL_GROUP = D_POOL // len(POOL_WINDOWS)
D_ATTN = 1024
HEAD_DIM = 128
N_HEADS = D_ATTN // HEAD_DIM
IDX_HEADS = 16
IDX_DIM = 64
TOPK_MAX = 256
D_FF = 5632
CONV_WIDTH = 3
EPS = 1e-6

D_MAIN = D_POOL + 3 * D_ATTN + IDX_HEADS * IDX_DIM
D_TAIL = IDX_DIM + IDX_HEADS
TAIL_PAD = 128
HALO = 16
MASK_NEG = -1e30
LOG2E = 1.4426950408889634
MAX_UNSHIFTED_LOG2 = 60.0
V7X_VMEM_BYTES = 64 * 1024 * 1024
VMEM_LIMIT_BYTES = V7X_VMEM_BYTES * 7 // 8


class _Tiles(NamedTuple):
    inproj_rows: int = 1024
    inproj_cols: int = 1024
    pool_rows: int = 1024
    dsa_queries: int = 256
    attn_queries: int = 512
    index_keys: int = 512
    attn_keys: int = 1024
    attn_sub_keys: int = 512
    outproj_rows: int = 512
    outproj_cols: int = 2048
    ffn_rows: int = 1024
    ffn_cols: int = 512


TILES = _Tiles()

F32 = jnp.float32
BF16 = jnp.bfloat16


def _params(n_axes):
    return pltpu.CompilerParams(dimension_semantics=("arbitrary",) * n_axes,
                                vmem_limit_bytes=VMEM_LIMIT_BYTES)


def _rms(xf, g):
    return xf * lax.rsqrt(jnp.mean(xf * xf, axis=-1, keepdims=True) + EPS) * g


_NT_DIMS = (((1,), (1,)), ((), ()))


def _inproj_kernel(x_ref, g_ref, wn_ref, wt_ref, wtail_ref, kg_ref, qg_ref, nat_ref, tr_ref, kidx_ref,
                   wT_ref, h_ref, *, tn, q_scale):
    j = pl.program_id(1)
    n_nat = (D_POOL + D_ATTN) // tn

    @pl.when(j == 0)
    def _():
        h = _rms(x_ref[...], g_ref[...]).astype(BF16)
        h_ref[...] = h
        tail = jnp.dot(h, wtail_ref[...], preferred_element_type=F32)
        kidx_ref[...] = tail[:, :IDX_DIM].astype(BF16)
        wT_ref[...] = tail.T[IDX_DIM:D_TAIL, :] * (IDX_HEADS ** -0.5) * (IDX_DIM ** -0.5)

    @pl.when(j < n_nat)
    def _():
        z = jnp.dot(h_ref[...], wn_ref[...], preferred_element_type=F32)

        @pl.when(j < D_POOL // tn)
        def _():
            nat_ref[...] = z.astype(BF16)

        @pl.when(j >= D_POOL // tn)
        def _():
            for c in range(tn // HEAD_DIM):
                sl = slice(c * HEAD_DIM, (c + 1) * HEAD_DIM)
                nat_ref[:, sl] = _rms(z[:, sl], kg_ref[...]).astype(BF16)

    @pl.when(j >= n_nat)
    def _():
        zT = lax.dot_general(wt_ref[...], h_ref[...], _NT_DIMS, preferred_element_type=F32)

        @pl.when(j < n_nat + D_ATTN // tn)
        def _():
            for c in range(tn // HEAD_DIM):
                sl = slice(c * HEAD_DIM, (c + 1) * HEAD_DIM)
                zc = zT[sl, :]
                inv = lax.rsqrt(jnp.mean(zc * zc, axis=0, keepdims=True) + EPS)
                tr_ref[sl, :] = (zc * inv * (qg_ref[...] * q_scale)).astype(BF16)

        @pl.when(j >= n_nat + D_ATTN // tn)
        def _():
            tr_ref[...] = zT.astype(BF16)


def _inproj(x2, g, w_nat, w_trT, w_tail, kg, qg_col, *, tm, tn, q_scale):
    S = x2.shape[0]
    n_nat, n_tr = w_nat.shape[1] // tn, w_trT.shape[0] // tn
    return pl.pallas_call(
        functools.partial(_inproj_kernel, tn=tn, q_scale=q_scale),
        out_shape=(jax.ShapeDtypeStruct((S, w_nat.shape[1]), BF16),
                   jax.ShapeDtypeStruct((w_trT.shape[0], S), BF16),
                   jax.ShapeDtypeStruct((S, IDX_DIM), BF16),
                   jax.ShapeDtypeStruct((IDX_HEADS, S), F32)),
        grid=(S // tm, n_nat + n_tr),
        in_specs=[
            pl.BlockSpec((tm, D_MODEL), lambda i, j: (i, 0)),
            pl.BlockSpec((1, D_MODEL), lambda i, j: (0, 0)),
            pl.BlockSpec((D_MODEL, tn), lambda i, j: (0, jnp.minimum(j, n_nat - 1))),
            pl.BlockSpec((tn, D_MODEL), lambda i, j: (jnp.maximum(j - n_nat, 0), 0)),
            pl.BlockSpec((D_MODEL, TAIL_PAD), lambda i, j: (0, 0)),
            pl.BlockSpec((1, HEAD_DIM), lambda i, j: (0, 0)),
            pl.BlockSpec((HEAD_DIM, 1), lambda i, j: (0, 0)),
        ],
        out_specs=(pl.BlockSpec((tm, tn), lambda i, j: (i, jnp.minimum(j, n_nat - 1))),
                   pl.BlockSpec((tn, tm), lambda i, j: (jnp.maximum(j - n_nat, 0), i)),
                   pl.BlockSpec((tm, IDX_DIM), lambda i, j: (i, 0)),
                   pl.BlockSpec((IDX_HEADS, tm), lambda i, j: (0, i))),
        scratch_shapes=[pltpu.VMEM((tm, D_MODEL), BF16)],
        compiler_params=_params(2),
        name="inproj",
    )(x2, g, w_nat, w_trT, w_tail, kg, qg_col)


def _pool_kernel(u_ref, halo_ref, pw_ref, ps_ref, o_ref):
    i = pl.program_id(0)
    tm = u_ref.shape[0]
    u = u_ref[...].astype(F32)
    halo = jnp.where(i == 0, 0.0, halo_ref[...].astype(F32))
    ext = jnp.concatenate([halo, u], axis=0)
    t = i * tm + lax.broadcasted_iota(jnp.int32, (tm, 1), 0)
    for gi, w in enumerate(POOL_WINDOWS):
        sl = slice(gi * POOL_GROUP, (gi + 1) * POOL_GROUP)
        s = ext[:, sl]
        step = 1
        while step < w:
            s = s + pltpu.roll(s, step, axis=0)
            step *= 2
        cnt = jnp.minimum(t + 1, w).astype(F32)
        d = s[HALO:, :] / cnt - u[:, sl]
        y = jnp.dot(d.astype(BF16), pw_ref[gi], preferred_element_type=F32)
        o_ref[:, sl] = (y * ps_ref[:, sl]).astype(BF16)


def _pool(zm, pool_w, pool_scale, *, tm):
    S = zm.shape[0]
    return pl.pallas_call(
        _pool_kernel,
        out_shape=jax.ShapeDtypeStruct((S, D_POOL), BF16),
        grid=(S // tm,),
        in_specs=[
            pl.BlockSpec((tm, D_POOL), lambda i: (i, 0)),
            pl.BlockSpec((HALO, D_POOL), lambda i: (jnp.maximum(i * (tm // HALO) - 1, 0), 0)),
            pl.BlockSpec((len(POOL_WINDOWS), POOL_GROUP, POOL_GROUP), lambda i: (0, 0, 0)),
            pl.BlockSpec((1, D_POOL), lambda i: (0, 0)),
        ],
        out_specs=pl.BlockSpec((tm, D_POOL), lambda i: (i, 0)),
        compiler_params=_params(1),
        name="pool",
    )(zm, zm, pool_w, pool_scale)


I16 = jnp.int16
I16_MIN = -2 ** 15
POOL = 8


def _index_kernel(kidx_ref, qiT_ref, wT_ref, keep_ref, hi_ref, lo_ref, pool_ref, L_ref, n_eq_ref, take_eq_ref,
                  *, T, CK, S, topk):
    qb = pl.program_id(0)
    nck = (qb * T + T + CK - 1) // CK
    q_pos = qb * T + lax.broadcasted_iota(jnp.int32, (1, T), 1)

    def score_chunk(off):
        kc = kidx_ref[pl.ds(off, CK), :]
        acc = None
        for h in range(IDX_HEADS):
            s = jnp.dot(kc, qiT_ref[h * IDX_DIM:(h + 1) * IDX_DIM, :], preferred_element_type=F32)
            t = wT_ref[h:h + 1, :] * jnp.maximum(s, 0.0)
            acc = t if acc is None else acc + t
        bits = lax.bitcast_convert_type(acc, jnp.int32)
        key = bits ^ ((bits >> 31) & jnp.int32(0x7FFFFFFF))
        key_pos = off + lax.broadcasted_iota(jnp.int32, (CK, 1), 0)
        key = jnp.where(key_pos <= q_pos, key, jnp.int32(-2 ** 31))
        hi_ref[pl.ds(off, CK), :] = (key >> 16).astype(I16)
        lo_ref[pl.ds(off, CK), :] = ((key & jnp.int32(0xFFFF)) + I16_MIN).astype(I16)

    def score_body(c, carry):
        for u in range(2):
            score_chunk(pl.multiple_of((2 * c + u) * CK, CK))
        return carry

    lax.fori_loop(0, (nck + 1) // 2, score_body, 0)

    none16 = jnp.full((), I16_MIN, I16)

    def count_ge(ref, cand, trips):
        c16 = cand.astype(I16)

        def count_body(c, acc):
            off = pl.multiple_of(c * CK, CK)
            ge = jnp.where(ref[pl.ds(off, CK), :] >= c16, jnp.ones((), I16), jnp.zeros((), I16))
            ge = ge.reshape(CK // 16, 16, T)
            parts = [ge[r] for r in range(CK // 16)]
            while len(parts) > 1:
                parts = [parts[i] + parts[i + 1] for i in range(0, len(parts), 2)]
            return acc + parts[0]

        acc = lax.fori_loop(0, trips, count_body, jnp.zeros((16, T), I16))
        return acc.astype(jnp.int32).sum(axis=0, keepdims=True)

    def kth_largest(ref, k, trips):
        def bit_body(i, carry):
            v, n_ge_v, n_gt_v = carry
            cand = v + (jnp.int32(1) << (15 - i))
            n_cand = count_ge(ref, cand, trips)
            ok = n_cand >= k
            return jnp.where(ok, cand, v), jnp.where(ok, n_cand, n_ge_v), jnp.where(ok, n_gt_v, n_cand)

        zero = jnp.zeros((1, T), jnp.int32)
        return lax.fori_loop(0, 16, bit_body, (zero + I16_MIN, zero + trips * CK, zero))

    H, n_ge, n_gt = kth_largest(hi_ref, topk, nck)
    H16 = H.astype(I16)
    need = topk - n_gt

    def pool_body(c, carry):
        for g in range(CK // (16 * POOL)):
            a = b = jnp.full((16, T), I16_MIN, I16)
            for r in range(POOL):
                rows = pl.ds(pl.multiple_of(c * CK + (g * POOL + r) * 16, 16), 16)
                x = jnp.where(hi_ref[rows, :] == H16, lo_ref[rows, :], none16)
                up = x > a
                t = jnp.where(up, a, x)
                a = jnp.where(up, x, a)
                b = jnp.where(t > b, t, b)
            base = pl.multiple_of(c * POOL_ROWS + g * 32, 32)
            pool_ref[pl.ds(base, 16), :] = a
            pool_ref[pl.ds(base + 16, 16), :] = b
        return carry

    POOL_ROWS = 2 * CK // POOL
    per_trip = CK // POOL_ROWS
    pool_trips = (nck + per_trip - 1) // per_trip
    lax.fori_loop(0, nck, pool_body, 0)

    def pad_body(c, carry):
        pool_ref[pl.ds(pl.multiple_of(c * POOL_ROWS, POOL_ROWS), POOL_ROWS), :] = jnp.full((POOL_ROWS, T), I16_MIN, I16)
        return carry

    lax.fori_loop(nck, pool_trips * per_trip, pad_body, 0)

    n_tied = n_ge - n_gt
    n_kept = count_ge(pool_ref, jnp.full((1, T), I16_MIN + 1, jnp.int32), pool_trips)
    lost = jnp.sum(jnp.where((n_kept == n_tied) | (H == I16_MIN), 0, 1))

    def second_level(ref, trips):
        L, n_ge_l, n_gt_l = kth_largest(ref, need, trips)
        L_ref[...] = L
        n_eq_ref[...] = n_ge_l - n_gt_l
        take_eq_ref[...] = need - n_gt_l

    @pl.when(lost == 0)
    def _():
        second_level(pool_ref, pool_trips)

    @pl.when(lost != 0)
    def _():
        def mask_body(c, carry):
            off = pl.multiple_of(c * CK, CK)
            lo_ref[pl.ds(off, CK), :] = jnp.where(hi_ref[pl.ds(off, CK), :] == H16, lo_ref[pl.ds(off, CK), :],
                                                  none16)
            return carry

        lax.fori_loop(0, nck, mask_body, 0)
        second_level(lo_ref, nck)

    L16 = L_ref[...].astype(I16)
    take_eq = take_eq_ref[...]
    surplus = jnp.sum(jnp.where((n_eq_ref[...] > take_eq) & (H > I16_MIN), 1, 0))

    def emit(off, sel):
        sel = sel & (hi_ref[pl.ds(off, CK), :] > none16)
        keep_ref[0, pl.ds(off, CK), :] = jnp.where(sel, jnp.ones((), BF16), jnp.zeros((), BF16))

    @pl.when(surplus == 0)
    def _():
        def emit_body(c, carry):
            off = pl.multiple_of(c * CK, CK)
            hi = hi_ref[pl.ds(off, CK), :]
            emit(off, (hi > H16) | ((hi == H16) & (lo_ref[pl.ds(off, CK), :] >= L16)))
            return carry

        lax.fori_loop(0, nck, emit_body, 0)

    @pl.when(surplus != 0)
    def _():
        tri = (lax.broadcasted_iota(jnp.int32, (CK, CK), 0) >= lax.broadcasted_iota(jnp.int32, (CK, CK), 1))
        tri = jnp.where(tri, 1.0, 0.0).astype(BF16)
        take_f = take_eq.astype(F32)

        def emit_body(c, seen):
            off = pl.multiple_of(c * CK, CK)
            hi = hi_ref[pl.ds(off, CK), :]
            lo = lo_ref[pl.ds(off, CK), :]
            eq = (hi == H16) & (lo == L16) & (hi > none16)
            rank = seen + jnp.dot(tri, jnp.where(eq, jnp.ones((), BF16), jnp.zeros((), BF16)),
                                  preferred_element_type=F32)
            first = jnp.where(rank <= take_f, 1.0, 0.0).astype(BF16) > jnp.zeros((), BF16)
            emit(off, (hi > H16) | ((hi == H16) & (lo > L16)) | (eq & first))
            return rank[CK - 1:CK, :]

        lax.fori_loop(0, nck, emit_body, jnp.zeros((1, T), F32))

    def fill_body(c, carry):
        off = pl.multiple_of(c * CK, CK)
        keep_ref[0, pl.ds(off, CK), :] = jnp.zeros((CK, T), BF16)
        return carry

    lax.fori_loop(nck, S // CK, fill_body, 0)


def _index(kidx, tr, wT, *, T, CK, topk):
    S = kidx.shape[0]
    assert S % (2 * CK) == 0 and CK % T == 0 and (S // CK) % (POOL // 2) == 0
    return pl.pallas_call(
        functools.partial(_index_kernel, T=T, CK=CK, S=S, topk=topk),
        out_shape=jax.ShapeDtypeStruct((S // T, S, T), BF16),
        grid=(S // T,),
        in_specs=[
            pl.BlockSpec((S, IDX_DIM), lambda q: (0, 0)),
            pl.BlockSpec((IDX_HEADS * IDX_DIM, T), lambda q: (2, q)),
            pl.BlockSpec((IDX_HEADS, T), lambda q: (0, q)),
        ],
        out_specs=pl.BlockSpec((1, S, T), lambda q: (q, 0, 0)),
        scratch_shapes=[pltpu.VMEM((S, T), I16), pltpu.VMEM((S, T), I16),
                        pltpu.VMEM((2 * S // POOL, T), I16)] + [pltpu.VMEM((1, T), jnp.int32)] * 3,
        compiler_params=_params(1),
        name="index",
    )(kidx, tr, wT)


def _attn_kernel(qb_ref, kb_ref, kn_ref, qT_ref, vT_ref, *rest, T, KB, SUB, online):
    *keep_refs, o_ref, m_ref, l_ref, acc_ref = rest
    i = pl.program_id(0)
    qb = qb_ref[i]
    kb = kb_ref[i]
    TQ = T * len(keep_refs)

    @pl.when(kb == 0)
    def _():
        m_ref[...] = jnp.full(m_ref.shape, MASK_NEG, F32)
        l_ref[...] = jnp.zeros(l_ref.shape, F32)
        acc_ref[...] = jnp.zeros(acc_ref.shape, F32)

    items = [(kb0, h) for h in range(N_HEADS) for kb0 in range(0, KB, SUB)]

    def qk(item):
        kb0, h = item
        hs = slice(h * HEAD_DIM, (h + 1) * HEAD_DIM)
        return jnp.dot(kn_ref[kb0:kb0 + SUB, hs], qT_ref[hs, :], preferred_element_type=F32)

    ahead = 4
    pending = [qk(it) for it in items[:ahead]]
    def additive(kb0):
        keep = jnp.concatenate([r[0, kb0:kb0 + SUB, :] for r in keep_refs], axis=1)
        return jnp.where(keep > jnp.zeros((), BF16), jnp.zeros((), BF16), jnp.full((), MASK_NEG, BF16)).astype(F32)

    bias = {kb0: additive(kb0) for kb0 in range(0, KB, SUB)}
    for n, (kb0, h) in enumerate(items):
        hs = slice(h * HEAD_DIM, (h + 1) * HEAD_DIM)
        s = bias[kb0] + pending.pop(0)
        if n + ahead < len(items):
            pending.append(qk(items[n + ahead]))
        if online:
            m_old = m_ref[h]
            m_new = jnp.maximum(m_old, s.max(axis=0, keepdims=True))
            alpha = jnp.exp2(m_old - m_new)
            p = jnp.exp2(s - m_new)
            l_ref[h] = alpha * l_ref[h] + p.sum(axis=0, keepdims=True)
            pv = jnp.dot(vT_ref[hs, kb0:kb0 + SUB], p.astype(BF16), preferred_element_type=F32)
            acc_ref[h] = alpha * acc_ref[h] + pv
            m_ref[h] = m_new
        else:
            p = jnp.exp2(s)
            l_ref[h] += p.sum(axis=0, keepdims=True)
            acc_ref[h] += jnp.dot(vT_ref[hs, kb0:kb0 + SUB], p.astype(BF16), preferred_element_type=F32)

    @pl.when(kb == (qb * TQ + TQ - 1) // KB)
    def _():
        for h in range(N_HEADS):
            o_ref[:, h * HEAD_DIM:(h + 1) * HEAD_DIM] = (acc_ref[h] / l_ref[h]).T.astype(BF16)


def _attn(nat, tr, keep, *, T, TQ, KB, SUB, online):
    S = nat.shape[0]
    groups = TQ // T
    pairs = [(q, k) for q in range(S // TQ) for k in range((q * TQ + TQ - 1) // KB + 1)]
    qb_ids = jnp.asarray(np.array([p[0] for p in pairs], np.int32))
    kb_ids = jnp.asarray(np.array([p[1] for p in pairs], np.int32))
    grid_spec = pltpu.PrefetchScalarGridSpec(
        num_scalar_prefetch=2,
        grid=(len(pairs),),
        in_specs=[
            pl.BlockSpec((KB, D_ATTN), lambda i, qb, kb: (kb[i], 1)),
            pl.BlockSpec((D_ATTN, TQ), lambda i, qb, kb: (0, qb[i])),
            pl.BlockSpec((D_ATTN, KB), lambda i, qb, kb: (1, kb[i])),
        ] + [pl.BlockSpec((1, KB, T), lambda i, qb, kb, g=g: (qb[i] * groups + g, kb[i], 0)) for g in range(groups)],
        out_specs=pl.BlockSpec((TQ, D_ATTN), lambda i, qb, kb: (qb[i], 0)),
        scratch_shapes=[pltpu.VMEM((N_HEADS, 1, TQ), F32), pltpu.VMEM((N_HEADS, 1, TQ), F32),
                        pltpu.VMEM((N_HEADS, HEAD_DIM, TQ), F32)],
    )
    return pl.pallas_call(
        functools.partial(_attn_kernel, T=T, KB=KB, SUB=SUB, online=online),
        out_shape=jax.ShapeDtypeStruct((S, D_ATTN), BF16),
        grid_spec=grid_spec,
        compiler_params=_params(1),
        name="attn_online" if online else "attn",
    )(qb_ids, kb_ids, nat, tr, tr, *([keep] * groups))


def _outproj_kernel(yp_ref, ya_ref, wp_ref, wa_ref, x_ref, o_ref):
    acc = jnp.dot(yp_ref[...], wp_ref[...], preferred_element_type=F32)
    acc = acc + jnp.dot(ya_ref[...], wa_ref[...], preferred_element_type=F32)
    o_ref[...] = x_ref[...] + acc


def _outproj(yp, ya, w_out, x2, *, tm, tn):
    S = x2.shape[0]
    n_j = D_MODEL // tn
    return pl.pallas_call(
        _outproj_kernel,
        out_shape=jax.ShapeDtypeStruct((S, D_MODEL), F32),
        grid=(S // tm, n_j),
        in_specs=[
            pl.BlockSpec((tm, D_POOL), lambda i, j: (i, 0)),
            pl.BlockSpec((tm, D_ATTN), lambda i, j: (i, 0)),
            pl.BlockSpec((D_POOL, tn), lambda i, j: (0, j)),
            pl.BlockSpec((D_ATTN, tn), lambda i, j: (D_POOL // D_ATTN, j)),
            pl.BlockSpec((tm, tn), lambda i, j: (i, j)),
        ],
        out_specs=pl.BlockSpec((tm, tn), lambda i, j: (i, j)),
        compiler_params=_params(2),
        name="outproj",
    )(yp, ya, w_out, w_out, x2)


def _ffn_kernel(x_ref, halo_ref, g_ref, wg_ref, wv_ref, cwg_ref, cwv_ref, cbg_ref, cbv_ref, wd_ref,
                o_ref, h_ref):
    i = pl.program_id(0)
    f = pl.program_id(1)

    @pl.when(f == 0)
    def _():
        hh = jnp.where(i == 0, 0.0, _rms(halo_ref[...], g_ref[...]))
        h_ref[0:HALO, :] = hh.astype(BF16)
        x = x_ref[...]
        h_ref[HALO:, :] = _rms(x, g_ref[...]).astype(BF16)
        o_ref[...] = x

    h = h_ref[...]

    def conv(w_ref, cw_ref, cb_ref):
        up = jnp.dot(h, w_ref[...], preferred_element_type=F32)
        c = (cw_ref[0:1, :] * pltpu.roll(up, 2, axis=0) + cw_ref[1:2, :] * pltpu.roll(up, 1, axis=0)
             + cw_ref[2:3, :] * up)
        return cb_ref[...] + c[HALO:, :]

    cg = conv(wg_ref, cwg_ref, cbg_ref)
    cv = conv(wv_ref, cwv_ref, cbv_ref)
    act = cg * (1.0 / (1.0 + jnp.exp(-cg))) * cv
    o_ref[...] += jnp.dot(act.astype(BF16), wd_ref[...], preferred_element_type=F32)


def _ffn(x1, g, w_up, conv_w, conv_b, w_down, *, tm, tf):
    S = x1.shape[0]
    n_f = D_FF // tf
    return pl.pallas_call(
        _ffn_kernel,
        out_shape=jax.ShapeDtypeStruct((S, D_MODEL), F32),
        grid=(S // tm, n_f),
        in_specs=[
            pl.BlockSpec((tm, D_MODEL), lambda i, f: (i, 0)),
            pl.BlockSpec((HALO, D_MODEL), lambda i, f: (jnp.maximum(i * (tm // HALO) - 1, 0), 0)),
            pl.BlockSpec((1, D_MODEL), lambda i, f: (0, 0)),
            pl.BlockSpec((D_MODEL, tf), lambda i, f: (0, f)),
            pl.BlockSpec((D_MODEL, tf), lambda i, f: (0, f + n_f)),
            pl.BlockSpec((CONV_WIDTH, tf), lambda i, f: (0, f)),
            pl.BlockSpec((CONV_WIDTH, tf), lambda i, f: (0, f + n_f)),
            pl.BlockSpec((1, tf), lambda i, f: (0, f)),
            pl.BlockSpec((1, tf), lambda i, f: (0, f + n_f)),
            pl.BlockSpec((tf, D_MODEL), lambda i, f: (f, 0)),
        ],
        out_specs=pl.BlockSpec((tm, D_MODEL), lambda i, f: (i, 0)),
        scratch_shapes=[pltpu.VMEM((HALO + tm, D_MODEL), BF16)],
        compiler_params=_params(2),
        name="ffn",
    )(x1, x1, g, w_up, w_up, conv_w, conv_w, conv_b, conv_b, w_down)


def kernel(x, attn_norm_g, w_in, pool_w, pool_scale, q_norm_g, k_norm_g, w_out, ffn_norm_g, w_up,
           conv_w, conv_b, w_down):
    B, S, D = x.shape
    assert B == 1 and D == D_MODEL and w_in.shape == (D_MODEL, D_MAIN + D_TAIL)
    t = TILES
    assert all(S % n == 0 for n in (t.inproj_rows, t.pool_rows, t.outproj_rows, t.ffn_rows, t.attn_keys))
    topk = min(TOPK_MAX, S // 4)
    q_scale = HEAD_DIM ** -0.5 * LOG2E
    x2 = x.reshape(S, D)

    o = D_POOL
    w_q, w_k, w_v = (w_in[:, o + n * D_ATTN:o + (n + 1) * D_ATTN] for n in range(3))
    w_qi = w_in[:, o + 3 * D_ATTN:D_MAIN]
    w_nat = jnp.concatenate([w_in[:, :o], w_k], axis=1).astype(BF16)
    w_trT = jnp.concatenate([w_q, w_v, w_qi], axis=1).T.astype(BF16)
    w_tail = jnp.pad(w_in[:, D_MAIN:], ((0, 0), (0, TAIL_PAD - D_TAIL))).astype(BF16)
    nat, tr, kidx, wT = _inproj(x2, attn_norm_g.reshape(1, D), w_nat, w_trT, w_tail,
                                k_norm_g.reshape(1, HEAD_DIM), q_norm_g.reshape(HEAD_DIM, 1),
                                tm=t.inproj_rows, tn=t.inproj_cols, q_scale=q_scale)

    y_pool = _pool(nat, pool_w.astype(BF16), pool_scale.reshape(1, D_POOL), tm=t.pool_rows)
    keep = _index(kidx, tr, wT, T=t.dsa_queries, CK=t.index_keys, topk=topk)
    score_bound = HEAD_DIM * jnp.max(jnp.abs(q_norm_g)) * jnp.max(jnp.abs(k_norm_g)) * q_scale
    attn = functools.partial(_attn, T=t.dsa_queries, TQ=t.attn_queries, KB=t.attn_keys, SUB=t.attn_sub_keys)
    y_attn = lax.cond(score_bound < MAX_UNSHIFTED_LOG2,
                      functools.partial(attn, online=False), functools.partial(attn, online=True),
                      nat, tr, keep)

    x1 = _outproj(y_pool, y_attn, w_out.astype(BF16), x2, tm=t.outproj_rows, tn=t.outproj_cols)
    out = _ffn(x1, ffn_norm_g.reshape(1, D), w_up.astype(BF16), conv_w, conv_b.reshape(1, 2 * D_FF),
               w_down.astype(BF16), tm=t.ffn_rows, tf=t.ffn_cols)
    return out.reshape(B, S, D)
```

```python
import functools
from typing import NamedTuple

import numpy as np
import jax
import jax.numpy as jnp
from jax import lax
from jax.experimental import pallas as pl
from jax.experimental.pallas import tpu as pltpu

D_MODEL = 2048
D_POOL = 1024
POOL_WINDOWS = (2, 4, 8, 16)
POOL_GROUP = D_POOL // len(POOL_WINDOWS)
D_ATTN = 1024
HEAD_DIM = 128
N_HEADS = D_ATTN // HEAD_DIM
IDX_HEADS = 16
IDX_DIM = 64
TOPK_MAX = 256
D_FF = 5632
CONV_WIDTH = 3
EPS = 1e-6

D_MAIN = D_POOL + 3 * D_ATTN + IDX_HEADS * IDX_DIM
D_TAIL = IDX_DIM + IDX_HEADS
TAIL_PAD = 128
HALO = 16
MASK_NEG = -1e30
LOG2E = 1.4426950408889634
MAX_UNSHIFTED_LOG2 = 60.0
V7X_VMEM_BYTES = 64 * 1024 * 1024
VMEM_LIMIT_BYTES = V7X_VMEM_BYTES * 7 // 8


class _Tiles(NamedTuple):
    inproj_rows: int = 1024
    inproj_cols: int = 1024
    pool_rows: int = 1024
    dsa_queries: int = 256
    attn_queries: int = 512
    index_keys: int = 512
    attn_keys: int = 1024
    attn_sub_keys: int = 512
    outproj_rows: int = 512
    outproj_cols: int = 2048
    ffn_rows: int = 1024
    ffn_cols: int = 512


TILES = _Tiles()

F32 = jnp.float32
BF16 = jnp.bfloat16


def _params(n_axes):
    return pltpu.CompilerParams(dimension_semantics=("arbitrary",) * n_axes,
                                vmem_limit_bytes=VMEM_LIMIT_BYTES)


def _rms(xf, g):
    return xf * lax.rsqrt(jnp.mean(xf * xf, axis=-1, keepdims=True) + EPS) * g


_TN_DIMS = (((0,), (1,)), ((), ()))


def _inproj_kernel(x_ref, g_ref, wn_ref, wt_ref, wtail_ref, kg_ref, qg_ref, nat_ref, tr_ref, kidx_ref,
                   wT_ref, h_ref, *, tn, q_scale):
    j = pl.program_id(1)
    n_nat = (D_POOL + D_ATTN) // tn

    @pl.when(j == 0)
    def _():
        h = _rms(x_ref[...], g_ref[...]).astype(BF16)
        h_ref[...] = h
        tail = jnp.dot(h, wtail_ref[...], preferred_element_type=F32)
        kidx_ref[...] = tail[:, :IDX_DIM].astype(BF16)
        wT_ref[...] = tail.T[IDX_DIM:D_TAIL, :] * (IDX_HEADS ** -0.5) * (IDX_DIM ** -0.5)

    @pl.when(j < n_nat)
    def _():
        z = jnp.dot(h_ref[...], wn_ref[...], preferred_element_type=F32)

        @pl.when(j < D_POOL // tn)
        def _():
            nat_ref[...] = z.astype(BF16)

        @pl.when(j >= D_POOL // tn)
        def _():
            for c in range(tn // HEAD_DIM):
                sl = slice(c * HEAD_DIM, (c + 1) * HEAD_DIM)
                nat_ref[:, sl] = _rms(z[:, sl], kg_ref[...]).astype(BF16)

    @pl.when(j >= n_nat)
    def _():
        zT = lax.dot_general(wt_ref[...], h_ref[...], _TN_DIMS, preferred_element_type=F32)

        @pl.when(j < n_nat + D_ATTN // tn)
        def _():
            for c in range(tn // HEAD_DIM):
                sl = slice(c * HEAD_DIM, (c + 1) * HEAD_DIM)
                zc = zT[sl, :]
                inv = lax.rsqrt(jnp.mean(zc * zc, axis=0, keepdims=True) + EPS)
                tr_ref[sl, :] = (zc * inv * (qg_ref[...] * q_scale)).astype(BF16)

        @pl.when(j >= n_nat + D_ATTN // tn)
        def _():
            tr_ref[...] = zT.astype(BF16)


def _inproj(x2, g, w_nat, w_tr, w_tail, kg, qg_col, *, tm, tn, q_scale):
    S = x2.shape[0]
    n_nat, n_tr = w_nat.shape[1] // tn, w_tr.shape[1] // tn
    return pl.pallas_call(
        functools.partial(_inproj_kernel, tn=tn, q_scale=q_scale),
        out_shape=(jax.ShapeDtypeStruct((S, w_nat.shape[1]), BF16),
                   jax.ShapeDtypeStruct((w_tr.shape[1], S), BF16),
                   jax.ShapeDtypeStruct((S, IDX_DIM), BF16),
                   jax.ShapeDtypeStruct((IDX_HEADS, S), F32)),
        grid=(S // tm, n_nat + n_tr),
        in_specs=[
            pl.BlockSpec((tm, D_MODEL), lambda i, j: (i, 0)),
            pl.BlockSpec((1, D_MODEL), lambda i, j: (0, 0)),
            pl.BlockSpec((D_MODEL, tn), lambda i, j: (0, jnp.minimum(j, n_nat - 1))),
            pl.BlockSpec((D_MODEL, tn), lambda i, j: (0, jnp.maximum(j - n_nat, 0))),
            pl.BlockSpec((D_MODEL, TAIL_PAD), lambda i, j: (0, 0)),
            pl.BlockSpec((1, HEAD_DIM), lambda i, j: (0, 0)),
            pl.BlockSpec((HEAD_DIM, 1), lambda i, j: (0, 0)),
        ],
        out_specs=(pl.BlockSpec((tm, tn), lambda i, j: (i, jnp.minimum(j, n_nat - 1))),
                   pl.BlockSpec((tn, tm), lambda i, j: (jnp.maximum(j - n_nat, 0), i)),
                   pl.BlockSpec((tm, IDX_DIM), lambda i, j: (i, 0)),
                   pl.BlockSpec((IDX_HEADS, tm), lambda i, j: (0, i))),
        scratch_shapes=[pltpu.VMEM((tm, D_MODEL), BF16)],
        compiler_params=_params(2),
        name="inproj",
    )(x2, g, w_nat, w_tr, w_tail, kg, qg_col)


def _pool_kernel(u_ref, halo_ref, pw_ref, ps_ref, o_ref):
    i = pl.program_id(0)
    tm = u_ref.shape[0]
    u = u_ref[...].astype(F32)
    halo = jnp.where(i == 0, 0.0, halo_ref[...].astype(F32))
    ext = jnp.concatenate([halo, u], axis=0)
    t = i * tm + lax.broadcasted_iota(jnp.int32, (tm, 1), 0)
    for gi, w in enumerate(POOL_WINDOWS):
        sl = slice(gi * POOL_GROUP, (gi + 1) * POOL_GROUP)
        s = ext[:, sl]
        step = 1
        while step < w:
            s = s + pltpu.roll(s, step, axis=0)
            step *= 2
        cnt = jnp.minimum(t + 1, w).astype(F32)
        d = s[HALO:, :] / cnt - u[:, sl]
        y = jnp.dot(d.astype(BF16), pw_ref[gi], preferred_element_type=F32)
        o_ref[:, sl] = (y * ps_ref[:, sl]).astype(BF16)


def _pool(zm, pool_w, pool_scale, *, tm):
    S = zm.shape[0]
    return pl.pallas_call(
        _pool_kernel,
        out_shape=jax.ShapeDtypeStruct((S, D_POOL), BF16),
        grid=(S // tm,),
        in_specs=[
            pl.BlockSpec((tm, D_POOL), lambda i: (i, 0)),
            pl.BlockSpec((HALO, D_POOL), lambda i: (jnp.maximum(i * (tm // HALO) - 1, 0), 0)),
            pl.BlockSpec((len(POOL_WINDOWS), POOL_GROUP, POOL_GROUP), lambda i: (0, 0, 0)),
            pl.BlockSpec((1, D_POOL), lambda i: (0, 0)),
        ],
        out_specs=pl.BlockSpec((tm, D_POOL), lambda i: (i, 0)),
        compiler_params=_params(1),
        name="pool",
    )(zm, zm, pool_w, pool_scale)


I16 = jnp.int16
I16_MIN = -2 ** 15
POOL = 8


def _index_kernel(kidx_ref, qiT_ref, wT_ref, keep_ref, hi_ref, lo_ref, pool_ref, L_ref, n_eq_ref, take_eq_ref,
                  *, T, CK, S, topk):
    qb = pl.program_id(0)
    nck = (qb * T + T + CK - 1) // CK
    q_pos = qb * T + lax.broadcasted_iota(jnp.int32, (1, T), 1)

    def score_chunk(off):
        kc = kidx_ref[pl.ds(off, CK), :]
        acc = None
        for h in range(IDX_HEADS):
            s = jnp.dot(kc, qiT_ref[h * IDX_DIM:(h + 1) * IDX_DIM, :], preferred_element_type=F32)
            t = wT_ref[h:h + 1, :] * jnp.maximum(s, 0.0)
            acc = t if acc is None else acc + t
        bits = lax.bitcast_convert_type(acc, jnp.int32)
        key = bits ^ ((bits >> 31) & jnp.int32(0x7FFFFFFF))
        key_pos = off + lax.broadcasted_iota(jnp.int32, (CK, 1), 0)
        key = jnp.where(key_pos <= q_pos, key, jnp.int32(-2 ** 31))
        hi_ref[pl.ds(off, CK), :] = (key >> 16).astype(I16)
        lo_ref[pl.ds(off, CK), :] = ((key & jnp.int32(0xFFFF)) + I16_MIN).astype(I16)

    def score_body(c, carry):
        for u in range(2):
            score_chunk(pl.multiple_of((2 * c + u) * CK, CK))
        return carry

    lax.fori_loop(0, (nck + 1) // 2, score_body, 0)

    none16 = jnp.full((), I16_MIN, I16)

    def count_ge(ref, cand, trips):
        c16 = cand.astype(I16)

        def count_body(c, acc):
            off = pl.multiple_of(c * CK, CK)
            ge = jnp.where(ref[pl.ds(off, CK), :] >= c16, jnp.ones((), I16), jnp.zeros((), I16))
            ge = ge.reshape(CK // 16, 16, T)
            parts = [ge[r] for r in range(CK // 16)]
            while len(parts) > 1:
                parts = [parts[i] + parts[i + 1] for i in range(0, len(parts), 2)]
            return acc + parts[0]

        acc = lax.fori_loop(0, trips, count_body, jnp.zeros((16, T), I16))
        return acc.astype(jnp.int32).sum(axis=0, keepdims=True)

    def kth_largest(ref, k, trips):
        def bit_body(i, carry):
            v, n_ge_v, n_gt_v = carry
            cand = v + (jnp.int32(1) << (15 - i))
            n_cand = count_ge(ref, cand, trips)
            ok = n_cand >= k
            return jnp.where(ok, cand, v), jnp.where(ok, n_cand, n_ge_v), jnp.where(ok, n_gt_v, n_cand)

        zero = jnp.zeros((1, T), jnp.int32)
        return lax.fori_loop(0, 16, bit_body, (zero + I16_MIN, zero + trips * CK, zero))

    H, n_ge, n_gt = kth_largest(hi_ref, topk, nck)
    H16 = H.astype(I16)
    need = topk - n_gt

    def pool_body(c, carry):
        for g in range(CK // (16 * POOL)):
            a = b = jnp.full((16, T), I16_MIN, I16)
            for r in range(POOL):
                rows = pl.ds(pl.multiple_of(c * CK + (g * POOL + r) * 16, 16), 16)
                x = jnp.where(hi_ref[rows, :] == H16, lo_ref[rows, :], none16)
                up = x > a
                t = jnp.where(up, a, x)
                a = jnp.where(up, x, a)
                b = jnp.where(t > b, t, b)
            base = pl.multiple_of(c * POOL_ROWS + g * 32, 32)
            pool_ref[pl.ds(base, 16), :] = a
            pool_ref[pl.ds(base + 16, 16), :] = b
        return carry

    POOL_ROWS = 2 * CK // POOL
    per_trip = CK // POOL_ROWS
    pool_trips = (nck + per_trip - 1) // per_trip
    lax.fori_loop(0, nck, pool_body, 0)

    def pad_body(c, carry):
        pool_ref[pl.ds(pl.multiple_of(c * POOL_ROWS, POOL_ROWS), POOL_ROWS), :] = jnp.full((POOL_ROWS, T), I16_MIN, I16)
        return carry

    lax.fori_loop(nck, pool_trips * per_trip, pad_body, 0)

    n_tied = n_ge - n_gt
    n_kept = count_ge(pool_ref, jnp.full((1, T), I16_MIN + 1, jnp.int32), pool_trips)
    lost = jnp.sum(jnp.where((n_kept == n_tied) | (H == I16_MIN), 0, 1))

    def second_level(ref, trips):
        L, n_ge_l, n_gt_l = kth_largest(ref, need, trips)
        L_ref[...] = L
        n_eq_ref[...] = n_ge_l - n_gt_l
        take_eq_ref[...] = need - n_gt_l

    @pl.when(lost == 0)
    def _():
        second_level(pool_ref, pool_trips)

    @pl.when(lost != 0)
    def _():
        def mask_body(c, carry):
            off = pl.multiple_of(c * CK, CK)
            lo_ref[pl.ds(off, CK), :] = jnp.where(hi_ref[pl.ds(off, CK), :] == H16, lo_ref[pl.ds(off, CK), :],
                                                  none16)
            return carry

        lax.fori_loop(0, nck, mask_body, 0)
        second_level(lo_ref, nck)

    L16 = L_ref[...].astype(I16)
    take_eq = take_eq_ref[...]
    surplus = jnp.sum(jnp.where((n_eq_ref[...] > take_eq) & (H > I16_MIN), 1, 0))

    def emit(off, sel):
        sel = sel & (hi_ref[pl.ds(off, CK), :] > none16)
        keep_ref[0, pl.ds(off, CK), :] = jnp.where(sel, jnp.ones((), BF16), jnp.zeros((), BF16))

    @pl.when(surplus == 0)
    def _():
        def emit_body(c, carry):
            off = pl.multiple_of(c * CK, CK)
            hi = hi_ref[pl.ds(off, CK), :]
            emit(off, (hi > H16) | ((hi == H16) & (lo_ref[pl.ds(off, CK), :] >= L16)))
            return carry

        lax.fori_loop(0, nck, emit_body, 0)

    @pl.when(surplus != 0)
    def _():
        tri = (lax.broadcasted_iota(jnp.int32, (CK, CK), 0) >= lax.broadcasted_iota(jnp.int32, (CK, CK), 1))
        tri = jnp.where(tri, 1.0, 0.0).astype(BF16)
        take_f = take_eq.astype(F32)

        def emit_body(c, seen):
            off = pl.multiple_of(c * CK, CK)
            hi = hi_ref[pl.ds(off, CK), :]
            lo = lo_ref[pl.ds(off, CK), :]
            eq = (hi == H16) & (lo == L16) & (hi > none16)
            rank = seen + jnp.dot(tri, jnp.where(eq, jnp.ones((), BF16), jnp.zeros((), BF16)),
                                  preferred_element_type=F32)
            first = jnp.where(rank <= take_f, 1.0, 0.0).astype(BF16) > jnp.zeros((), BF16)
            emit(off, (hi > H16) | ((hi == H16) & (lo > L16)) | (eq & first))
            return rank[CK - 1:CK, :]

        lax.fori_loop(0, nck, emit_body, jnp.zeros((1, T), F32))

    def fill_body(c, carry):
        off = pl.multiple_of(c * CK, CK)
        keep_ref[0, pl.ds(off, CK), :] = jnp.zeros((CK, T), BF16)
        return carry

    lax.fori_loop(nck, S // CK, fill_body, 0)


def _index(kidx, tr, wT, *, T, CK, topk):
    S = kidx.shape[0]
    assert S % (2 * CK) == 0 and CK % T == 0 and (S // CK) % (POOL // 2) == 0
    return pl.pallas_call(
        functools.partial(_index_kernel, T=T, CK=CK, S=S, topk=topk),
        out_shape=jax.ShapeDtypeStruct((S // T, S, T), BF16),
        grid=(S // T,),
        in_specs=[
            pl.BlockSpec((S, IDX_DIM), lambda q: (0, 0)),
            pl.BlockSpec((IDX_HEADS * IDX_DIM, T), lambda q: (2, q)),
            pl.BlockSpec((IDX_HEADS, T), lambda q: (0, q)),
        ],
        out_specs=pl.BlockSpec((1, S, T), lambda q: (q, 0, 0)),
        scratch_shapes=[pltpu.VMEM((S, T), I16), pltpu.VMEM((S, T), I16),
                        pltpu.VMEM((2 * S // POOL, T), I16)] + [pltpu.VMEM((1, T), jnp.int32)] * 3,
        compiler_params=_params(1),
        name="index",
    )(kidx, tr, wT)


def _attn_kernel(qb_ref, kb_ref, kn_ref, qT_ref, vT_ref, *rest, T, KB, SUB, online):
    *keep_refs, o_ref, m_ref, l_ref, acc_ref = rest
    i = pl.program_id(0)
    qb = qb_ref[i]
    kb = kb_ref[i]
    TQ = T * len(keep_refs)

    @pl.when(kb == 0)
    def _():
        m_ref[...] = jnp.full(m_ref.shape, MASK_NEG, F32)
        l_ref[...] = jnp.zeros(l_ref.shape, F32)
        acc_ref[...] = jnp.zeros(acc_ref.shape, F32)

    items = [(kb0, h) for h in range(N_HEADS) for kb0 in range(0, KB, SUB)]

    def qk(item):
        kb0, h = item
        hs = slice(h * HEAD_DIM, (h + 1) * HEAD_DIM)
        return jnp.dot(kn_ref[kb0:kb0 + SUB, hs], qT_ref[hs, :], preferred_element_type=F32)

    ahead = 4
    pending = [qk(it) for it in items[:ahead]]
    def additive(kb0):
        keep = jnp.concatenate([r[0, kb0:kb0 + SUB, :] for r in keep_refs], axis=1)
        return jnp.where(keep > jnp.zeros((), BF16), jnp.zeros((), BF16), jnp.full((), MASK_NEG, BF16)).astype(F32)

    bias = {kb0: additive(kb0) for kb0 in range(0, KB, SUB)}
    for n, (kb0, h) in enumerate(items):
        hs = slice(h * HEAD_DIM, (h + 1) * HEAD_DIM)
        s = bias[kb0] + pending.pop(0)
        if n + ahead < len(items):
            pending.append(qk(items[n + ahead]))
        if online:
            m_old = m_ref[h]
            m_new = jnp.maximum(m_old, s.max(axis=0, keepdims=True))
            alpha = jnp.exp2(m_old - m_new)
            p = jnp.exp2(s - m_new)
            l_ref[h] = alpha * l_ref[h] + p.sum(axis=0, keepdims=True)
            pv = jnp.dot(vT_ref[hs, kb0:kb0 + SUB], p.astype(BF16), preferred_element_type=F32)
            acc_ref[h] = alpha * acc_ref[h] + pv
            m_ref[h] = m_new
        else:
            p = jnp.exp2(s)
            l_ref[h] += p.sum(axis=0, keepdims=True)
            acc_ref[h] += jnp.dot(vT_ref[hs, kb0:kb0 + SUB], p.astype(BF16), preferred_element_type=F32)

    @pl.when(kb == (qb * TQ + TQ - 1) // KB)
    def _():
        for h in range(N_HEADS):
            o_ref[:, h * HEAD_DIM:(h + 1) * HEAD_DIM] = (acc_ref[h] / l_ref[h]).T.astype(BF16)


def _attn(nat, tr, keep, *, T, TQ, KB, SUB, online):
    S = nat.shape[0]
    groups = TQ // T
    pairs = [(q, k) for q in range(S // TQ) for k in range((q * TQ + TQ - 1) // KB + 1)]
    qb_ids = jnp.asarray(np.array([p[0] for p in pairs], np.int32))
    kb_ids = jnp.asarray(np.array([p[1] for p in pairs], np.int32))
    grid_spec = pltpu.PrefetchScalarGridSpec(
        num_scalar_prefetch=2,
        grid=(len(pairs),),
        in_specs=[
            pl.BlockSpec((KB, D_ATTN), lambda i, qb, kb: (kb[i], 1)),
            pl.BlockSpec((D_ATTN, TQ), lambda i, qb, kb: (0, qb[i])),
            pl.BlockSpec((D_ATTN, KB), lambda i, qb, kb: (1, kb[i])),
        ] + [pl.BlockSpec((1, KB, T), lambda i, qb, kb, g=g: (qb[i] * groups + g, kb[i], 0)) for g in range(groups)],
        out_specs=pl.BlockSpec((TQ, D_ATTN), lambda i, qb, kb: (qb[i], 0)),
        scratch_shapes=[pltpu.VMEM((N_HEADS, 1, TQ), F32), pltpu.VMEM((N_HEADS, 1, TQ), F32),
                        pltpu.VMEM((N_HEADS, HEAD_DIM, TQ), F32)],
    )
    return pl.pallas_call(
        functools.partial(_attn_kernel, T=T, KB=KB, SUB=SUB, online=online),
        out_shape=jax.ShapeDtypeStruct((S, D_ATTN), BF16),
        grid_spec=grid_spec,
        compiler_params=_params(1),
        name="attn_online" if online else "attn",
    )(qb_ids, kb_ids, nat, tr, tr, *([keep] * groups))


def _outproj_kernel(yp_ref, ya_ref, wp_ref, wa_ref, x_ref, o_ref):
    acc = jnp.dot(yp_ref[...], wp_ref[...], preferred_element_type=F32)
    acc = acc + jnp.dot(ya_ref[...], wa_ref[...], preferred_element_type=F32)
    o_ref[...] = x_ref[...] + acc


def _outproj(yp, ya, w_out, x2, *, tm, tn):
    S = x2.shape[0]
    n_j = D_MODEL // tn
    return pl.pallas_call(
        _outproj_kernel,
        out_shape=jax.ShapeDtypeStruct((S, D_MODEL), F32),
        grid=(S // tm, n_j),
        in_specs=[
            pl.BlockSpec((tm, D_POOL), lambda i, j: (i, 0)),
            pl.BlockSpec((tm, D_ATTN), lambda i, j: (i, 0)),
            pl.BlockSpec((D_POOL, tn), lambda i, j: (0, j)),
            pl.BlockSpec((D_ATTN, tn), lambda i, j: (D_POOL // D_ATTN, j)),
            pl.BlockSpec((tm, tn), lambda i, j: (i, j)),
        ],
        out_specs=pl.BlockSpec((tm, tn), lambda i, j: (i, j)),
        compiler_params=_params(2),
        name="outproj",
    )(yp, ya, w_out, w_out, x2)


def _ffn_kernel(x_ref, halo_ref, g_ref, wg_ref, wv_ref, cwg_ref, cwv_ref, cbg_ref, cbv_ref, wd_ref,
                o_ref, h_ref):
    i = pl.program_id(0)
    f = pl.program_id(1)

    @pl.when(f == 0)
    def _():
        hh = jnp.where(i == 0, 0.0, _rms(halo_ref[...], g_ref[...]))
        h_ref[0:HALO, :] = hh.astype(BF16)
        x = x_ref[...]
        h_ref[HALO:, :] = _rms(x, g_ref[...]).astype(BF16)
        o_ref[...] = x

    h = h_ref[...]

    def conv(w_ref, cw_ref, cb_ref):
        up = jnp.dot(h, w_ref[...], preferred_element_type=F32)
        c = (cw_ref[0:1, :] * pltpu.roll(up, 2, axis=0) + cw_ref[1:2, :] * pltpu.roll(up, 1, axis=0)
             + cw_ref[2:3, :] * up)
        return cb_ref[...] + c[HALO:, :]

    cg = conv(wg_ref, cwg_ref, cbg_ref)
    cv = conv(wv_ref, cwv_ref, cbv_ref)
    act = cg * (1.0 / (1.0 + jnp.exp(-cg))) * cv
    o_ref[...] += jnp.dot(act.astype(BF16), wd_ref[...], preferred_element_type=F32)


def _ffn(x1, g, w_up, conv_w, conv_b, w_down, *, tm, tf):
    S = x1.shape[0]
    n_f = D_FF // tf
    return pl.pallas_call(
        _ffn_kernel,
        out_shape=jax.ShapeDtypeStruct((S, D_MODEL), F32),
        grid=(S // tm, n_f),
        in_specs=[
            pl.BlockSpec((tm, D_MODEL), lambda i, f: (i, 0)),
            pl.BlockSpec((HALO, D_MODEL), lambda i, f: (jnp.maximum(i * (tm // HALO) - 1, 0), 0)),
            pl.BlockSpec((1, D_MODEL), lambda i, f: (0, 0)),
            pl.BlockSpec((D_MODEL, tf), lambda i, f: (0, f)),
            pl.BlockSpec((D_MODEL, tf), lambda i, f: (0, f + n_f)),
            pl.BlockSpec((CONV_WIDTH, tf), lambda i, f: (0, f)),
            pl.BlockSpec((CONV_WIDTH, tf), lambda i, f: (0, f + n_f)),
            pl.BlockSpec((1, tf), lambda i, f: (0, f)),
            pl.BlockSpec((1, tf), lambda i, f: (0, f + n_f)),
            pl.BlockSpec((tf, D_MODEL), lambda i, f: (f, 0)),
        ],
        out_specs=pl.BlockSpec((tm, D_MODEL), lambda i, f: (i, 0)),
        scratch_shapes=[pltpu.VMEM((HALO + tm, D_MODEL), BF16)],
        compiler_params=_params(2),
        name="ffn",
    )(x1, x1, g, w_up, w_up, conv_w, conv_w, conv_b, conv_b, w_down)


def kernel(x, attn_norm_g, w_in, pool_w, pool_scale, q_norm_g, k_norm_g, w_out, ffn_norm_g, w_up,
           conv_w, conv_b, w_down):
    B, S, D = x.shape
    assert B == 1 and D == D_MODEL and w_in.shape == (D_MODEL, D_MAIN + D_TAIL)
    t = TILES
    assert all(S % n == 0 for n in (t.inproj_rows, t.pool_rows, t.outproj_rows, t.ffn_rows, t.attn_keys))
    topk = min(TOPK_MAX, S // 4)
    q_scale = HEAD_DIM ** -0.5 * LOG2E
    x2 = x.reshape(S, D)

    o = D_POOL
    w_q, w_k, w_v = (w_in[:, o + n * D_ATTN:o + (n + 1) * D_ATTN] for n in range(3))
    w_qi = w_in[:, o + 3 * D_ATTN:D_MAIN]
    w_nat = jnp.concatenate([w_in[:, :o], w_k], axis=1).astype(BF16)
    w_tr = jnp.concatenate([w_q, w_v, w_qi], axis=1).astype(BF16)
    w_tail = jnp.pad(w_in[:, D_MAIN:], ((0, 0), (0, TAIL_PAD - D_TAIL))).astype(BF16)
    nat, tr, kidx, wT = _inproj(x2, attn_norm_g.reshape(1, D), w_nat, w_tr, w_tail,
                                k_norm_g.reshape(1, HEAD_DIM), q_norm_g.reshape(HEAD_DIM, 1),
                                tm=t.inproj_rows, tn=t.inproj_cols, q_scale=q_scale)

    y_pool = _pool(nat, pool_w.astype(BF16), pool_scale.reshape(1, D_POOL), tm=t.pool_rows)
    keep = _index(kidx, tr, wT, T=t.dsa_queries, CK=t.index_keys, topk=topk)
    score_bound = HEAD_DIM * jnp.max(jnp.abs(q_norm_g)) * jnp.max(jnp.abs(k_norm_g)) * q_scale
    attn = functools.partial(_attn, T=t.dsa_queries, TQ=t.attn_queries, KB=t.attn_keys, SUB=t.attn_sub_keys)
    y_attn = lax.cond(score_bound < MAX_UNSHIFTED_LOG2,
                      functools.partial(attn, online=False), functools.partial(attn, online=True),
                      nat, tr, keep)

    x1 = _outproj(y_pool, y_attn, w_out.astype(BF16), x2, tm=t.outproj_rows, tn=t.outproj_cols)
    out = _ffn(x1, ffn_norm_g.reshape(1, D), w_up.astype(BF16), conv_w, conv_b.reshape(1, 2 * D_FF),
               w_down.astype(BF16), tm=t.ffn_rows, tf=t.ffn_cols)
    return out.reshape(B, S, D)
```

```python
import functools
from typing import NamedTuple

import numpy as np
import jax
import jax.numpy as jnp
from jax import lax
from jax.experimental import pallas as pl
from jax.experimental.pallas import tpu as pltpu

D_MODEL = 2048
D_POOL = 1024
POOL_WINDOWS = (2, 4, 8, 16)
POOL_GROUP = D_POOL // len(POOL_WINDOWS)
D_ATTN = 1024
HEAD_DIM = 128
N_HEADS = D_ATTN // HEAD_DIM
IDX_HEADS = 16
IDX_DIM = 64
TOPK_MAX = 256
D_FF = 5632
CONV_WIDTH = 3
EPS = 1e-6

D_MAIN = D_POOL + 3 * D_ATTN + IDX_HEADS * IDX_DIM
D_TAIL = IDX_DIM + IDX_HEADS
TAIL_PAD = 128
HALO = 16
MASK_NEG = -1e30
LOG2E = 1.4426950408889634
MAX_UNSHIFTED_LOG2 = 60.0
V7X_VMEM_BYTES = 64 * 1024 * 1024
VMEM_LIMIT_BYTES = V7X_VMEM_BYTES * 7 // 8


class _Tiles(NamedTuple):
    inproj_rows: int = 1024
    inproj_cols: int = 1024
    pool_rows: int = 1024
    dsa_queries: int = 256
    attn_queries: int = 512
    index_keys: int = 512
    attn_keys: int = 1024
    attn_sub_keys: int = 512
    outproj_rows: int = 512
    outproj_cols: int = 2048
    ffn_rows: int = 1024
    ffn_cols: int = 512


TILES = _Tiles()

F32 = jnp.float32
BF16 = jnp.bfloat16


def _params(n_axes):
    return pltpu.CompilerParams(dimension_semantics=("arbitrary",) * n_axes,
                                vmem_limit_bytes=VMEM_LIMIT_BYTES)


def _rms(xf, g):
    return xf * lax.rsqrt(jnp.mean(xf * xf, axis=-1, keepdims=True) + EPS) * g


_TN_DIMS = (((0,), (1,)), ((), ()))


def _inproj_kernel(x_ref, g_ref, wn_ref, wt_ref, wtail_ref, kg_ref, qg_ref, nat_ref, tr_ref, kidx_ref,
                   wT_ref, h_ref, *, tn, q_scale):
    j = pl.program_id(1)
    n_nat = (D_POOL + D_ATTN) // tn

    @pl.when(j == 0)
    def _():
        h = _rms(x_ref[...], g_ref[...]).astype(BF16)
        h_ref[...] = h
        tail = jnp.dot(h, wtail_ref[...], preferred_element_type=F32)
        kidx_ref[...] = tail[:, :IDX_DIM].astype(BF16)
        wT_ref[...] = tail.T[IDX_DIM:D_TAIL, :] * (IDX_HEADS ** -0.5) * (IDX_DIM ** -0.5)

    @pl.when(j < n_nat)
    def _():
        z = jnp.dot(h_ref[...], wn_ref[...], preferred_element_type=F32)

        @pl.when(j < D_POOL // tn)
        def _():
            nat_ref[...] = z.astype(BF16)

        @pl.when(j >= D_POOL // tn)
        def _():
            for c in range(tn // HEAD_DIM):
                sl = slice(c * HEAD_DIM, (c + 1) * HEAD_DIM)
                nat_ref[:, sl] = _rms(z[:, sl], kg_ref[...]).astype(BF16)

    @pl.when(j >= n_nat)
    def _():
        zT = lax.dot_general(wt_ref[...], h_ref[...], _TN_DIMS, preferred_element_type=F32)

        @pl.when(j < n_nat + D_ATTN // tn)
        def _():
            for c in range(tn // HEAD_DIM):
                sl = slice(c * HEAD_DIM, (c + 1) * HEAD_DIM)
                zc = zT[sl, :]
                inv = lax.rsqrt(jnp.mean(zc * zc, axis=0, keepdims=True) + EPS)
                tr_ref[sl, :] = (zc * inv * (qg_ref[...] * q_scale)).astype(BF16)

        @pl.when(j >= n_nat + D_ATTN // tn)
        def _():
            tr_ref[...] = zT.astype(BF16)


def _inproj(x2, g, w_main, w_tail, kg, qg_col, *, tm, tn, q_scale):
    S = x2.shape[0]
    assert D_POOL == D_ATTN == IDX_HEADS * IDX_DIM and D_ATTN % tn == 0
    per = D_ATTN // tn
    n_nat, n_tr = 2 * per, 3 * per

    def nat_tile(j):
        j = jnp.minimum(j, n_nat - 1)
        return 2 * (j // per) * per + j % per

    def tr_tile(j):
        j = jnp.maximum(j - n_nat, 0)
        sec = j // per
        return jnp.where(sec == 2, 4, 1 + 2 * sec) * per + j % per

    return pl.pallas_call(
        functools.partial(_inproj_kernel, tn=tn, q_scale=q_scale),
        out_shape=(jax.ShapeDtypeStruct((S, n_nat * tn), BF16),
                   jax.ShapeDtypeStruct((n_tr * tn, S), BF16),
                   jax.ShapeDtypeStruct((S, IDX_DIM), BF16),
                   jax.ShapeDtypeStruct((IDX_HEADS, S), F32)),
        grid=(S // tm, n_nat + n_tr),
        in_specs=[
            pl.BlockSpec((tm, D_MODEL), lambda i, j: (i, 0)),
            pl.BlockSpec((1, D_MODEL), lambda i, j: (0, 0)),
            pl.BlockSpec((D_MODEL, tn), lambda i, j: (0, nat_tile(j))),
            pl.BlockSpec((D_MODEL, tn), lambda i, j: (0, tr_tile(j))),
            pl.BlockSpec((D_MODEL, TAIL_PAD), lambda i, j: (0, 0)),
            pl.BlockSpec((1, HEAD_DIM), lambda i, j: (0, 0)),
            pl.BlockSpec((HEAD_DIM, 1), lambda i, j: (0, 0)),
        ],
        out_specs=(pl.BlockSpec((tm, tn), lambda i, j: (i, jnp.minimum(j, n_nat - 1))),
                   pl.BlockSpec((tn, tm), lambda i, j: (jnp.maximum(j - n_nat, 0), i)),
                   pl.BlockSpec((tm, IDX_DIM), lambda i, j: (i, 0)),
                   pl.BlockSpec((IDX_HEADS, tm), lambda i, j: (0, i))),
        scratch_shapes=[pltpu.VMEM((tm, D_MODEL), BF16)],
        compiler_params=_params(2),
        name="inproj",
    )(x2, g, w_main, w_main, w_tail, kg, qg_col)


def _pool_kernel(u_ref, halo_ref, pw_ref, ps_ref, o_ref):
    i = pl.program_id(0)
    tm = u_ref.shape[0]
    u = u_ref[...].astype(F32)
    halo = jnp.where(i == 0, 0.0, halo_ref[...].astype(F32))
    ext = jnp.concatenate([halo, u], axis=0)
    t = i * tm + lax.broadcasted_iota(jnp.int32, (tm, 1), 0)
    for gi, w in enumerate(POOL_WINDOWS):
        sl = slice(gi * POOL_GROUP, (gi + 1) * POOL_GROUP)
        s = ext[:, sl]
        step = 1
        while step < w:
            s = s + pltpu.roll(s, step, axis=0)
            step *= 2
        cnt = jnp.minimum(t + 1, w).astype(F32)
        d = s[HALO:, :] / cnt - u[:, sl]
        y = jnp.dot(d.astype(BF16), pw_ref[gi], preferred_element_type=F32)
        o_ref[:, sl] = (y * ps_ref[:, sl]).astype(BF16)


def _pool(zm, pool_w, pool_scale, *, tm):
    S = zm.shape[0]
    return pl.pallas_call(
        _pool_kernel,
        out_shape=jax.ShapeDtypeStruct((S, D_POOL), BF16),
        grid=(S // tm,),
        in_specs=[
            pl.BlockSpec((tm, D_POOL), lambda i: (i, 0)),
            pl.BlockSpec((HALO, D_POOL), lambda i: (jnp.maximum(i * (tm // HALO) - 1, 0), 0)),
            pl.BlockSpec((len(POOL_WINDOWS), POOL_GROUP, POOL_GROUP), lambda i: (0, 0, 0)),
            pl.BlockSpec((1, D_POOL), lambda i: (0, 0)),
        ],
        out_specs=pl.BlockSpec((tm, D_POOL), lambda i: (i, 0)),
        compiler_params=_params(1),
        name="pool",
    )(zm, zm, pool_w, pool_scale)


I16 = jnp.int16
I16_MIN = -2 ** 15
POOL = 8


def _index_kernel(kidx_ref, qiT_ref, wT_ref, keep_ref, hi_ref, lo_ref, pool_ref, L_ref, n_eq_ref, take_eq_ref,
                  *, T, CK, S, topk):
    qb = pl.program_id(0)
    nck = (qb * T + T + CK - 1) // CK
    q_pos = qb * T + lax.broadcasted_iota(jnp.int32, (1, T), 1)

    def score_chunk(off):
        kc = kidx_ref[pl.ds(off, CK), :]
        acc = None
        for h in range(IDX_HEADS):
            s = jnp.dot(kc, qiT_ref[h * IDX_DIM:(h + 1) * IDX_DIM, :], preferred_element_type=F32)
            t = wT_ref[h:h + 1, :] * jnp.maximum(s, 0.0)
            acc = t if acc is None else acc + t
        bits = lax.bitcast_convert_type(acc, jnp.int32)
        key = bits ^ ((bits >> 31) & jnp.int32(0x7FFFFFFF))
        key_pos = off + lax.broadcasted_iota(jnp.int32, (CK, 1), 0)
        key = jnp.where(key_pos <= q_pos, key, jnp.int32(-2 ** 31))
        hi_ref[pl.ds(off, CK), :] = (key >> 16).astype(I16)
        lo_ref[pl.ds(off, CK), :] = ((key & jnp.int32(0xFFFF)) + I16_MIN).astype(I16)

    def score_body(c, carry):
        for u in range(2):
            score_chunk(pl.multiple_of((2 * c + u) * CK, CK))
        return carry

    lax.fori_loop(0, (nck + 1) // 2, score_body, 0)

    none16 = jnp.full((), I16_MIN, I16)

    def count_ge(ref, cand, trips):
        c16 = cand.astype(I16)

        def count_body(c, acc):
            off = pl.multiple_of(c * CK, CK)
            ge = jnp.where(ref[pl.ds(off, CK), :] >= c16, jnp.ones((), I16), jnp.zeros((), I16))
            ge = ge.reshape(CK // 16, 16, T)
            parts = [ge[r] for r in range(CK // 16)]
            while len(parts) > 1:
                parts = [parts[i] + parts[i + 1] for i in range(0, len(parts), 2)]
            return acc + parts[0]

        acc = lax.fori_loop(0, trips, count_body, jnp.zeros((16, T), I16))
        return acc.astype(jnp.int32).sum(axis=0, keepdims=True)

    def kth_largest(ref, k, trips):
        def bit_body(i, carry):
            v, n_ge_v, n_gt_v = carry
            cand = v + (jnp.int32(1) << (15 - i))
            n_cand = count_ge(ref, cand, trips)
            ok = n_cand >= k
            return jnp.where(ok, cand, v), jnp.where(ok, n_cand, n_ge_v), jnp.where(ok, n_gt_v, n_cand)

        zero = jnp.zeros((1, T), jnp.int32)
        return lax.fori_loop(0, 16, bit_body, (zero + I16_MIN, zero + trips * CK, zero))

    H, n_ge, n_gt = kth_largest(hi_ref, topk, nck)
    H16 = H.astype(I16)
    need = topk - n_gt

    def pool_body(c, carry):
        for g in range(CK // (16 * POOL)):
            a = b = jnp.full((16, T), I16_MIN, I16)
            for r in range(POOL):
                rows = pl.ds(pl.multiple_of(c * CK + (g * POOL + r) * 16, 16), 16)
                x = jnp.where(hi_ref[rows, :] == H16, lo_ref[rows, :], none16)
                up = x > a
                t = jnp.where(up, a, x)
                a = jnp.where(up, x, a)
                b = jnp.where(t > b, t, b)
            base = pl.multiple_of(c * POOL_ROWS + g * 32, 32)
            pool_ref[pl.ds(base, 16), :] = a
            pool_ref[pl.ds(base + 16, 16), :] = b
        return carry

    POOL_ROWS = 2 * CK // POOL
    per_trip = CK // POOL_ROWS
    pool_trips = (nck + per_trip - 1) // per_trip
    lax.fori_loop(0, nck, pool_body, 0)

    def pad_body(c, carry):
        pool_ref[pl.ds(pl.multiple_of(c * POOL_ROWS, POOL_ROWS), POOL_ROWS), :] = jnp.full((POOL_ROWS, T), I16_MIN, I16)
        return carry

    lax.fori_loop(nck, pool_trips * per_trip, pad_body, 0)

    n_tied = n_ge - n_gt
    n_kept = count_ge(pool_ref, jnp.full((1, T), I16_MIN + 1, jnp.int32), pool_trips)
    lost = jnp.sum(jnp.where((n_kept == n_tied) | (H == I16_MIN), 0, 1))

    def second_level(ref, trips):
        L, n_ge_l, n_gt_l = kth_largest(ref, need, trips)
        L_ref[...] = L
        n_eq_ref[...] = n_ge_l - n_gt_l
        take_eq_ref[...] = need - n_gt_l

    @pl.when(lost == 0)
    def _():
        second_level(pool_ref, pool_trips)

    @pl.when(lost != 0)
    def _():
        def mask_body(c, carry):
            off = pl.multiple_of(c * CK, CK)
            lo_ref[pl.ds(off, CK), :] = jnp.where(hi_ref[pl.ds(off, CK), :] == H16, lo_ref[pl.ds(off, CK), :],
                                                  none16)
            return carry

        lax.fori_loop(0, nck, mask_body, 0)
        second_level(lo_ref, nck)

    L16 = L_ref[...].astype(I16)
    take_eq = take_eq_ref[...]
    surplus = jnp.sum(jnp.where((n_eq_ref[...] > take_eq) & (H > I16_MIN), 1, 0))

    def emit(off, sel):
        sel = sel & (hi_ref[pl.ds(off, CK), :] > none16)
        keep_ref[0, pl.ds(off, CK), :] = jnp.where(sel, jnp.ones((), BF16), jnp.zeros((), BF16))

    @pl.when(surplus == 0)
    def _():
        def emit_body(c, carry):
            off = pl.multiple_of(c * CK, CK)
            hi = hi_ref[pl.ds(off, CK), :]
            emit(off, (hi > H16) | ((hi == H16) & (lo_ref[pl.ds(off, CK), :] >= L16)))
            return carry

        lax.fori_loop(0, nck, emit_body, 0)

    @pl.when(surplus != 0)
    def _():
        tri = (lax.broadcasted_iota(jnp.int32, (CK, CK), 0) >= lax.broadcasted_iota(jnp.int32, (CK, CK), 1))
        tri = jnp.where(tri, 1.0, 0.0).astype(BF16)
        take_f = take_eq.astype(F32)

        def emit_body(c, seen):
            off = pl.multiple_of(c * CK, CK)
            hi = hi_ref[pl.ds(off, CK), :]
            lo = lo_ref[pl.ds(off, CK), :]
            eq = (hi == H16) & (lo == L16) & (hi > none16)
            rank = seen + jnp.dot(tri, jnp.where(eq, jnp.ones((), BF16), jnp.zeros((), BF16)),
                                  preferred_element_type=F32)
            first = jnp.where(rank <= take_f, 1.0, 0.0).astype(BF16) > jnp.zeros((), BF16)
            emit(off, (hi > H16) | ((hi == H16) & (lo > L16)) | (eq & first))
            return rank[CK - 1:CK, :]

        lax.fori_loop(0, nck, emit_body, jnp.zeros((1, T), F32))

    def fill_body(c, carry):
        off = pl.multiple_of(c * CK, CK)
        keep_ref[0, pl.ds(off, CK), :] = jnp.zeros((CK, T), BF16)
        return carry

    lax.fori_loop(nck, S // CK, fill_body, 0)


def _index(kidx, tr, wT, *, T, CK, topk):
    S = kidx.shape[0]
    assert S % (2 * CK) == 0 and CK % T == 0 and (S // CK) % (POOL // 2) == 0
    return pl.pallas_call(
        functools.partial(_index_kernel, T=T, CK=CK, S=S, topk=topk),
        out_shape=jax.ShapeDtypeStruct((S // T, S, T), BF16),
        grid=(S // T,),
        in_specs=[
            pl.BlockSpec((S, IDX_DIM), lambda q: (0, 0)),
            pl.BlockSpec((IDX_HEADS * IDX_DIM, T), lambda q: (2, q)),
            pl.BlockSpec((IDX_HEADS, T), lambda q: (0, q)),
        ],
        out_specs=pl.BlockSpec((1, S, T), lambda q: (q, 0, 0)),
        scratch_shapes=[pltpu.VMEM((S, T), I16), pltpu.VMEM((S, T), I16),
                        pltpu.VMEM((2 * S // POOL, T), I16)] + [pltpu.VMEM((1, T), jnp.int32)] * 3,
        compiler_params=_params(1),
        name="index",
    )(kidx, tr, wT)


def _attn_kernel(qb_ref, kb_ref, kn_ref, qT_ref, vT_ref, *rest, T, KB, SUB, online):
    *keep_refs, o_ref, m_ref, l_ref, acc_ref = rest
    i = pl.program_id(0)
    qb = qb_ref[i]
    kb = kb_ref[i]
    TQ = T * len(keep_refs)

    @pl.when(kb == 0)
    def _():
        m_ref[...] = jnp.full(m_ref.shape, MASK_NEG, F32)
        l_ref[...] = jnp.zeros(l_ref.shape, F32)
        acc_ref[...] = jnp.zeros(acc_ref.shape, F32)

    items = [(kb0, h) for h in range(N_HEADS) for kb0 in range(0, KB, SUB)]

    def qk(item):
        kb0, h = item
        hs = slice(h * HEAD_DIM, (h + 1) * HEAD_DIM)
        return jnp.dot(kn_ref[kb0:kb0 + SUB, hs], qT_ref[hs, :], preferred_element_type=F32)

    ahead = 4
    pending = [qk(it) for it in items[:ahead]]
    def additive(kb0):
        keep = jnp.concatenate([r[0, kb0:kb0 + SUB, :] for r in keep_refs], axis=1)
        return jnp.where(keep > jnp.zeros((), BF16), jnp.zeros((), BF16), jnp.full((), MASK_NEG, BF16)).astype(F32)

    bias = {kb0: additive(kb0) for kb0 in range(0, KB, SUB)}
    for n, (kb0, h) in enumerate(items):
        hs = slice(h * HEAD_DIM, (h + 1) * HEAD_DIM)
        s = bias[kb0] + pending.pop(0)
        if n + ahead < len(items):
            pending.append(qk(items[n + ahead]))
        if online:
            m_old = m_ref[h]
            m_new = jnp.maximum(m_old, s.max(axis=0, keepdims=True))
            alpha = jnp.exp2(m_old - m_new)
            p = jnp.exp2(s - m_new)
            l_ref[h] = alpha * l_ref[h] + p.sum(axis=0, keepdims=True)
            pv = jnp.dot(vT_ref[hs, kb0:kb0 + SUB], p.astype(BF16), preferred_element_type=F32)
            acc_ref[h] = alpha * acc_ref[h] + pv
            m_ref[h] = m_new
        else:
            p = jnp.exp2(s)
            l_ref[h] += p.sum(axis=0, keepdims=True)
            acc_ref[h] += jnp.dot(vT_ref[hs, kb0:kb0 + SUB], p.astype(BF16), preferred_element_type=F32)

    @pl.when(kb == (qb * TQ + TQ - 1) // KB)
    def _():
        for h in range(N_HEADS):
            o_ref[:, h * HEAD_DIM:(h + 1) * HEAD_DIM] = (acc_ref[h] / l_ref[h]).T.astype(BF16)


def _attn(nat, tr, keep, *, T, TQ, KB, SUB, online):
    S = nat.shape[0]
    groups = TQ // T
    pairs = [(q, k) for q in range(S // TQ) for k in range((q * TQ + TQ - 1) // KB + 1)]
    qb_ids = jnp.asarray(np.array([p[0] for p in pairs], np.int32))
    kb_ids = jnp.asarray(np.array([p[1] for p in pairs], np.int32))
    grid_spec = pltpu.PrefetchScalarGridSpec(
        num_scalar_prefetch=2,
        grid=(len(pairs),),
        in_specs=[
            pl.BlockSpec((KB, D_ATTN), lambda i, qb, kb: (kb[i], 1)),
            pl.BlockSpec((D_ATTN, TQ), lambda i, qb, kb: (0, qb[i])),
            pl.BlockSpec((D_ATTN, KB), lambda i, qb, kb: (1, kb[i])),
        ] + [pl.BlockSpec((1, KB, T), lambda i, qb, kb, g=g: (qb[i] * groups + g, kb[i], 0)) for g in range(groups)],
        out_specs=pl.BlockSpec((TQ, D_ATTN), lambda i, qb, kb: (qb[i], 0)),
        scratch_shapes=[pltpu.VMEM((N_HEADS, 1, TQ), F32), pltpu.VMEM((N_HEADS, 1, TQ), F32),
                        pltpu.VMEM((N_HEADS, HEAD_DIM, TQ), F32)],
    )
    return pl.pallas_call(
        functools.partial(_attn_kernel, T=T, KB=KB, SUB=SUB, online=online),
        out_shape=jax.ShapeDtypeStruct((S, D_ATTN), BF16),
        grid_spec=grid_spec,
        compiler_params=_params(1),
        name="attn_online" if online else "attn",
    )(qb_ids, kb_ids, nat, tr, tr, *([keep] * groups))


def _outproj_kernel(yp_ref, ya_ref, wp_ref, wa_ref, x_ref, o_ref):
    acc = jnp.dot(yp_ref[...], wp_ref[...], preferred_element_type=F32)
    acc = acc + jnp.dot(ya_ref[...], wa_ref[...], preferred_element_type=F32)
    o_ref[...] = x_ref[...] + acc


def _outproj(yp, ya, w_out, x2, *, tm, tn):
    S = x2.shape[0]
    n_j = D_MODEL // tn
    return pl.pallas_call(
        _outproj_kernel,
        out_shape=jax.ShapeDtypeStruct((S, D_MODEL), F32),
        grid=(S // tm, n_j),
        in_specs=[
            pl.BlockSpec((tm, D_POOL), lambda i, j: (i, 0)),
            pl.BlockSpec((tm, D_ATTN), lambda i, j: (i, 0)),
            pl.BlockSpec((D_POOL, tn), lambda i, j: (0, j)),
            pl.BlockSpec((D_ATTN, tn), lambda i, j: (D_POOL // D_ATTN, j)),
            pl.BlockSpec((tm, tn), lambda i, j: (i, j)),
        ],
        out_specs=pl.BlockSpec((tm, tn), lambda i, j: (i, j)),
        compiler_params=_params(2),
        name="outproj",
    )(yp, ya, w_out, w_out, x2)


def _ffn_kernel(x_ref, halo_ref, g_ref, wg_ref, wv_ref, cwg_ref, cwv_ref, cbg_ref, cbv_ref, wd_ref,
                o_ref, h_ref):
    i = pl.program_id(0)
    f = pl.program_id(1)

    @pl.when(f == 0)
    def _():
        hh = jnp.where(i == 0, 0.0, _rms(halo_ref[...], g_ref[...]))
        h_ref[0:HALO, :] = hh.astype(BF16)
        x = x_ref[...]
        h_ref[HALO:, :] = _rms(x, g_ref[...]).astype(BF16)
        o_ref[...] = x

    h = h_ref[...]

    def conv(w_ref, cw_ref, cb_ref):
        up = jnp.dot(h, w_ref[...], preferred_element_type=F32)
        c = (cw_ref[0:1, :] * pltpu.roll(up, 2, axis=0) + cw_ref[1:2, :] * pltpu.roll(up, 1, axis=0)
             + cw_ref[2:3, :] * up)
        return cb_ref[...] + c[HALO:, :]

    cg = conv(wg_ref, cwg_ref, cbg_ref)
    cv = conv(wv_ref, cwv_ref, cbv_ref)
    act = cg * (1.0 / (1.0 + jnp.exp(-cg))) * cv
    o_ref[...] += jnp.dot(act.astype(BF16), wd_ref[...], preferred_element_type=F32)


def _ffn(x1, g, w_up, conv_w, conv_b, w_down, *, tm, tf):
    S = x1.shape[0]
    n_f = D_FF // tf
    return pl.pallas_call(
        _ffn_kernel,
        out_shape=jax.ShapeDtypeStruct((S, D_MODEL), F32),
        grid=(S // tm, n_f),
        in_specs=[
            pl.BlockSpec((tm, D_MODEL), lambda i, f: (i, 0)),
            pl.BlockSpec((HALO, D_MODEL), lambda i, f: (jnp.maximum(i * (tm // HALO) - 1, 0), 0)),
            pl.BlockSpec((1, D_MODEL), lambda i, f: (0, 0)),
            pl.BlockSpec((D_MODEL, tf), lambda i, f: (0, f)),
            pl.BlockSpec((D_MODEL, tf), lambda i, f: (0, f + n_f)),
            pl.BlockSpec((CONV_WIDTH, tf), lambda i, f: (0, f)),
            pl.BlockSpec((CONV_WIDTH, tf), lambda i, f: (0, f + n_f)),
            pl.BlockSpec((1, tf), lambda i, f: (0, f)),
            pl.BlockSpec((1, tf), lambda i, f: (0, f + n_f)),
            pl.BlockSpec((tf, D_MODEL), lambda i, f: (f, 0)),
        ],
        out_specs=pl.BlockSpec((tm, D_MODEL), lambda i, f: (i, 0)),
        scratch_shapes=[pltpu.VMEM((HALO + tm, D_MODEL), BF16)],
        compiler_params=_params(2),
        name="ffn",
    )(x1, x1, g, w_up, w_up, conv_w, conv_w, conv_b, conv_b, w_down)


def kernel(x, attn_norm_g, w_in, pool_w, pool_scale, q_norm_g, k_norm_g, w_out, ffn_norm_g, w_up,
           conv_w, conv_b, w_down):
    B, S, D = x.shape
    assert B == 1 and D == D_MODEL and w_in.shape == (D_MODEL, D_MAIN + D_TAIL)
    t = TILES
    assert all(S % n == 0 for n in (t.inproj_rows, t.pool_rows, t.outproj_rows, t.ffn_rows, t.attn_keys))
    topk = min(TOPK_MAX, S // 4)
    q_scale = HEAD_DIM ** -0.5 * LOG2E
    x2 = x.reshape(S, D)

    w_main = w_in[:, :D_MAIN].astype(BF16)
    w_tail = jnp.pad(w_in[:, D_MAIN:], ((0, 0), (0, TAIL_PAD - D_TAIL))).astype(BF16)
    nat, tr, kidx, wT = _inproj(x2, attn_norm_g.reshape(1, D), w_main, w_tail,
                                k_norm_g.reshape(1, HEAD_DIM), q_norm_g.reshape(HEAD_DIM, 1),
                                tm=t.inproj_rows, tn=t.inproj_cols, q_scale=q_scale)

    y_pool = _pool(nat, pool_w.astype(BF16), pool_scale.reshape(1, D_POOL), tm=t.pool_rows)
    keep = _index(kidx, tr, wT, T=t.dsa_queries, CK=t.index_keys, topk=topk)
    score_bound = HEAD_DIM * jnp.max(jnp.abs(q_norm_g)) * jnp.max(jnp.abs(k_norm_g)) * q_scale
    attn = functools.partial(_attn, T=t.dsa_queries, TQ=t.attn_queries, KB=t.attn_keys, SUB=t.attn_sub_keys)
    y_attn = lax.cond(score_bound < MAX_UNSHIFTED_LOG2,
                      functools.partial(attn, online=False), functools.partial(attn, online=True),
                      nat, tr, keep)

    x1 = _outproj(y_pool, y_attn, w_out.astype(BF16), x2, tm=t.outproj_rows, tn=t.outproj_cols)
    out = _ffn(x1, ffn_norm_g.reshape(1, D), w_up.astype(BF16), conv_w, conv_b.reshape(1, 2 * D_FF),
               w_down.astype(BF16), tm=t.ffn_rows, tf=t.ffn_cols)
    return out.reshape(B, S, D)
```

```python
import functools
from typing import NamedTuple

import numpy as np
import jax
import jax.numpy as jnp
from jax import lax
from jax.experimental import pallas as pl
from jax.experimental.pallas import tpu as pltpu

D_MODEL = 2048
D_POOL = 1024
POOL_WINDOWS = (2, 4, 8, 16)
POOL_GROUP = D_POOL // len(POOL_WINDOWS)
D_ATTN = 1024
HEAD_DIM = 128
N_HEADS = D_ATTN // HEAD_DIM
IDX_HEADS = 16
IDX_DIM = 64
TOPK_MAX = 256
D_FF = 5632
CONV_WIDTH = 3
EPS = 1e-6

D_MAIN = D_POOL + 3 * D_ATTN + IDX_HEADS * IDX_DIM
D_TAIL = IDX_DIM + IDX_HEADS
TAIL_PAD = 128
HALO = 16
MASK_NEG = -1e30
LOG2E = 1.4426950408889634
MAX_UNSHIFTED_LOG2 = 60.0
V7X_VMEM_BYTES = 64 * 1024 * 1024
VMEM_LIMIT_BYTES = V7X_VMEM_BYTES * 7 // 8


class _Tiles(NamedTuple):
    inproj_rows: int = 1024
    inproj_cols: int = 1024
    pool_rows: int = 1024
    dsa_queries: int = 256
    attn_queries: int = 512
    index_keys: int = 512
    attn_keys: int = 1024
    attn_sub_keys: int = 512
    outproj_rows: int = 512
    outproj_cols: int = 2048
    ffn_rows: int = 1024
    ffn_cols: int = 512


TILES = _Tiles()

F32 = jnp.float32
BF16 = jnp.bfloat16


def _params(n_axes):
    return pltpu.CompilerParams(dimension_semantics=("arbitrary",) * n_axes,
                                vmem_limit_bytes=VMEM_LIMIT_BYTES)


def _rms(xf, g):
    return xf * lax.rsqrt(jnp.mean(xf * xf, axis=-1, keepdims=True) + EPS) * g


_TN_DIMS = (((0,), (1,)), ((), ()))


def _inproj_kernel(x_ref, g_ref, wn_ref, wt_ref, wtail_ref, kg_ref, qg_ref, nat_ref, tr_ref, kidx_ref,
                   wT_ref, h_ref, *, tn, q_scale):
    j = pl.program_id(1)
    n_nat = (D_POOL + D_ATTN) // tn

    @pl.when(j == 0)
    def _():
        h = _rms(x_ref[...], g_ref[...]).astype(BF16)
        h_ref[...] = h
        tail = jnp.dot(h, wtail_ref[...], preferred_element_type=F32)
        kidx_ref[...] = tail[:, :IDX_DIM].astype(BF16)
        wT_ref[...] = tail.T[IDX_DIM:D_TAIL, :] * (IDX_HEADS ** -0.5) * (IDX_DIM ** -0.5)

    @pl.when(j < n_nat)
    def _():
        z = jnp.dot(h_ref[...], wn_ref[...], preferred_element_type=F32)

        @pl.when(j < D_POOL // tn)
        def _():
            nat_ref[...] = z.astype(BF16)

        @pl.when(j >= D_POOL // tn)
        def _():
            for c in range(tn // HEAD_DIM):
                sl = slice(c * HEAD_DIM, (c + 1) * HEAD_DIM)
                nat_ref[:, sl] = _rms(z[:, sl], kg_ref[...]).astype(BF16)

    @pl.when(j >= n_nat)
    def _():
        zT = lax.dot_general(wt_ref[...], h_ref[...], _TN_DIMS, preferred_element_type=F32)

        @pl.when(j < n_nat + D_ATTN // tn)
        def _():
            for c in range(tn // HEAD_DIM):
                sl = slice(c * HEAD_DIM, (c + 1) * HEAD_DIM)
                zc = zT[sl, :]
                inv = lax.rsqrt(jnp.mean(zc * zc, axis=0, keepdims=True) + EPS)
                tr_ref[sl, :] = (zc * inv * (qg_ref[...] * q_scale)).astype(BF16)

        @pl.when(j >= n_nat + D_ATTN // tn)
        def _():
            tr_ref[...] = zT.astype(BF16)


def _inproj(x2, g, w_main, w_tail, kg, qg_col, *, tm, tn, q_scale):
    S = x2.shape[0]
    assert D_POOL == D_ATTN == IDX_HEADS * IDX_DIM and D_ATTN % tn == 0
    per = D_ATTN // tn
    n_nat, n_tr = 2 * per, 3 * per

    def nat_tile(j):
        j = jnp.minimum(j, n_nat - 1)
        return 2 * (j // per) * per + j % per

    def tr_tile(j):
        j = jnp.maximum(j - n_nat, 0)
        sec = j // per
        return jnp.where(sec == 2, 4, 1 + 2 * sec) * per + j % per

    return pl.pallas_call(
        functools.partial(_inproj_kernel, tn=tn, q_scale=q_scale),
        out_shape=(jax.ShapeDtypeStruct((S, n_nat * tn), BF16),
                   jax.ShapeDtypeStruct((n_tr * tn, S), BF16),
                   jax.ShapeDtypeStruct((S, IDX_DIM), BF16),
                   jax.ShapeDtypeStruct((IDX_HEADS, S), F32)),
        grid=(S // tm, n_nat + n_tr),
        in_specs=[
            pl.BlockSpec((tm, D_MODEL), lambda i, j: (i, 0)),
            pl.BlockSpec((1, D_MODEL), lambda i, j: (0, 0)),
            pl.BlockSpec((D_MODEL, tn), lambda i, j: (0, nat_tile(j))),
            pl.BlockSpec((D_MODEL, tn), lambda i, j: (0, tr_tile(j))),
            pl.BlockSpec((D_MODEL, TAIL_PAD), lambda i, j: (0, 0)),
            pl.BlockSpec((1, HEAD_DIM), lambda i, j: (0, 0)),
            pl.BlockSpec((HEAD_DIM, 1), lambda i, j: (0, 0)),
        ],
        out_specs=(pl.BlockSpec((tm, tn), lambda i, j: (i, jnp.minimum(j, n_nat - 1))),
                   pl.BlockSpec((tn, tm), lambda i, j: (jnp.maximum(j - n_nat, 0), i)),
                   pl.BlockSpec((tm, IDX_DIM), lambda i, j: (i, 0)),
                   pl.BlockSpec((IDX_HEADS, tm), lambda i, j: (0, i))),
        scratch_shapes=[pltpu.VMEM((tm, D_MODEL), BF16)],
        compiler_params=_params(2),
        name="inproj",
    )(x2, g, w_main, w_main, w_tail, kg, qg_col)


def _pool_kernel(u_ref, halo_ref, pw_ref, ps_ref, o_ref):
    i = pl.program_id(0)
    tm = u_ref.shape[0]
    u = u_ref[...].astype(F32)
    halo = jnp.where(i == 0, 0.0, halo_ref[...].astype(F32))
    ext = jnp.concatenate([halo, u], axis=0)
    t = i * tm + lax.broadcasted_iota(jnp.int32, (tm, 1), 0)
    for gi, w in enumerate(POOL_WINDOWS):
        sl = slice(gi * POOL_GROUP, (gi + 1) * POOL_GROUP)
        s = ext[:, sl]
        step = 1
        while step < w:
            s = s + pltpu.roll(s, step, axis=0)
            step *= 2
        cnt = jnp.minimum(t + 1, w).astype(F32)
        d = s[HALO:, :] / cnt - u[:, sl]
        y = jnp.dot(d.astype(BF16), pw_ref[gi], preferred_element_type=F32)
        o_ref[:, sl] = (y * ps_ref[:, sl]).astype(BF16)


def _pool(zm, pool_w, pool_scale, *, tm):
    S = zm.shape[0]
    return pl.pallas_call(
        _pool_kernel,
        out_shape=jax.ShapeDtypeStruct((S, D_POOL), BF16),
        grid=(S // tm,),
        in_specs=[
            pl.BlockSpec((tm, D_POOL), lambda i: (i, 0)),
            pl.BlockSpec((HALO, D_POOL), lambda i: (jnp.maximum(i * (tm // HALO) - 1, 0), 0)),
            pl.BlockSpec((len(POOL_WINDOWS), POOL_GROUP, POOL_GROUP), lambda i: (0, 0, 0)),
            pl.BlockSpec((1, D_POOL), lambda i: (0, 0)),
        ],
        out_specs=pl.BlockSpec((tm, D_POOL), lambda i: (i, 0)),
        compiler_params=_params(1),
        name="pool",
    )(zm, zm, pool_w, pool_scale)


I16 = jnp.int16
I16_MIN = -2 ** 15
SLAB = 16
POOL = 8


def _index_kernel(kidx_ref, qiT_ref, wT_ref, keep_ref, hi_ref, lo_ref, pool_ref, L_ref, n_eq_ref, take_eq_ref,
                  *, T, CK, S, topk):
    qb = pl.program_id(0)
    nck = (qb * T + T + CK - 1) // CK
    q_pos = qb * T + lax.broadcasted_iota(jnp.int32, (1, T), 1)

    def score_chunk(off):
        kc = kidx_ref[pl.ds(off, CK), :]
        acc = None
        for h in range(IDX_HEADS):
            s = jnp.dot(kc, qiT_ref[h * IDX_DIM:(h + 1) * IDX_DIM, :], preferred_element_type=F32)
            t = wT_ref[h:h + 1, :] * jnp.maximum(s, 0.0)
            acc = t if acc is None else acc + t
        bits = lax.bitcast_convert_type(acc, jnp.int32)
        key = bits ^ ((bits >> 31) & jnp.int32(0x7FFFFFFF))
        key_pos = off + lax.broadcasted_iota(jnp.int32, (CK, 1), 0)
        key = jnp.where(key_pos <= q_pos, key, jnp.int32(-2 ** 31))
        hi_ref[pl.ds(off, CK), :] = (key >> 16).astype(I16)
        lo_ref[pl.ds(off, CK), :] = ((key & jnp.int32(0xFFFF)) + I16_MIN).astype(I16)

    def score_body(c, carry):
        for u in range(2):
            score_chunk(pl.multiple_of((2 * c + u) * CK, CK))
        return carry

    lax.fori_loop(0, (nck + 1) // 2, score_body, 0)

    none16 = jnp.full((), I16_MIN, I16)

    def count_ge(ref, cand, trips):
        c16 = cand.astype(I16)

        def count_body(c, acc):
            off = pl.multiple_of(c * CK, CK)
            ge = jnp.where(ref[pl.ds(off, CK), :] >= c16, jnp.ones((), I16), jnp.zeros((), I16))
            ge = ge.reshape(CK // SLAB, SLAB, T)
            parts = [ge[r] for r in range(CK // SLAB)]
            while len(parts) > 1:
                parts = [parts[i] + parts[i + 1] for i in range(0, len(parts), 2)]
            return acc + parts[0]

        acc = lax.fori_loop(0, trips, count_body, jnp.zeros((SLAB, T), I16))
        return acc.astype(jnp.int32).sum(axis=0, keepdims=True)

    def kth_largest(ref, k, trips):
        def bit_body(i, carry):
            v, n_ge_v, n_gt_v = carry
            cand = v + (jnp.int32(1) << (15 - i))
            n_cand = count_ge(ref, cand, trips)
            ok = n_cand >= k
            return jnp.where(ok, cand, v), jnp.where(ok, n_cand, n_ge_v), jnp.where(ok, n_gt_v, n_cand)

        zero = jnp.zeros((1, T), jnp.int32)
        return lax.fori_loop(0, 16, bit_body, (zero + I16_MIN, zero + trips * CK, zero))

    H, n_ge, n_gt = kth_largest(hi_ref, topk, nck)
    H16 = H.astype(I16)
    need = topk - n_gt

    def pool_body(c, carry):
        for g in range(CK // (SLAB * POOL)):
            a = b = jnp.full((SLAB, T), I16_MIN, I16)
            for r in range(POOL):
                rows = pl.ds(pl.multiple_of(c * CK + (g * POOL + r) * SLAB, SLAB), SLAB)
                x = jnp.where(hi_ref[rows, :] == H16, lo_ref[rows, :], none16)
                up = x > a
                t = jnp.where(up, a, x)
                a = jnp.where(up, x, a)
                b = jnp.where(t > b, t, b)
            base = pl.multiple_of(c * POOL_ROWS + g * 2 * SLAB, 2 * SLAB)
            pool_ref[pl.ds(base, SLAB), :] = a
            pool_ref[pl.ds(base + SLAB, SLAB), :] = b
        return carry

    POOL_ROWS = 2 * CK // POOL
    per_trip = CK // POOL_ROWS
    pool_trips = (nck + per_trip - 1) // per_trip
    lax.fori_loop(0, nck, pool_body, 0)

    def pad_body(c, carry):
        pool_ref[pl.ds(pl.multiple_of(c * POOL_ROWS, POOL_ROWS), POOL_ROWS), :] = jnp.full((POOL_ROWS, T), I16_MIN, I16)
        return carry

    lax.fori_loop(nck, pool_trips * per_trip, pad_body, 0)

    n_tied = n_ge - n_gt
    n_kept = count_ge(pool_ref, jnp.full((1, T), I16_MIN + 1, jnp.int32), pool_trips)
    lost = jnp.sum(jnp.where((n_kept == n_tied) | (H == I16_MIN), 0, 1))

    def second_level(ref, trips):
        L, n_ge_l, n_gt_l = kth_largest(ref, need, trips)
        L_ref[...] = L
        n_eq_ref[...] = n_ge_l - n_gt_l
        take_eq_ref[...] = need - n_gt_l

    @pl.when(lost == 0)
    def _():
        second_level(pool_ref, pool_trips)

    @pl.when(lost != 0)
    def _():
        def mask_body(c, carry):
            off = pl.multiple_of(c * CK, CK)
            lo_ref[pl.ds(off, CK), :] = jnp.where(hi_ref[pl.ds(off, CK), :] == H16, lo_ref[pl.ds(off, CK), :],
                                                  none16)
            return carry

        lax.fori_loop(0, nck, mask_body, 0)
        second_level(lo_ref, nck)

    L16 = L_ref[...].astype(I16)
    take_eq = take_eq_ref[...]
    surplus = jnp.sum(jnp.where((n_eq_ref[...] > take_eq) & (H > I16_MIN), 1, 0))

    def emit(off, sel):
        sel = sel & (hi_ref[pl.ds(off, CK), :] > none16)
        keep_ref[0, pl.ds(off, CK), :] = jnp.where(sel, jnp.ones((), BF16), jnp.zeros((), BF16))

    @pl.when(surplus == 0)
    def _():
        def emit_body(c, carry):
            off = pl.multiple_of(c * CK, CK)
            hi = hi_ref[pl.ds(off, CK), :]
            emit(off, (hi > H16) | ((hi == H16) & (lo_ref[pl.ds(off, CK), :] >= L16)))
            return carry

        lax.fori_loop(0, nck, emit_body, 0)

    @pl.when(surplus != 0)
    def _():
        tri = (lax.broadcasted_iota(jnp.int32, (CK, CK), 0) >= lax.broadcasted_iota(jnp.int32, (CK, CK), 1))
        tri = jnp.where(tri, 1.0, 0.0).astype(BF16)
        take_f = take_eq.astype(F32)

        def emit_body(c, seen):
            off = pl.multiple_of(c * CK, CK)
            hi = hi_ref[pl.ds(off, CK), :]
            lo = lo_ref[pl.ds(off, CK), :]
            eq = (hi == H16) & (lo == L16) & (hi > none16)
            rank = seen + jnp.dot(tri, jnp.where(eq, jnp.ones((), BF16), jnp.zeros((), BF16)),
                                  preferred_element_type=F32)
            first = jnp.where(rank <= take_f, 1.0, 0.0).astype(BF16) > jnp.zeros((), BF16)
            emit(off, (hi > H16) | ((hi == H16) & (lo > L16)) | (eq & first))
            return rank[CK - 1:CK, :]

        lax.fori_loop(0, nck, emit_body, jnp.zeros((1, T), F32))

    def fill_body(c, carry):
        off = pl.multiple_of(c * CK, CK)
        keep_ref[0, pl.ds(off, CK), :] = jnp.zeros((CK, T), BF16)
        return carry

    lax.fori_loop(nck, S // CK, fill_body, 0)


def _index(kidx, tr, wT, *, T, CK, topk):
    S = kidx.shape[0]
    assert S % (2 * CK) == 0 and CK % T == 0 and (S // CK) % (POOL // 2) == 0
    return pl.pallas_call(
        functools.partial(_index_kernel, T=T, CK=CK, S=S, topk=topk),
        out_shape=jax.ShapeDtypeStruct((S // T, S, T), BF16),
        grid=(S // T,),
        in_specs=[
            pl.BlockSpec((S, IDX_DIM), lambda q: (0, 0)),
            pl.BlockSpec((IDX_HEADS * IDX_DIM, T), lambda q: (2, q)),
            pl.BlockSpec((IDX_HEADS, T), lambda q: (0, q)),
        ],
        out_specs=pl.BlockSpec((1, S, T), lambda q: (q, 0, 0)),
        scratch_shapes=[pltpu.VMEM((S, T), I16), pltpu.VMEM((S, T), I16),
                        pltpu.VMEM((2 * S // POOL, T), I16)] + [pltpu.VMEM((1, T), jnp.int32)] * 3,
        compiler_params=_params(1),
        name="index",
    )(kidx, tr, wT)


def _attn_kernel(qb_ref, kb_ref, kn_ref, qT_ref, vT_ref, *rest, T, KB, SUB, online):
    *keep_refs, o_ref, m_ref, l_ref, acc_ref = rest
    i = pl.program_id(0)
    qb = qb_ref[i]
    kb = kb_ref[i]
    TQ = T * len(keep_refs)

    @pl.when(kb == 0)
    def _():
        m_ref[...] = jnp.full(m_ref.shape, MASK_NEG, F32)
        l_ref[...] = jnp.zeros(l_ref.shape, F32)
        acc_ref[...] = jnp.zeros(acc_ref.shape, F32)

    items = [(kb0, h) for h in range(N_HEADS) for kb0 in range(0, KB, SUB)]

    def qk(item):
        kb0, h = item
        hs = slice(h * HEAD_DIM, (h + 1) * HEAD_DIM)
        return jnp.dot(kn_ref[kb0:kb0 + SUB, hs], qT_ref[hs, :], preferred_element_type=F32)

    ahead = 4
    pending = [qk(it) for it in items[:ahead]]
    def additive(kb0):
        keep = jnp.concatenate([r[0, kb0:kb0 + SUB, :] for r in keep_refs], axis=1)
        return jnp.where(keep > jnp.zeros((), BF16), jnp.zeros((), BF16), jnp.full((), MASK_NEG, BF16)).astype(F32)

    bias = {kb0: additive(kb0) for kb0 in range(0, KB, SUB)}
    for n, (kb0, h) in enumerate(items):
        hs = slice(h * HEAD_DIM, (h + 1) * HEAD_DIM)
        s = bias[kb0] + pending.pop(0)
        if n + ahead < len(items):
            pending.append(qk(items[n + ahead]))
        if online:
            m_old = m_ref[h]
            m_new = jnp.maximum(m_old, s.max(axis=0, keepdims=True))
            alpha = jnp.exp2(m_old - m_new)
            p = jnp.exp2(s - m_new)
            l_ref[h] = alpha * l_ref[h] + p.sum(axis=0, keepdims=True)
            pv = jnp.dot(vT_ref[hs, kb0:kb0 + SUB], p.astype(BF16), preferred_element_type=F32)
            acc_ref[h] = alpha * acc_ref[h] + pv
            m_ref[h] = m_new
        else:
            p = jnp.exp2(s)
            l_ref[h] += p.sum(axis=0, keepdims=True)
            acc_ref[h] += jnp.dot(vT_ref[hs, kb0:kb0 + SUB], p.astype(BF16), preferred_element_type=F32)

    @pl.when(kb == (qb * TQ + TQ - 1) // KB)
    def _():
        for h in range(N_HEADS):
            o_ref[:, h * HEAD_DIM:(h + 1) * HEAD_DIM] = (acc_ref[h] / l_ref[h]).T.astype(BF16)


def _attn(nat, tr, keep, *, T, TQ, KB, SUB, online):
    S = nat.shape[0]
    groups = TQ // T
    pairs = [(q, k) for q in range(S // TQ) for k in range((q * TQ + TQ - 1) // KB + 1)]
    qb_ids = jnp.asarray(np.array([p[0] for p in pairs], np.int32))
    kb_ids = jnp.asarray(np.array([p[1] for p in pairs], np.int32))
    grid_spec = pltpu.PrefetchScalarGridSpec(
        num_scalar_prefetch=2,
        grid=(len(pairs),),
        in_specs=[
            pl.BlockSpec((KB, D_ATTN), lambda i, qb, kb: (kb[i], 1)),
            pl.BlockSpec((D_ATTN, TQ), lambda i, qb, kb: (0, qb[i])),
            pl.BlockSpec((D_ATTN, KB), lambda i, qb, kb: (1, kb[i])),
        ] + [pl.BlockSpec((1, KB, T), lambda i, qb, kb, g=g: (qb[i] * groups + g, kb[i], 0)) for g in range(groups)],
        out_specs=pl.BlockSpec((TQ, D_ATTN), lambda i, qb, kb: (qb[i], 0)),
        scratch_shapes=[pltpu.VMEM((N_HEADS, 1, TQ), F32), pltpu.VMEM((N_HEADS, 1, TQ), F32),
                        pltpu.VMEM((N_HEADS, HEAD_DIM, TQ), F32)],
    )
    return pl.pallas_call(
        functools.partial(_attn_kernel, T=T, KB=KB, SUB=SUB, online=online),
        out_shape=jax.ShapeDtypeStruct((S, D_ATTN), BF16),
        grid_spec=grid_spec,
        compiler_params=_params(1),
        name="attn_online" if online else "attn",
    )(qb_ids, kb_ids, nat, tr, tr, *([keep] * groups))


def _outproj_kernel(yp_ref, ya_ref, wp_ref, wa_ref, x_ref, o_ref):
    acc = jnp.dot(yp_ref[...], wp_ref[...], preferred_element_type=F32)
    acc = acc + jnp.dot(ya_ref[...], wa_ref[...], preferred_element_type=F32)
    o_ref[...] = x_ref[...] + acc


def _outproj(yp, ya, w_out, x2, *, tm, tn):
    S = x2.shape[0]
    n_j = D_MODEL // tn
    return pl.pallas_call(
        _outproj_kernel,
        out_shape=jax.ShapeDtypeStruct((S, D_MODEL), F32),
        grid=(S // tm, n_j),
        in_specs=[
            pl.BlockSpec((tm, D_POOL), lambda i, j: (i, 0)),
            pl.BlockSpec((tm, D_ATTN), lambda i, j: (i, 0)),
            pl.BlockSpec((D_POOL, tn), lambda i, j: (0, j)),
            pl.BlockSpec((D_ATTN, tn), lambda i, j: (D_POOL // D_ATTN, j)),
            pl.BlockSpec((tm, tn), lambda i, j: (i, j)),
        ],
        out_specs=pl.BlockSpec((tm, tn), lambda i, j: (i, j)),
        compiler_params=_params(2),
        name="outproj",
    )(yp, ya, w_out, w_out, x2)


def _ffn_kernel(x_ref, halo_ref, g_ref, wg_ref, wv_ref, cwg_ref, cwv_ref, cbg_ref, cbv_ref, wd_ref,
                o_ref, h_ref):
    i = pl.program_id(0)
    f = pl.program_id(1)

    @pl.when(f == 0)
    def _():
        hh = jnp.where(i == 0, 0.0, _rms(halo_ref[...], g_ref[...]))
        h_ref[0:HALO, :] = hh.astype(BF16)
        x = x_ref[...]
        h_ref[HALO:, :] = _rms(x, g_ref[...]).astype(BF16)
        o_ref[...] = x

    h = h_ref[...]

    def conv(w_ref, cw_ref, cb_ref):
        up = jnp.dot(h, w_ref[...], preferred_element_type=F32)
        c = (cw_ref[0:1, :] * pltpu.roll(up, 2, axis=0) + cw_ref[1:2, :] * pltpu.roll(up, 1, axis=0)
             + cw_ref[2:3, :] * up)
        return cb_ref[...] + c[HALO:, :]

    cg = conv(wg_ref, cwg_ref, cbg_ref)
    cv = conv(wv_ref, cwv_ref, cbv_ref)
    act = cg * (1.0 / (1.0 + jnp.exp(-cg))) * cv
    o_ref[...] += jnp.dot(act.astype(BF16), wd_ref[...], preferred_element_type=F32)


def _ffn(x1, g, w_up, conv_w, conv_b, w_down, *, tm, tf):
    S = x1.shape[0]
    n_f = D_FF // tf
    return pl.pallas_call(
        _ffn_kernel,
        out_shape=jax.ShapeDtypeStruct((S, D_MODEL), F32),
        grid=(S // tm, n_f),
        in_specs=[
            pl.BlockSpec((tm, D_MODEL), lambda i, f: (i, 0)),
            pl.BlockSpec((HALO, D_MODEL), lambda i, f: (jnp.maximum(i * (tm // HALO) - 1, 0), 0)),
            pl.BlockSpec((1, D_MODEL), lambda i, f: (0, 0)),
            pl.BlockSpec((D_MODEL, tf), lambda i, f: (0, f)),
            pl.BlockSpec((D_MODEL, tf), lambda i, f: (0, f + n_f)),
            pl.BlockSpec((CONV_WIDTH, tf), lambda i, f: (0, f)),
            pl.BlockSpec((CONV_WIDTH, tf), lambda i, f: (0, f + n_f)),
            pl.BlockSpec((1, tf), lambda i, f: (0, f)),
            pl.BlockSpec((1, tf), lambda i, f: (0, f + n_f)),
            pl.BlockSpec((tf, D_MODEL), lambda i, f: (f, 0)),
        ],
        out_specs=pl.BlockSpec((tm, D_MODEL), lambda i, f: (i, 0)),
        scratch_shapes=[pltpu.VMEM((HALO + tm, D_MODEL), BF16)],
        compiler_params=_params(2),
        name="ffn",
    )(x1, x1, g, w_up, w_up, conv_w, conv_w, conv_b, conv_b, w_down)


def kernel(x, attn_norm_g, w_in, pool_w, pool_scale, q_norm_g, k_norm_g, w_out, ffn_norm_g, w_up,
           conv_w, conv_b, w_down):
    B, S, D = x.shape
    assert B == 1 and D == D_MODEL and w_in.shape == (D_MODEL, D_MAIN + D_TAIL)
    t = TILES
    assert all(S % n == 0 for n in (t.inproj_rows, t.pool_rows, t.outproj_rows, t.ffn_rows, t.attn_keys))
    topk = min(TOPK_MAX, S // 4)
    q_scale = HEAD_DIM ** -0.5 * LOG2E
    x2 = x.reshape(S, D)

    w_main = w_in[:, :D_MAIN].astype(BF16)
    w_tail = jnp.pad(w_in[:, D_MAIN:], ((0, 0), (0, TAIL_PAD - D_TAIL))).astype(BF16)
    nat, tr, kidx, wT = _inproj(x2, attn_norm_g.reshape(1, D), w_main, w_tail,
                                k_norm_g.reshape(1, HEAD_DIM), q_norm_g.reshape(HEAD_DIM, 1),
                                tm=t.inproj_rows, tn=t.inproj_cols, q_scale=q_scale)

    y_pool = _pool(nat, pool_w.astype(BF16), pool_scale.reshape(1, D_POOL), tm=t.pool_rows)
    keep = _index(kidx, tr, wT, T=t.dsa_queries, CK=t.index_keys, topk=topk)
    score_bound = HEAD_DIM * jnp.max(jnp.abs(q_norm_g)) * jnp.max(jnp.abs(k_norm_g)) * q_scale
    attn = functools.partial(_attn, T=t.dsa_queries, TQ=t.attn_queries, KB=t.attn_keys, SUB=t.attn_sub_keys)
    y_attn = lax.cond(score_bound < MAX_UNSHIFTED_LOG2,
                      functools.partial(attn, online=False), functools.partial(attn, online=True),
                      nat, tr, keep)

    x1 = _outproj(y_pool, y_attn, w_out.astype(BF16), x2, tm=t.outproj_rows, tn=t.outproj_cols)
    out = _ffn(x1, ffn_norm_g.reshape(1, D), w_up.astype(BF16), conv_w, conv_b.reshape(1, 2 * D_FF),
               w_down.astype(BF16), tm=t.ffn_rows, tf=t.ffn_cols)
    return out.reshape(B, S, D)
```

```python
import functools
from typing import NamedTuple

import numpy as np
import jax
import jax.numpy as jnp
from jax import lax
from jax.experimental import pallas as pl
from jax.experimental.pallas import tpu as pltpu

D_MODEL = 2048
D_POOL = 1024
POOL_WINDOWS = (2, 4, 8, 16)
POOL_GROUP = D_POOL // len(POOL_WINDOWS)
D_ATTN = 1024
HEAD_DIM = 128
N_HEADS = D_ATTN // HEAD_DIM
IDX_HEADS = 16
IDX_DIM = 64
TOPK_MAX = 256
D_FF = 5632
CONV_WIDTH = 3
EPS = 1e-6

D_MAIN = D_POOL + 3 * D_ATTN + IDX_HEADS * IDX_DIM
D_TAIL = IDX_DIM + IDX_HEADS
TAIL_PAD = 128
HALO = 16
MASK_NEG = -1e30
LOG2E = 1.4426950408889634
MAX_UNSHIFTED_LOG2 = 60.0
V7X_VMEM_BYTES = 64 * 1024 * 1024
VMEM_LIMIT_BYTES = V7X_VMEM_BYTES * 7 // 8


class _Tiles(NamedTuple):
    inproj_rows: int = 1024
    inproj_cols: int = 1024
    pool_rows: int = 1024
    dsa_queries: int = 256
    attn_queries: int = 512
    index_keys: int = 512
    attn_keys: int = 1024
    attn_sub_keys: int = 512
    outproj_rows: int = 512
    outproj_cols: int = 2048
    ffn_rows: int = 1024
    ffn_cols: int = 512


TILES = _Tiles()

F32 = jnp.float32
BF16 = jnp.bfloat16


def _params(n_axes):
    return pltpu.CompilerParams(dimension_semantics=("arbitrary",) * n_axes,
                                vmem_limit_bytes=VMEM_LIMIT_BYTES)


def _rms(xf, g):
    return xf * lax.rsqrt(jnp.mean(xf * xf, axis=-1, keepdims=True) + EPS) * g


_TN_DIMS = (((0,), (1,)), ((), ()))


def _inproj_kernel(x_ref, g_ref, wn_ref, wt_ref, wtail_ref, kg_ref, qg_ref, nat_ref, tr_ref, kidx_ref,
                   wT_ref, h_ref, *, tn, q_scale):
    j = pl.program_id(1)
    n_nat = (D_POOL + D_ATTN) // tn

    @pl.when(j == 0)
    def _():
        h = _rms(x_ref[...], g_ref[...]).astype(BF16)
        h_ref[...] = h
        tail = jnp.dot(h, wtail_ref[...], preferred_element_type=F32)
        kidx_ref[...] = tail[:, :IDX_DIM].astype(BF16)
        wT_ref[...] = tail.T[IDX_DIM:D_TAIL, :] * (IDX_HEADS ** -0.5) * (IDX_DIM ** -0.5)

    @pl.when(j < n_nat)
    def _():
        z = jnp.dot(h_ref[...], wn_ref[...], preferred_element_type=F32)

        @pl.when(j < D_POOL // tn)
        def _():
            nat_ref[...] = z.astype(BF16)

        @pl.when(j >= D_POOL // tn)
        def _():
            for c in range(tn // HEAD_DIM):
                sl = slice(c * HEAD_DIM, (c + 1) * HEAD_DIM)
                nat_ref[:, sl] = _rms(z[:, sl], kg_ref[...]).astype(BF16)

    @pl.when(j >= n_nat)
    def _():
        zT = lax.dot_general(wt_ref[...], h_ref[...], _TN_DIMS, preferred_element_type=F32)

        @pl.when(j < n_nat + D_ATTN // tn)
        def _():
            for c in range(tn // HEAD_DIM):
                sl = slice(c * HEAD_DIM, (c + 1) * HEAD_DIM)
                zc = zT[sl, :]
                inv = lax.rsqrt(jnp.mean(zc * zc, axis=0, keepdims=True) + EPS)
                tr_ref[sl, :] = (zc * inv * (qg_ref[...] * q_scale)).astype(BF16)

        @pl.when(j >= n_nat + D_ATTN // tn)
        def _():
            tr_ref[...] = zT.astype(BF16)


def _inproj(x2, g, w_main, w_tail, kg, qg_col, *, tm, tn, q_scale):
    S = x2.shape[0]
    assert D_POOL == D_ATTN == IDX_HEADS * IDX_DIM and D_ATTN % tn == 0
    per = D_ATTN // tn
    n_nat, n_tr = 2 * per, 3 * per

    def nat_tile(j):
        j = jnp.minimum(j, n_nat - 1)
        return 2 * (j // per) * per + j % per

    def tr_tile(j):
        j = jnp.maximum(j - n_nat, 0)
        sec = j // per
        return jnp.where(sec == 2, 4, 1 + 2 * sec) * per + j % per

    return pl.pallas_call(
        functools.partial(_inproj_kernel, tn=tn, q_scale=q_scale),
        out_shape=(jax.ShapeDtypeStruct((S, n_nat * tn), BF16),
                   jax.ShapeDtypeStruct((n_tr * tn, S), BF16),
                   jax.ShapeDtypeStruct((S, IDX_DIM), BF16),
                   jax.ShapeDtypeStruct((IDX_HEADS, S), F32)),
        grid=(S // tm, n_nat + n_tr),
        in_specs=[
            pl.BlockSpec((tm, D_MODEL), lambda i, j: (i, 0)),
            pl.BlockSpec((1, D_MODEL), lambda i, j: (0, 0)),
            pl.BlockSpec((D_MODEL, tn), lambda i, j: (0, nat_tile(j))),
            pl.BlockSpec((D_MODEL, tn), lambda i, j: (0, tr_tile(j))),
            pl.BlockSpec((D_MODEL, TAIL_PAD), lambda i, j: (0, 0)),
            pl.BlockSpec((1, HEAD_DIM), lambda i, j: (0, 0)),
            pl.BlockSpec((HEAD_DIM, 1), lambda i, j: (0, 0)),
        ],
        out_specs=(pl.BlockSpec((tm, tn), lambda i, j: (i, jnp.minimum(j, n_nat - 1))),
                   pl.BlockSpec((tn, tm), lambda i, j: (jnp.maximum(j - n_nat, 0), i)),
                   pl.BlockSpec((tm, IDX_DIM), lambda i, j: (i, 0)),
                   pl.BlockSpec((IDX_HEADS, tm), lambda i, j: (0, i))),
        scratch_shapes=[pltpu.VMEM((tm, D_MODEL), BF16)],
        compiler_params=_params(2),
        name="inproj",
    )(x2, g, w_main, w_main, w_tail, kg, qg_col)


def _pool_kernel(u_ref, halo_ref, pw_ref, ps_ref, o_ref):
    i = pl.program_id(0)
    tm = u_ref.shape[0]
    u = u_ref[...].astype(F32)
    halo = jnp.where(i == 0, 0.0, halo_ref[...].astype(F32))
    ext = jnp.concatenate([halo, u], axis=0)
    t = i * tm + lax.broadcasted_iota(jnp.int32, (tm, 1), 0)
    for gi, w in enumerate(POOL_WINDOWS):
        sl = slice(gi * POOL_GROUP, (gi + 1) * POOL_GROUP)
        s = ext[:, sl]
        step = 1
        while step < w:
            s = s + pltpu.roll(s, step, axis=0)
            step *= 2
        cnt = jnp.minimum(t + 1, w).astype(F32)
        d = s[HALO:, :] / cnt - u[:, sl]
        y = jnp.dot(d.astype(BF16), pw_ref[gi], preferred_element_type=F32)
        o_ref[:, sl] = (y * ps_ref[:, sl]).astype(BF16)


def _pool(zm, pool_w, pool_scale, *, tm):
    S = zm.shape[0]
    return pl.pallas_call(
        _pool_kernel,
        out_shape=jax.ShapeDtypeStruct((S, D_POOL), BF16),
        grid=(S // tm,),
        in_specs=[
            pl.BlockSpec((tm, D_POOL), lambda i: (i, 0)),
            pl.BlockSpec((HALO, D_POOL), lambda i: (jnp.maximum(i * (tm // HALO) - 1, 0), 0)),
            pl.BlockSpec((len(POOL_WINDOWS), POOL_GROUP, POOL_GROUP), lambda i: (0, 0, 0)),
            pl.BlockSpec((1, D_POOL), lambda i: (0, 0)),
        ],
        out_specs=pl.BlockSpec((tm, D_POOL), lambda i: (i, 0)),
        compiler_params=_params(1),
        name="pool",
    )(zm, zm, pool_w, pool_scale)


I16 = jnp.int16
I16_MIN = -2 ** 15
SLAB = 16
POOL = 8


def _index_kernel(kidx_ref, qiT_ref, wT_ref, keep_ref, hi_ref, lo_ref, pool_ref, L_ref, n_eq_ref, take_eq_ref,
                  *, T, CK, S, topk):
    qb = pl.program_id(0)
    nck = (qb * T + T + CK - 1) // CK
    q_pos = qb * T + lax.broadcasted_iota(jnp.int32, (1, T), 1)

    def score_chunk(off):
        kc = kidx_ref[pl.ds(off, CK), :]
        acc = None
        for h in range(IDX_HEADS):
            s = jnp.dot(kc, qiT_ref[h * IDX_DIM:(h + 1) * IDX_DIM, :], preferred_element_type=F32)
            t = wT_ref[h:h + 1, :] * jnp.maximum(s, 0.0)
            acc = t if acc is None else acc + t
        bits = lax.bitcast_convert_type(acc, jnp.int32)
        key = bits ^ ((bits >> 31) & jnp.int32(0x7FFFFFFF))
        key_pos = off + lax.broadcasted_iota(jnp.int32, (CK, 1), 0)
        key = jnp.where(key_pos <= q_pos, key, jnp.int32(-2 ** 31))
        hi16 = (key >> 16).astype(I16)
        hi_ref[pl.ds(off, CK), :] = hi16
        lo_ref[pl.ds(off, CK), :] = ((key & jnp.int32(0xFFFF)) + I16_MIN).astype(I16)
        return slab_count(hi16, jnp.zeros((), I16))

    def slab_count(x16, c16):
        ge = jnp.where(x16 >= c16, jnp.ones((), I16), jnp.zeros((), I16)).reshape(CK // SLAB, SLAB, T)
        parts = [ge[r] for r in range(CK // SLAB)]
        while len(parts) > 1:
            parts = [parts[i] + parts[i + 1] for i in range(0, len(parts), 2)]
        return parts[0]

    def total(acc16):
        return acc16.astype(jnp.int32).sum(axis=0, keepdims=True)

    def score_body(c, acc):
        for u in range(2):
            acc = acc + score_chunk(pl.multiple_of((2 * c + u) * CK, CK))
        return acc

    n_nonneg = total(lax.fori_loop(0, (nck + 1) // 2, score_body, jnp.zeros((SLAB, T), I16)))

    none16 = jnp.full((), I16_MIN, I16)

    def count_ge(ref, cand, trips):
        c16 = cand.astype(I16)

        def count_body(c, acc):
            off = pl.multiple_of(c * CK, CK)
            return acc + slab_count(ref[pl.ds(off, CK), :], c16)

        return total(lax.fori_loop(0, trips, count_body, jnp.zeros((SLAB, T), I16)))

    def kth_largest(ref, k, trips, n_first=None):
        def step(i, carry, n_cand=None):
            v, n_ge_v, n_gt_v = carry
            cand = v + (jnp.int32(1) << (15 - i))
            if n_cand is None:
                n_cand = count_ge(ref, cand, trips)
            ok = n_cand >= k
            return jnp.where(ok, cand, v), jnp.where(ok, n_cand, n_ge_v), jnp.where(ok, n_gt_v, n_cand)

        zero = jnp.zeros((1, T), jnp.int32)
        carry = (zero + I16_MIN, zero + trips * CK, zero)
        if n_first is None:
            return lax.fori_loop(0, 16, step, carry)
        return lax.fori_loop(1, 16, step, step(0, carry, n_first))

    H, n_ge, n_gt = kth_largest(hi_ref, topk, nck, n_first=n_nonneg)
    H16 = H.astype(I16)
    need = topk - n_gt

    def pool_body(c, carry):
        for g in range(CK // (SLAB * POOL)):
            a = b = jnp.full((SLAB, T), I16_MIN, I16)
            for r in range(POOL):
                rows = pl.ds(pl.multiple_of(c * CK + (g * POOL + r) * SLAB, SLAB), SLAB)
                x = jnp.where(hi_ref[rows, :] == H16, lo_ref[rows, :], none16)
                up = x > a
                t = jnp.where(up, a, x)
                a = jnp.where(up, x, a)
                b = jnp.where(t > b, t, b)
            base = pl.multiple_of(c * POOL_ROWS + g * 2 * SLAB, 2 * SLAB)
            pool_ref[pl.ds(base, SLAB), :] = a
            pool_ref[pl.ds(base + SLAB, SLAB), :] = b
        return carry

    POOL_ROWS = 2 * CK // POOL
    per_trip = CK // POOL_ROWS
    pool_trips = (nck + per_trip - 1) // per_trip
    lax.fori_loop(0, nck, pool_body, 0)

    def pad_body(c, carry):
        pool_ref[pl.ds(pl.multiple_of(c * POOL_ROWS, POOL_ROWS), POOL_ROWS), :] = jnp.full((POOL_ROWS, T), I16_MIN, I16)
        return carry

    lax.fori_loop(nck, pool_trips * per_trip, pad_body, 0)

    n_tied = n_ge - n_gt
    n_kept = count_ge(pool_ref, jnp.full((1, T), I16_MIN + 1, jnp.int32), pool_trips)
    lost = jnp.sum(jnp.where((n_kept == n_tied) | (H == I16_MIN), 0, 1))

    def second_level(ref, trips):
        L, n_ge_l, n_gt_l = kth_largest(ref, need, trips)
        L_ref[...] = L
        n_eq_ref[...] = n_ge_l - n_gt_l
        take_eq_ref[...] = need - n_gt_l

    @pl.when(lost == 0)
    def _():
        second_level(pool_ref, pool_trips)

    @pl.when(lost != 0)
    def _():
        def mask_body(c, carry):
            off = pl.multiple_of(c * CK, CK)
            lo_ref[pl.ds(off, CK), :] = jnp.where(hi_ref[pl.ds(off, CK), :] == H16, lo_ref[pl.ds(off, CK), :],
                                                  none16)
            return carry

        lax.fori_loop(0, nck, mask_body, 0)
        second_level(lo_ref, nck)

    L16 = L_ref[...].astype(I16)
    take_eq = take_eq_ref[...]
    surplus = jnp.sum(jnp.where((n_eq_ref[...] > take_eq) & (H > I16_MIN), 1, 0))

    def emit(off, sel):
        sel = sel & (hi_ref[pl.ds(off, CK), :] > none16)
        keep_ref[0, pl.ds(off, CK), :] = jnp.where(sel, jnp.ones((), BF16), jnp.zeros((), BF16))

    @pl.when(surplus == 0)
    def _():
        def emit_body(c, carry):
            off = pl.multiple_of(c * CK, CK)
            hi = hi_ref[pl.ds(off, CK), :]
            emit(off, (hi > H16) | ((hi == H16) & (lo_ref[pl.ds(off, CK), :] >= L16)))
            return carry

        lax.fori_loop(0, nck, emit_body, 0)

    @pl.when(surplus != 0)
    def _():
        tri = (lax.broadcasted_iota(jnp.int32, (CK, CK), 0) >= lax.broadcasted_iota(jnp.int32, (CK, CK), 1))
        tri = jnp.where(tri, 1.0, 0.0).astype(BF16)
        take_f = take_eq.astype(F32)

        def emit_body(c, seen):
            off = pl.multiple_of(c * CK, CK)
            hi = hi_ref[pl.ds(off, CK), :]
            lo = lo_ref[pl.ds(off, CK), :]
            eq = (hi == H16) & (lo == L16) & (hi > none16)
            rank = seen + jnp.dot(tri, jnp.where(eq, jnp.ones((), BF16), jnp.zeros((), BF16)),
                                  preferred_element_type=F32)
            first = jnp.where(rank <= take_f, 1.0, 0.0).astype(BF16) > jnp.zeros((), BF16)
            emit(off, (hi > H16) | ((hi == H16) & (lo > L16)) | (eq & first))
            return rank[CK - 1:CK, :]

        lax.fori_loop(0, nck, emit_body, jnp.zeros((1, T), F32))

    def fill_body(c, carry):
        off = pl.multiple_of(c * CK, CK)
        keep_ref[0, pl.ds(off, CK), :] = jnp.zeros((CK, T), BF16)
        return carry

    lax.fori_loop(nck, S // CK, fill_body, 0)


def _index(kidx, tr, wT, *, T, CK, topk):
    S = kidx.shape[0]
    assert S % (2 * CK) == 0 and CK % T == 0 and (S // CK) % (POOL // 2) == 0
    return pl.pallas_call(
        functools.partial(_index_kernel, T=T, CK=CK, S=S, topk=topk),
        out_shape=jax.ShapeDtypeStruct((S // T, S, T), BF16),
        grid=(S // T,),
        in_specs=[
            pl.BlockSpec((S, IDX_DIM), lambda q: (0, 0)),
            pl.BlockSpec((IDX_HEADS * IDX_DIM, T), lambda q: (2, q)),
            pl.BlockSpec((IDX_HEADS, T), lambda q: (0, q)),
        ],
        out_specs=pl.BlockSpec((1, S, T), lambda q: (q, 0, 0)),
        scratch_shapes=[pltpu.VMEM((S, T), I16), pltpu.VMEM((S, T), I16),
                        pltpu.VMEM((2 * S // POOL, T), I16)] + [pltpu.VMEM((1, T), jnp.int32)] * 3,
        compiler_params=_params(1),
        name="index",
    )(kidx, tr, wT)


def _attn_kernel(qb_ref, kb_ref, kn_ref, qT_ref, vT_ref, *rest, T, KB, SUB, online):
    *keep_refs, o_ref, m_ref, l_ref, acc_ref = rest
    i = pl.program_id(0)
    qb = qb_ref[i]
    kb = kb_ref[i]
    TQ = T * len(keep_refs)

    @pl.when(kb == 0)
    def _():
        m_ref[...] = jnp.full(m_ref.shape, MASK_NEG, F32)
        l_ref[...] = jnp.zeros(l_ref.shape, F32)
        acc_ref[...] = jnp.zeros(acc_ref.shape, F32)

    items = [(kb0, h) for h in range(N_HEADS) for kb0 in range(0, KB, SUB)]

    def qk(item):
        kb0, h = item
        hs = slice(h * HEAD_DIM, (h + 1) * HEAD_DIM)
        return jnp.dot(kn_ref[kb0:kb0 + SUB, hs], qT_ref[hs, :], preferred_element_type=F32)

    ahead = 4
    pending = [qk(it) for it in items[:ahead]]
    def additive(kb0):
        keep = jnp.concatenate([r[0, kb0:kb0 + SUB, :] for r in keep_refs], axis=1)
        return jnp.where(keep > jnp.zeros((), BF16), jnp.zeros((), BF16), jnp.full((), MASK_NEG, BF16)).astype(F32)

    bias = {kb0: additive(kb0) for kb0 in range(0, KB, SUB)}
    for n, (kb0, h) in enumerate(items):
        hs = slice(h * HEAD_DIM, (h + 1) * HEAD_DIM)
        s = bias[kb0] + pending.pop(0)
        if n + ahead < len(items):
            pending.append(qk(items[n + ahead]))
        if online:
            m_old = m_ref[h]
            m_new = jnp.maximum(m_old, s.max(axis=0, keepdims=True))
            alpha = jnp.exp2(m_old - m_new)
            p = jnp.exp2(s - m_new)
            l_ref[h] = alpha * l_ref[h] + p.sum(axis=0, keepdims=True)
            pv = jnp.dot(vT_ref[hs, kb0:kb0 + SUB], p.astype(BF16), preferred_element_type=F32)
            acc_ref[h] = alpha * acc_ref[h] + pv
            m_ref[h] = m_new
        else:
            p = jnp.exp2(s)
            l_ref[h] += p.sum(axis=0, keepdims=True)
            acc_ref[h] += jnp.dot(vT_ref[hs, kb0:kb0 + SUB], p.astype(BF16), preferred_element_type=F32)

    @pl.when(kb == (qb * TQ + TQ - 1) // KB)
    def _():
        for h in range(N_HEADS):
            o_ref[:, h * HEAD_DIM:(h + 1) * HEAD_DIM] = (acc_ref[h] / l_ref[h]).T.astype(BF16)


def _attn(nat, tr, keep, *, T, TQ, KB, SUB, online):
    S = nat.shape[0]
    groups = TQ // T
    pairs = [(q, k) for q in range(S // TQ) for k in range((q * TQ + TQ - 1) // KB + 1)]
    qb_ids = jnp.asarray(np.array([p[0] for p in pairs], np.int32))
    kb_ids = jnp.asarray(np.array([p[1] for p in pairs], np.int32))
    grid_spec = pltpu.PrefetchScalarGridSpec(
        num_scalar_prefetch=2,
        grid=(len(pairs),),
        in_specs=[
            pl.BlockSpec((KB, D_ATTN), lambda i, qb, kb: (kb[i], 1)),
            pl.BlockSpec((D_ATTN, TQ), lambda i, qb, kb: (0, qb[i])),
            pl.BlockSpec((D_ATTN, KB), lambda i, qb, kb: (1, kb[i])),
        ] + [pl.BlockSpec((1, KB, T), lambda i, qb, kb, g=g: (qb[i] * groups + g, kb[i], 0)) for g in range(groups)],
        out_specs=pl.BlockSpec((TQ, D_ATTN), lambda i, qb, kb: (qb[i], 0)),
        scratch_shapes=[pltpu.VMEM((N_HEADS, 1, TQ), F32), pltpu.VMEM((N_HEADS, 1, TQ), F32),
                        pltpu.VMEM((N_HEADS, HEAD_DIM, TQ), F32)],
    )
    return pl.pallas_call(
        functools.partial(_attn_kernel, T=T, KB=KB, SUB=SUB, online=online),
        out_shape=jax.ShapeDtypeStruct((S, D_ATTN), BF16),
        grid_spec=grid_spec,
        compiler_params=_params(1),
        name="attn_online" if online else "attn",
    )(qb_ids, kb_ids, nat, tr, tr, *([keep] * groups))


def _outproj_kernel(yp_ref, ya_ref, wp_ref, wa_ref, x_ref, o_ref):
    acc = jnp.dot(yp_ref[...], wp_ref[...], preferred_element_type=F32)
    acc = acc + jnp.dot(ya_ref[...], wa_ref[...], preferred_element_type=F32)
    o_ref[...] = x_ref[...] + acc


def _outproj(yp, ya, w_out, x2, *, tm, tn):
    S = x2.shape[0]
    n_j = D_MODEL // tn
    return pl.pallas_call(
        _outproj_kernel,
        out_shape=jax.ShapeDtypeStruct((S, D_MODEL), F32),
        grid=(S // tm, n_j),
        in_specs=[
            pl.BlockSpec((tm, D_POOL), lambda i, j: (i, 0)),
            pl.BlockSpec((tm, D_ATTN), lambda i, j: (i, 0)),
            pl.BlockSpec((D_POOL, tn), lambda i, j: (0, j)),
            pl.BlockSpec((D_ATTN, tn), lambda i, j: (D_POOL // D_ATTN, j)),
            pl.BlockSpec((tm, tn), lambda i, j: (i, j)),
        ],
        out_specs=pl.BlockSpec((tm, tn), lambda i, j: (i, j)),
        compiler_params=_params(2),
        name="outproj",
    )(yp, ya, w_out, w_out, x2)


def _ffn_kernel(x_ref, halo_ref, g_ref, wg_ref, wv_ref, cwg_ref, cwv_ref, cbg_ref, cbv_ref, wd_ref,
                o_ref, h_ref):
    i = pl.program_id(0)
    f = pl.program_id(1)

    @pl.when(f == 0)
    def _():
        hh = jnp.where(i == 0, 0.0, _rms(halo_ref[...], g_ref[...]))
        h_ref[0:HALO, :] = hh.astype(BF16)
        x = x_ref[...]
        h_ref[HALO:, :] = _rms(x, g_ref[...]).astype(BF16)
        o_ref[...] = x

    h = h_ref[...]

    def conv(w_ref, cw_ref, cb_ref):
        up = jnp.dot(h, w_ref[...], preferred_element_type=F32)
        c = (cw_ref[0:1, :] * pltpu.roll(up, 2, axis=0) + cw_ref[1:2, :] * pltpu.roll(up, 1, axis=0)
             + cw_ref[2:3, :] * up)
        return cb_ref[...] + c[HALO:, :]

    cg = conv(wg_ref, cwg_ref, cbg_ref)
    cv = conv(wv_ref, cwv_ref, cbv_ref)
    act = cg * (1.0 / (1.0 + jnp.exp(-cg))) * cv
    o_ref[...] += jnp.dot(act.astype(BF16), wd_ref[...], preferred_element_type=F32)


def _ffn(x1, g, w_up, conv_w, conv_b, w_down, *, tm, tf):
    S = x1.shape[0]
    n_f = D_FF // tf
    return pl.pallas_call(
        _ffn_kernel,
        out_shape=jax.ShapeDtypeStruct((S, D_MODEL), F32),
        grid=(S // tm, n_f),
        in_specs=[
            pl.BlockSpec((tm, D_MODEL), lambda i, f: (i, 0)),
            pl.BlockSpec((HALO, D_MODEL), lambda i, f: (jnp.maximum(i * (tm // HALO) - 1, 0), 0)),
            pl.BlockSpec((1, D_MODEL), lambda i, f: (0, 0)),
            pl.BlockSpec((D_MODEL, tf), lambda i, f: (0, f)),
            pl.BlockSpec((D_MODEL, tf), lambda i, f: (0, f + n_f)),
            pl.BlockSpec((CONV_WIDTH, tf), lambda i, f: (0, f)),
            pl.BlockSpec((CONV_WIDTH, tf), lambda i, f: (0, f + n_f)),
            pl.BlockSpec((1, tf), lambda i, f: (0, f)),
            pl.BlockSpec((1, tf), lambda i, f: (0, f + n_f)),
            pl.BlockSpec((tf, D_MODEL), lambda i, f: (f, 0)),
        ],
        out_specs=pl.BlockSpec((tm, D_MODEL), lambda i, f: (i, 0)),
        scratch_shapes=[pltpu.VMEM((HALO + tm, D_MODEL), BF16)],
        compiler_params=_params(2),
        name="ffn",
    )(x1, x1, g, w_up, w_up, conv_w, conv_w, conv_b, conv_b, w_down)


def kernel(x, attn_norm_g, w_in, pool_w, pool_scale, q_norm_g, k_norm_g, w_out, ffn_norm_g, w_up,
           conv_w, conv_b, w_down):
    B, S, D = x.shape
    assert B == 1 and D == D_MODEL and w_in.shape == (D_MODEL, D_MAIN + D_TAIL)
    t = TILES
    assert all(S % n == 0 for n in (t.inproj_rows, t.pool_rows, t.outproj_rows, t.ffn_rows, t.attn_keys))
    topk = min(TOPK_MAX, S // 4)
    q_scale = HEAD_DIM ** -0.5 * LOG2E
    x2 = x.reshape(S, D)

    w_main = w_in[:, :D_MAIN].astype(BF16)
    w_tail = jnp.pad(w_in[:, D_MAIN:], ((0, 0), (0, TAIL_PAD - D_TAIL))).astype(BF16)
    nat, tr, kidx, wT = _inproj(x2, attn_norm_g.reshape(1, D), w_main, w_tail,
                                k_norm_g.reshape(1, HEAD_DIM), q_norm_g.reshape(HEAD_DIM, 1),
                                tm=t.inproj_rows, tn=t.inproj_cols, q_scale=q_scale)

    y_pool = _pool(nat, pool_w.astype(BF16), pool_scale.reshape(1, D_POOL), tm=t.pool_rows)
    keep = _index(kidx, tr, wT, T=t.dsa_queries, CK=t.index_keys, topk=topk)
    score_bound = HEAD_DIM * jnp.max(jnp.abs(q_norm_g)) * jnp.max(jnp.abs(k_norm_g)) * q_scale
    attn = functools.partial(_attn, T=t.dsa_queries, TQ=t.attn_queries, KB=t.attn_keys, SUB=t.attn_sub_keys)
    y_attn = lax.cond(score_bound < MAX_UNSHIFTED_LOG2,
                      functools.partial(attn, online=False), functools.partial(attn, online=True),
                      nat, tr, keep)

    x1 = _outproj(y_pool, y_attn, w_out.astype(BF16), x2, tm=t.outproj_rows, tn=t.outproj_cols)
    out = _ffn(x1, ffn_norm_g.reshape(1, D), w_up.astype(BF16), conv_w, conv_b.reshape(1, 2 * D_FF),
               w_down.astype(BF16), tm=t.ffn_rows, tf=t.ffn_cols)
    return out.reshape(B, S, D)
```
